```python
import math
import jax, jax.numpy as jnp
from jax import lax
import numpy as np

D_MODEL = 2048
BATCH = 8
SEQ = 2048
DEPTH = 1

CHUNK = 64
Q_BLOCK = 128
EPS = 1e-6
N_HEADS = 16
QK_NOPE_DIM = 128
QK_ROPE_DIM = 64
QK_HEAD_DIM = QK_NOPE_DIM + QK_ROPE_DIM
V_HEAD_DIM = 128
Q_LORA_RANK = 512
KV_LORA_RANK = 256
ROPE_THETA = 10000.0
S5_WIDTH = 1024
S5_GROUP = 16
S5_GROUPS = S5_WIDTH // S5_GROUP
S5_STATE = 64
DT_MIN = 1e-3
DT_MAX = 1e-1
N_EXPERTS = 32
TOP_K = 4
D_FF = 2048
SWIGLU_LIMIT = 7.0
SWIGLU_ALPHA = 1.702
MOE_BLOCK = 128

N_BRANCH = 2
IN_PROJ_DIM = Q_LORA_RANK + KV_LORA_RANK + QK_ROPE_DIM + S5_WIDTH + N_BRANCH * D_MODEL

kernel_name = 'hybrid_mla_s5_gated_moe_block'


def rms_norm(x, g):
    xf = x.astype(jnp.float32)
    y = xf * lax.rsqrt(jnp.mean(xf * xf, axis=-1, keepdims=True) + EPS)
    return y.astype(x.dtype) * g


def rope(x, positions):
    half = QK_ROPE_DIM // 2
    inv_freq = ROPE_THETA ** (-jnp.arange(half, dtype=jnp.float32) / half)
    ang = positions.astype(jnp.float32)[..., None] * inv_freq
    cos = jnp.cos(ang)[:, :, None, :]
    sin = jnp.sin(ang)[:, :, None, :]
    xf = x.astype(jnp.float32)
    x1, x2 = xf[..., :half], xf[..., half:]
    out = jnp.concatenate([x1 * cos - x2 * sin, x2 * cos + x1 * sin], axis=-1)
    return out.astype(x.dtype)


def chunk_causal_attention(q, k, v):
    b, s, h, dh = q.shape
    nqb = s // Q_BLOCK
    scale = 1.0 / math.sqrt(dh)
    qb = q.reshape(b, nqb, Q_BLOCK, h, dh).transpose(1, 0, 2, 3, 4)
    key_chunk = jnp.arange(s) // CHUNK

    def one_block(args):
        q_blk, blk = args
        q_chunk = (blk * Q_BLOCK + jnp.arange(Q_BLOCK)) // CHUNK
        allowed = key_chunk[None, :] <= q_chunk[:, None]
        scores = jnp.einsum('bqhd,bkhd->bhqk', q_blk, k).astype(jnp.float32) * scale
        scores = jnp.where(allowed[None, None], scores, -jnp.inf)
        p = jax.nn.softmax(scores, axis=-1).astype(v.dtype)
        return jnp.einsum('bhqk,bkhd->bqhd', p, v)

    out = lax.map(one_block, (qb, jnp.arange(nqb)))
    return out.transpose(1, 0, 2, 3, 4).reshape(b, s, h, v.shape[-1])


def mla_branch(q_lat, kv_lat, k_rope, positions, q_norm_g, w_uq, kv_norm_g, w_ukv, qk_q_g, qk_k_g, w_o):
    b, s, _ = q_lat.shape
    q = (rms_norm(q_lat, q_norm_g) @ w_uq).reshape(b, s, N_HEADS, QK_HEAD_DIM)
    kv = (rms_norm(kv_lat, kv_norm_g) @ w_ukv).reshape(b, s, N_HEADS, QK_NOPE_DIM + V_HEAD_DIM)
    k_nope, v = kv[..., :QK_NOPE_DIM], kv[..., QK_NOPE_DIM:]
    k_r = jnp.broadcast_to(k_rope[:, :, None, :], (b, s, N_HEADS, QK_ROPE_DIM))
    k = jnp.concatenate([k_nope, k_r], axis=-1)
    q = rms_norm(q, qk_q_g)
    k = rms_norm(k, qk_k_g)
    q = jnp.concatenate([q[..., :QK_NOPE_DIM], rope(q[..., QK_NOPE_DIM:], positions)], axis=-1)
    k = jnp.concatenate([k[..., :QK_NOPE_DIM], rope(k[..., QK_NOPE_DIM:], positions)], axis=-1)
    o = chunk_causal_attention(q, k, v).reshape(b, s, N_HEADS * V_HEAD_DIM)
    return o @ w_o


def s5_branch(u, lam_re, lam_im, log_dt, b_re, b_im, c_re, c_im, d, w_glu, w_o):
    b, s, _ = u.shape
    f32 = jnp.float32
    uf = u.astype(f32)
    ug = uf.reshape(b, s, S5_GROUPS, S5_GROUP)
    dt = jnp.exp(log_dt.astype(f32))[:, None]
    lr, li = lam_re.astype(f32), lam_im.astype(f32)
    mag = jnp.exp(lr * dt)
    ar, ai = mag * jnp.cos(li * dt), mag * jnp.sin(li * dt)
    den = lr * lr + li * li
    zr = ((ar - 1.0) * lr + ai * li) / den
    zi = (ai * lr - (ar - 1.0) * li) / den
    br, bi = b_re.astype(f32), b_im.astype(f32)
    bbr = zr[..., None] * br - zi[..., None] * bi
    bbi = zr[..., None] * bi + zi[..., None] * br
    bu_re = jnp.einsum('bsgc,gpc->bsgp', ug, bbr)
    bu_im = jnp.einsum('bsgc,gpc->bsgp', ug, bbi)
    a_re = jnp.broadcast_to(ar[None, None], (1, s, S5_GROUPS, S5_STATE))
    a_im = jnp.broadcast_to(ai[None, None], (1, s, S5_GROUPS, S5_STATE))

    def combine(left, right):
        a1r, a1i, b1r, b1i = left
        a2r, a2i, b2r, b2i = right
        return (a2r * a1r - a2i * a1i,
                a2r * a1i + a2i * a1r,
                a2r * b1r - a2i * b1i + b2r,
                a2r * b1i + a2i * b1r + b2i)

    _, _, xr, xi = lax.associative_scan(combine, (a_re, a_im, bu_re, bu_im), axis=1)
    y = (jnp.einsum('gcp,bsgp->bsgc', c_re.astype(f32), xr)
         - jnp.einsum('gcp,bsgp->bsgc', c_im.astype(f32), xi))
    y = (y.reshape(b, s, S5_WIDTH) + d.astype(f32) * uf).astype(u.dtype)
    g = jax.nn.gelu(y)
    y = g * jax.nn.sigmoid(g @ w_glu)
    return y @ w_o


def moe(h, w_router, b_router, w_gate, b_gate, w_up, b_up, w_down, b_down):
    bsz, s, d = h.shape
    t = bsz * s
    xf = h.reshape(t, d)
    logits = (xf @ w_router + b_router).astype(jnp.float32)
    top_vals, top_idx = lax.top_k(logits, TOP_K)
    weights = jax.nn.softmax(top_vals, axis=-1)
    n_assign = t * TOP_K
    flat_e = top_idx.reshape(-1)
    flat_tok = jnp.repeat(jnp.arange(t, dtype=jnp.int32), TOP_K)
    flat_w = weights.reshape(-1)
    order = jnp.argsort(flat_e)
    se, stok, sw = flat_e[order], flat_tok[order], flat_w[order]
    counts = jnp.zeros((N_EXPERTS,), jnp.int32).at[flat_e].add(1)
    starts = jnp.cumsum(counts) - counts
    padded = (counts + MOE_BLOCK - 1) // MOE_BLOCK * MOE_BLOCK
    pad_end = jnp.cumsum(padded)
    pad_start = pad_end - padded
    dest = pad_start[se] + jnp.arange(n_assign, dtype=jnp.int32) - starts[se]
    n_blocks = n_assign // MOE_BLOCK + N_EXPERTS
    n_slots = n_blocks * MOE_BLOCK
    slot_tok = jnp.full((n_slots,), t, jnp.int32).at[dest].set(stok)
    slot_w = jnp.zeros((n_slots,), jnp.float32).at[dest].set(sw)
    block_expert = jnp.minimum(
        jnp.searchsorted(pad_end, jnp.arange(n_blocks, dtype=jnp.int32) * MOE_BLOCK, side='right'),
        N_EXPERTS - 1)
    xpad = jnp.concatenate([xf, jnp.zeros((1, d), xf.dtype)], axis=0)
    xb = xpad[slot_tok].reshape(n_blocks, MOE_BLOCK, d)

    def expert_block(args):
        xblk, e = args
        gate = xblk @ w_gate[e] + b_gate[e]
        up = xblk @ w_up[e] + b_up[e]
        gate = jnp.minimum(gate, SWIGLU_LIMIT)
        up = jnp.clip(up, -SWIGLU_LIMIT, SWIGLU_LIMIT)
        glu = gate * jax.nn.sigmoid(SWIGLU_ALPHA * gate)
        return ((up + 1.0) * glu) @ w_down[e] + b_down[e]

    yb = lax.map(expert_block, (xb, block_expert)).reshape(n_slots, d)
    contrib = yb * slot_w[:, None].astype(yb.dtype)
    y = jnp.zeros((t + 1, d), yb.dtype).at[slot_tok].add(contrib)[:t]
    return y.reshape(bsz, s, d)


def setup_inputs(seed: int = 0) -> dict:
    key = jax.random.key(seed)
    ks = jax.random.split(key, 40)
    L = DEPTH
    f32 = jnp.float32

    def nrm(k, shape, scale):
        return jax.random.normal(k, shape, f32) * scale

    def gain(k, shape):
        return 1.0 + 0.01 * jax.random.normal(k, shape, f32)

    x = jax.random.normal(ks[0], (BATCH, SEQ, D_MODEL), f32)
    positions = (jax.random.randint(ks[1], (BATCH, 1), 0, 4096, dtype=jnp.int32)
                 + jnp.arange(SEQ, dtype=jnp.int32)[None, :])
    n_idx = jnp.arange(S5_STATE, dtype=f32)
    lam_re = -0.5 + 1e-3 * jax.random.normal(ks[12], (L, S5_GROUPS, S5_STATE), f32)
    lam_im = math.pi * n_idx[None, None, :] + 1e-3 * jax.random.normal(ks[13], (L, S5_GROUPS, S5_STATE), f32)
    log_dt = jax.random.uniform(ks[14], (L, S5_GROUPS), f32, math.log(DT_MIN), math.log(DT_MAX))
    return {
        'x': x,
        'positions': positions,
        'norm1_g': gain(ks[2], (L, D_MODEL)),
        'w_in': nrm(ks[3], (L, D_MODEL, IN_PROJ_DIM), D_MODEL ** -0.5),
        'b_gates': nrm(ks[4], (L, N_BRANCH * D_MODEL), 0.01),
        'q_norm_g': gain(ks[5], (L, Q_LORA_RANK)),
        'w_uq': nrm(ks[6], (L, Q_LORA_RANK, N_HEADS * QK_HEAD_DIM), Q_LORA_RANK ** -0.5),
        'kv_norm_g': gain(ks[7], (L, KV_LORA_RANK)),
        'w_ukv': nrm(ks[8], (L, KV_LORA_RANK, N_HEADS * (QK_NOPE_DIM + V_HEAD_DIM)), KV_LORA_RANK ** -0.5),
        'qk_norm_q_g': gain(ks[9], (L, QK_HEAD_DIM)),
        'qk_norm_k_g': gain(ks[10], (L, QK_HEAD_DIM)),
        'w_o_mla': nrm(ks[11], (L, N_HEADS * V_HEAD_DIM, D_MODEL), (N_HEADS * V_HEAD_DIM) ** -0.5),
        's5_lambda_re': lam_re,
        's5_lambda_im': lam_im,
        's5_log_dt': log_dt,
        's5_b_re': nrm(ks[15], (L, S5_GROUPS, S5_STATE, S5_GROUP), (2 * S5_GROUP) ** -0.5),
        's5_b_im': nrm(ks[16], (L, S5_GROUPS, S5_STATE, S5_GROUP), (2 * S5_GROUP) ** -0.5),
        's5_c_re': nrm(ks[17], (L, S5_GROUPS, S5_GROUP, S5_STATE), (S5_STATE / 2) ** -0.5),
        's5_c_im': nrm(ks[18], (L, S5_GROUPS, S5_GROUP, S5_STATE), (S5_STATE / 2) ** -0.5),
        's5_d': nrm(ks[19], (L, S5_WIDTH), 1.0),
        'w_glu': nrm(ks[20], (L, S5_WIDTH, S5_WIDTH), S5_WIDTH ** -0.5),
        'w_o_s5': nrm(ks[21], (L, S5_WIDTH, D_MODEL), S5_WIDTH ** -0.5),
        'w_out': nrm(ks[22], (L, D_MODEL, D_MODEL), D_MODEL ** -0.5),
        'norm2_g': gain(ks[23], (L, D_MODEL)),
        'w_router': nrm(ks[24], (L, D_MODEL, N_EXPERTS), D_MODEL ** -0.5),
        'b_router': nrm(ks[25], (L, N_EXPERTS), 0.01),
        'w_gate': nrm(ks[26], (L, N_EXPERTS, D_MODEL, D_FF), D_MODEL ** -0.5),
        'b_gate': nrm(ks[27], (L, N_EXPERTS, D_FF), 0.01),
        'w_up': nrm(ks[28], (L, N_EXPERTS, D_MODEL, D_FF), D_MODEL ** -0.5),
        'b_up': nrm(ks[29], (L, N_EXPERTS, D_FF), 0.01),
        'w_down': nrm(ks[30], (L, N_EXPERTS, D_FF, D_MODEL), D_FF ** -0.5),
        'b_down': nrm(ks[31], (L, N_EXPERTS, D_MODEL), 0.01),
    }


def reference(x, positions, norm1_g, w_in, b_gates, q_norm_g, w_uq, kv_norm_g, w_ukv,
              qk_norm_q_g, qk_norm_k_g, w_o_mla, s5_lambda_re, s5_lambda_im, s5_log_dt,
              s5_b_re, s5_b_im, s5_c_re, s5_c_im, s5_d, w_glu, w_o_s5, w_out, norm2_g,
              w_router, b_router, w_gate, b_gate, w_up, b_up, w_down, b_down):
    o1 = Q_LORA_RANK
    o2 = o1 + KV_LORA_RANK
    o3 = o2 + QK_ROPE_DIM
    o4 = o3 + S5_WIDTH
    for l in range(DEPTH):
        h = rms_norm(x, norm1_g[l])
        proj = h @ w_in[l]
        q_lat, kv_lat, k_rope = proj[..., :o1], proj[..., o1:o2], proj[..., o2:o3]
        u, gate_logits = proj[..., o3:o4], proj[..., o4:]
        y_a = mla_branch(q_lat, kv_lat, k_rope, positions, q_norm_g[l], w_uq[l], kv_norm_g[l],
                         w_ukv[l], qk_norm_q_g[l], qk_norm_k_g[l], w_o_mla[l])
        y_b = s5_branch(u, s5_lambda_re[l], s5_lambda_im[l], s5_log_dt[l], s5_b_re[l], s5_b_im[l],
                        s5_c_re[l], s5_c_im[l], s5_d[l], w_glu[l], w_o_s5[l])
        gates = jax.nn.sigmoid(gate_logits + b_gates[l])
        g_a, g_b = gates[..., :D_MODEL], gates[..., D_MODEL:]
        x = x + (g_a * y_a + g_b * y_b) @ w_out[l]
        h2 = rms_norm(x, norm2_g[l])
        x = x + moe(h2, w_router[l], b_router[l], w_gate[l], b_gate[l], w_up[l], b_up[l],
                    w_down[l], b_down[l])
    return x
```

```python
import functools
import math

import numpy as np
import jax
import jax.numpy as jnp
from jax import lax
from jax.experimental import pallas as pl
from jax.experimental.pallas import tpu as pltpu

F32 = jnp.float32
BF16 = jnp.bfloat16

D_MODEL = 2048
CHUNK = 64
EPS = 1e-6
N_HEADS = 16
QK_NOPE_DIM = 128
QK_ROPE_DIM = 64
QK_HEAD_DIM = QK_NOPE_DIM + QK_ROPE_DIM
V_HEAD_DIM = 128
Q_LORA_RANK = 512
KV_LORA_RANK = 256
ROPE_THETA = 10000.0
S5_WIDTH = 1024
S5_GROUP = 16
S5_GROUPS = S5_WIDTH // S5_GROUP
S5_STATE = 64
N_EXPERTS = 32
TOP_K = 4
D_FF = 2048
SWIGLU_LIMIT = 7.0
SWIGLU_ALPHA = 1.702

LANES = 128
HEAD_PAD = 256
ROPE_HALF = QK_ROPE_DIM // 2
S5_CLUSTER = 8
N_CLUSTERS = S5_GROUPS // S5_CLUSTER
CLUSTER_STATES = S5_CLUSTER * S5_STATE
SUB_T = 32
PROJ_W = 6144
VMEM_LIMIT = 56 * 1024 * 1024
NEG = -1e30


def _cparams(sem, vmem=VMEM_LIMIT, **kw):
    return pltpu.CompilerParams(dimension_semantics=sem, vmem_limit_bytes=vmem, **kw)


def _inproj_kernel(x_ref, g_ref, w_ref, o_ref, h_ref):
    @pl.when(pl.program_id(1) == 0)
    def _():
        x = x_ref[...]
        ms = jnp.mean(x * x, axis=-1, keepdims=True)
        h_ref[...] = (x * lax.rsqrt(ms + EPS) * g_ref[...]).astype(BF16)

    o_ref[...] = jnp.dot(h_ref[...], w_ref[...], preferred_element_type=F32).astype(o_ref.dtype)


def _inproj(x2, g, w, *, tm=1024, tn=1536):
    t, d = x2.shape
    n = w.shape[1]
    return pl.pallas_call(
        _inproj_kernel,
        grid=(t // tm, n // tn),
        in_specs=[
            pl.BlockSpec((tm, d), lambda i, j: (i, 0)),
            pl.BlockSpec((1, d), lambda i, j: (0, 0)),
            pl.BlockSpec((d, tn), lambda i, j: (0, j)),
        ],
        out_specs=pl.BlockSpec((tm, tn), lambda i, j: (i, j)),
        out_shape=jax.ShapeDtypeStruct((t, n), BF16),
        scratch_shapes=[pltpu.VMEM((tm, d), BF16)],
        compiler_params=_cparams(("parallel", "arbitrary")),
        name="inproj",
    )(x2, g, w)


def _rope_mid(mid, cos, sin_signed):
    lane = lax.broadcasted_iota(jnp.int32, mid.shape, 1)
    rot = jnp.where(lane < ROPE_HALF, pltpu.roll(mid, LANES - ROPE_HALF, 1), pltpu.roll(mid, ROPE_HALF, 1))
    return mid * cos + rot * sin_signed


def _qprep_kernel(ql_ref, pos_ref, gn_ref, w_ref, gfull_ref, ind_ref, freq_ref, sign_ref, o_ref):
    ql = ql_ref[...].astype(F32)
    ms = jnp.mean(ql * ql, axis=-1, keepdims=True)
    qn = (ql * lax.rsqrt(ms + EPS) * gn_ref[...]).astype(BF16)
    q = jnp.dot(qn, w_ref[...], preferred_element_type=F32)
    ssq = jnp.dot((q * q).astype(BF16), ind_ref[...], preferred_element_type=F32)
    sc = lax.rsqrt(ssq * (1.0 / QK_HEAD_DIM) + EPS)
    ang = pos_ref[...] * freq_ref[...]
    cos = jnp.cos(ang)
    sin_signed = jnp.sin(ang) * sign_ref[...]
    for h in range(N_HEADS):
        s_h = sc[:, h:h + 1]
        lo = h * HEAD_PAD
        nope = q[:, lo:lo + LANES] * s_h * gfull_ref[:, lo:lo + LANES]
        mid = q[:, lo + LANES:lo + HEAD_PAD] * s_h * gfull_ref[:, lo + LANES:lo + HEAD_PAD]
        o_ref[:, lo:lo + LANES] = nope.astype(o_ref.dtype)
        o_ref[:, lo + LANES:lo + HEAD_PAD] = _rope_mid(mid, cos, sin_signed).astype(o_ref.dtype)


def _qprep(proj, pos, gn, w, gfull, ind, freq, sign, *, tm=512):
    t = proj.shape[0]
    n = N_HEADS * HEAD_PAD
    full = lambda shape: pl.BlockSpec(shape, lambda i: (0, 0))
    return pl.pallas_call(
        _qprep_kernel,
        grid=(t // tm,),
        in_specs=[
            pl.BlockSpec((tm, Q_LORA_RANK), lambda i: (i, 10)),
            pl.BlockSpec((tm, 1), lambda i: (i, 0)),
            full((1, Q_LORA_RANK)),
            full((Q_LORA_RANK, n)),
            full((1, n)),
            full((n, LANES)),
            full((1, LANES)),
            full((1, LANES)),
        ],
        out_specs=pl.BlockSpec((tm, n), lambda i: (i, 0)),
        out_shape=jax.ShapeDtypeStruct((t, n), BF16),
        compiler_params=_cparams(("parallel",)),
        name="qprep",
    )(proj, pos, gn, w, gfull, ind, freq, sign)


def _kprep_kernel(kvl_ref, kr_ref, pos_ref, gn_ref, w_ref, gnope_ref, grope_ref, ind_ref, freq_ref, sign_ref,
                  k_ref, v_ref):
    kvl = kvl_ref[...].astype(F32)
    ms = jnp.mean(kvl * kvl, axis=-1, keepdims=True)
    kn = (kvl * lax.rsqrt(ms + EPS) * gn_ref[...]).astype(BF16)
    kv = jnp.dot(kn, w_ref[...], preferred_element_type=F32)
    nd = N_HEADS * QK_NOPE_DIM
    knope = kv[:, :nd]
    v_ref[...] = kv[:, nd:].astype(v_ref.dtype)
    kr = kr_ref[...].astype(F32)
    ssq = jnp.dot((knope * knope).astype(BF16), ind_ref[...], preferred_element_type=F32)
    ssq = ssq + jnp.sum(kr * kr, axis=-1, keepdims=True)
    sc = lax.rsqrt(ssq * (1.0 / QK_HEAD_DIM) + EPS)
    ang = pos_ref[...] * freq_ref[...]
    kr_rot = _rope_mid(kr * grope_ref[...], jnp.cos(ang), jnp.sin(ang) * sign_ref[...])
    for h in range(N_HEADS):
        s_h = sc[:, h:h + 1]
        lo = h * HEAD_PAD
        nope = knope[:, h * LANES:(h + 1) * LANES] * s_h * gnope_ref[...]
        k_ref[:, lo:lo + LANES] = nope.astype(k_ref.dtype)
        k_ref[:, lo + LANES:lo + HEAD_PAD] = (kr_rot * s_h).astype(k_ref.dtype)


def _kprep(proj, pos, gn, w, gnope, grope, ind, freq, sign, *, tm=512):
    t = proj.shape[0]
    n = N_HEADS * HEAD_PAD
    nd = N_HEADS * QK_NOPE_DIM
    full = lambda shape: pl.BlockSpec(shape, lambda i: (0, 0))
    return pl.pallas_call(
        _kprep_kernel,
        grid=(t // tm,),
        in_specs=[
            pl.BlockSpec((tm, KV_LORA_RANK), lambda i: (i, 22)),
            pl.BlockSpec((tm, LANES), lambda i: (i, 46)),
            pl.BlockSpec((tm, 1), lambda i: (i, 0)),
            full((1, KV_LORA_RANK)),
            full((KV_LORA_RANK, 2 * nd)),
            full((1, LANES)),
            full((1, LANES)),
            full((nd, LANES)),
            full((1, LANES)),
            full((1, LANES)),
        ],
        out_specs=[pl.BlockSpec((tm, n), lambda i: (i, 0)), pl.BlockSpec((tm, nd), lambda i: (i, 0))],
        out_shape=[jax.ShapeDtypeStruct((t, n), BF16), jax.ShapeDtypeStruct((t, nd), BF16)],
        compiler_params=_cparams(("parallel",)),
        name="kprep",
    )(proj, proj, pos, gn, w, gnope, grope, ind, freq, sign)


def _attn_kernel(q_ref, k_ref, v_ref, o_ref, m_ref, l_ref, acc_ref, *, tq):
    qi = pl.program_id(2)
    q = q_ref[...]
    m_ref[...] = jnp.full(m_ref.shape, NEG, F32)
    l_ref[...] = jnp.zeros(l_ref.shape, F32)
    acc_ref[...] = jnp.zeros(acc_ref.shape, F32)

    def block(j, masked):
        r0 = pl.multiple_of(j * tq, tq)
        kb = k_ref[pl.ds(r0, tq), :]
        vb = v_ref[pl.ds(r0, tq), :]
        s = lax.dot_general(q, kb, (((1,), (1,)), ((), ())), preferred_element_type=F32)
        if masked:
            row = lax.broadcasted_iota(jnp.int32, s.shape, 0) // CHUNK
            col = lax.broadcasted_iota(jnp.int32, s.shape, 1) // CHUNK
            s = jnp.where(col <= row, s, NEG)
        m_old = m_ref[...]
        m_new = jnp.maximum(m_old, jnp.max(s, axis=-1, keepdims=True))
        alpha = jnp.exp(m_old - m_new)
        p = jnp.exp(s - m_new)
        l_ref[...] = alpha * l_ref[...] + jnp.sum(p, axis=-1, keepdims=True)
        acc_ref[...] = alpha * acc_ref[...] + jnp.dot(p.astype(BF16), vb, preferred_element_type=F32)
        m_ref[...] = m_new

    def body(j, c):
        block(j, False)
        return c

    lax.fori_loop(0, qi, body, 0)
    block(qi, True)
    o_ref[...] = (acc_ref[...] / l_ref[...]).astype(o_ref.dtype)


def _attention(q3, k3, v3, *, tq=512):
    b, s, _ = q3.shape
    tq = min(tq, s)
    return pl.pallas_call(
        functools.partial(_attn_kernel, tq=tq),
        grid=(b, N_HEADS, s // tq),
        in_specs=[
            pl.BlockSpec((None, tq, HEAD_PAD), lambda bi, h, i: (bi, i, h)),
            pl.BlockSpec((None, s, HEAD_PAD), lambda bi, h, i: (bi, 0, h)),
            pl.BlockSpec((None, s, V_HEAD_DIM), lambda bi, h, i: (bi, 0, h)),
        ],
        out_specs=pl.BlockSpec((None, tq, V_HEAD_DIM), lambda bi, h, i: (bi, i, h)),
        out_shape=jax.ShapeDtypeStruct((b, s, N_HEADS * V_HEAD_DIM), BF16),
        scratch_shapes=[pltpu.VMEM((tq, 1), F32), pltpu.VMEM((tq, 1), F32), pltpu.VMEM((tq, V_HEAD_DIM), F32)],
        compiler_params=_cparams(("parallel", "parallel", "arbitrary")),
        name="attention",
    )(q3, k3, v3)


def _s5_kernel(u_ref, perm_ref, permt_ref, bb_ref, cc_ref, a_ref, d_ref, wglu_ref, wo_ref, o_ref,
               ut_ref, bu_ref, y_ref, carry_ref, *, tc):
    nb = u_ref.shape[0]
    rows_sub = nb * SUB_T
    nsub = tc // SUB_T

    @pl.when(pl.program_id(0) == 0)
    def _():
        carry_ref[...] = jnp.zeros(carry_ref.shape, F32)

    for j in range(nsub):
        ub = u_ref[:, j * SUB_T:(j + 1) * SUB_T, :].reshape(rows_sub, S5_WIDTH)
        ut_ref[j * rows_sub:(j + 1) * rows_sub, :] = jnp.dot(
            perm_ref[...], ub, preferred_element_type=F32).astype(BF16)

    for c in range(N_CLUSTERS):
        bu_ref[...] = jnp.dot(ut_ref[:, c * LANES:(c + 1) * LANES], bb_ref[c], preferred_element_type=F32)
        ar = a_ref[c, :, :CLUSTER_STATES]
        ai = a_ref[c, :, CLUSTER_STATES:]

        def step(t, carry):
            xr, xi = carry
            r0 = pl.multiple_of(t * nb, nb)
            br = bu_ref[pl.ds(r0, nb), :CLUSTER_STATES]
            bi = bu_ref[pl.ds(r0, nb), CLUSTER_STATES:]
            nxr = ar * xr - ai * xi + br
            nxi = ar * xi + ai * xr + bi
            bu_ref[pl.ds(r0, nb), :CLUSTER_STATES] = nxr
            bu_ref[pl.ds(r0, nb), CLUSTER_STATES:] = nxi
            return nxr, nxi

        xr, xi = lax.fori_loop(0, tc, step, (carry_ref[c, :, :CLUSTER_STATES], carry_ref[c, :, CLUSTER_STATES:]),
                               unroll=8)
        carry_ref[c, :, :CLUSTER_STATES] = xr
        carry_ref[c, :, CLUSTER_STATES:] = xi
        y_ref[:, c * LANES:(c + 1) * LANES] = jnp.dot(bu_ref[...].astype(BF16), cc_ref[c], preferred_element_type=F32)

    y = y_ref[...] + d_ref[...] * ut_ref[...].astype(F32)
    g = jax.nn.gelu(y)
    z = g * jax.nn.sigmoid(jnp.dot(g.astype(BF16), wglu_ref[...], preferred_element_type=F32))
    yb = jnp.dot(z.astype(BF16), wo_ref[...], preferred_element_type=F32).astype(BF16)
    for j in range(nsub):
        blk = jnp.dot(permt_ref[...], yb[j * rows_sub:(j + 1) * rows_sub, :], preferred_element_type=F32)
        o_ref[:, j * SUB_T:(j + 1) * SUB_T, :] = blk.astype(o_ref.dtype).reshape(nb, SUB_T, o_ref.shape[2])


def _s5(proj3, perm, permt, bb, cc, a_bc, d, wglu, wo, *, tc=64):
    b, s, _ = proj3.shape
    rows = b * tc
    dm = wo.shape[1]
    c2 = lambda shape: pl.BlockSpec(shape, lambda i: (0, 0))
    c3 = lambda shape: pl.BlockSpec(shape, lambda i: (0, 0, 0))
    return pl.pallas_call(
        functools.partial(_s5_kernel, tc=tc),
        grid=(s // tc,),
        in_specs=[
            pl.BlockSpec((b, tc, S5_WIDTH), lambda i: (0, i, 4)),
            c2(perm.shape), c2(permt.shape), c3(bb.shape), c3(cc.shape), c3(a_bc.shape), c2(d.shape),
            c2(wglu.shape), c2(wo.shape),
        ],
        out_specs=pl.BlockSpec((b, tc, dm), lambda i: (0, i, 0)),
        out_shape=jax.ShapeDtypeStruct((b, s, dm), BF16),
        scratch_shapes=[
            pltpu.VMEM((rows, S5_WIDTH), BF16),
            pltpu.VMEM((rows, 2 * CLUSTER_STATES), F32),
            pltpu.VMEM((rows, S5_WIDTH), F32),
            pltpu.VMEM((N_CLUSTERS, b, 2 * CLUSTER_STATES), F32),
        ],
        compiler_params=_cparams(("arbitrary",)),
        name="s5",
    )(proj3, perm, permt, bb, cc, a_bc, d, wglu, wo)


def _merge_kernel(o_ref, gl_ref, bg_ref, yb_ref, w_ref, m_ref):
    d = o_ref.shape[1]
    ya = jnp.dot(o_ref[...], w_ref[...], preferred_element_type=F32)
    ga = jax.nn.sigmoid(gl_ref[:, :d].astype(F32) + bg_ref[:, :d])
    gb = jax.nn.sigmoid(gl_ref[:, d:].astype(F32) + bg_ref[:, d:])
    m_ref[...] = (ga * ya + gb * yb_ref[...].astype(F32)).astype(m_ref.dtype)


def _merge(o2, proj, bg, yb2, w, *, tm=512):
    t, d = o2.shape
    return pl.pallas_call(
        _merge_kernel,
        grid=(t // tm,),
        in_specs=[
            pl.BlockSpec((tm, d), lambda i: (i, 0)),
            pl.BlockSpec((tm, 2 * d), lambda i: (i, 0)),
            pl.BlockSpec((1, 2 * d), lambda i: (0, 0)),
            pl.BlockSpec((tm, d), lambda i: (i, 0)),
            pl.BlockSpec((d, d), lambda i: (0, 0)),
        ],
        out_specs=pl.BlockSpec((tm, d), lambda i: (i, 0)),
        out_shape=jax.ShapeDtypeStruct((t, d), BF16),
        compiler_params=_cparams(("parallel",)),
        name="merge",
    )(o2, proj, bg, yb2, w)


def _outproj_kernel(m_ref, x_ref, w_ref, g_ref, wrh_ref, wrl_ref, br_ref, x1_ref, h2_ref, idx_ref, wt_ref):
    x1 = x_ref[...] + jnp.dot(m_ref[...], w_ref[...], preferred_element_type=F32)
    x1_ref[...] = x1
    ms = jnp.mean(x1 * x1, axis=-1, keepdims=True)
    h2 = x1 * lax.rsqrt(ms + EPS) * g_ref[...]
    h_hi = h2.astype(BF16)
    h2_ref[...] = h_hi.astype(h2_ref.dtype)
    h_lo = (h2 - h_hi.astype(F32)).astype(BF16)
    logits = (jnp.dot(h_hi, wrh_ref[...], preferred_element_type=F32)
              + jnp.dot(h_lo, wrh_ref[...], preferred_element_type=F32)
              + jnp.dot(h_hi, wrl_ref[...], preferred_element_type=F32)) + br_ref[...]
    lane = lax.broadcasted_iota(jnp.int32, logits.shape, 1)
    work = jnp.where(lane < N_EXPERTS, logits, -jnp.inf)
    idx_out = jnp.zeros(logits.shape, jnp.int32)
    val_out = jnp.zeros(logits.shape, F32)
    v0 = None
    denom = None
    for k in range(TOP_K):
        mx = jnp.max(work, axis=-1, keepdims=True)
        sel = jnp.min(jnp.where(work == mx, lane, LANES), axis=-1, keepdims=True)
        if k == 0:
            v0 = mx
        e = jnp.exp(mx - v0)
        denom = e if k == 0 else denom + e
        idx_out = jnp.where(lane == k, sel, idx_out)
        val_out = jnp.where(lane == k, e, val_out)
        work = jnp.where(lane == sel, -jnp.inf, work)
    idx_ref[...] = idx_out
    wt_ref[...] = val_out / denom


def _outproj(m2, x2, w, g, wrh, wrl, br, *, tm=512):
    t, d = x2.shape
    c2 = lambda shape: pl.BlockSpec(shape, lambda i: (0, 0))
    row = lambda width: pl.BlockSpec((tm, width), lambda i: (i, 0))
    return pl.pallas_call(
        _outproj_kernel,
        grid=(t // tm,),
        in_specs=[row(d), row(d), c2((d, d)), c2((1, d)), c2((d, LANES)), c2((d, LANES)), c2((1, LANES))],
        out_specs=[row(d), row(d), row(LANES), row(LANES)],
        out_shape=[
            jax.ShapeDtypeStruct((t, d), F32),
            jax.ShapeDtypeStruct((t, d), F32),
            jax.ShapeDtypeStruct((t, LANES), jnp.int32),
            jax.ShapeDtypeStruct((t, LANES), F32),
        ],
        compiler_params=_cparams(("parallel",)),
        name="outproj",
    )(m2, x2, w, g, wrh, wrl, br)


def _row_copy(src_hbm, dst_hbm, sem, src_row, dst_row):
    return pltpu.make_async_copy(src_hbm.at[pl.ds(src_row, 1)], dst_hbm.at[pl.ds(dst_row, 1)], sem)


def _dispatch_kernel(tok_ref, h_hbm, xs_hbm, sem, *, g):
    base = pl.program_id(0) * g

    def issue(r, c):
        _row_copy(h_hbm, xs_hbm, sem, tok_ref[0, 0, r], base + r).start()
        return c

    lax.fori_loop(0, g, issue, 0)
    pltpu.make_async_copy(xs_hbm.at[pl.ds(base, g)], xs_hbm.at[pl.ds(base, g)], sem).wait()


def _dispatch(slot_tok, h2, n_slots, *, g=1024):
    d = h2.shape[1]
    tok3 = slot_tok.reshape(n_slots // g, 1, g)
    return pl.pallas_call(
        functools.partial(_dispatch_kernel, g=g),
        grid=(n_slots // g,),
        in_specs=[
            pl.BlockSpec((1, 1, g), lambda i: (i, 0, 0), memory_space=pltpu.SMEM),
            pl.BlockSpec(memory_space=pl.ANY),
        ],
        out_specs=pl.BlockSpec(memory_space=pl.ANY),
        scratch_shapes=[pltpu.SemaphoreType.DMA(())],
        out_shape=jax.ShapeDtypeStruct((n_slots, d), h2.dtype),
        compiler_params=_cparams(("arbitrary",)),
        name="dispatch",
    )(tok3, h2)


def _expert_kernel(be_ref, nu_ref, x_ref, wg_ref, bg_ref, wu_ref, bu_ref, wd_ref, bd_ref, o_ref, xb_ref, acc_ref):
    b = pl.program_id(0)
    f = pl.program_id(1)

    @pl.when(b < nu_ref[0])
    def _():
        @pl.when(f == 0)
        def _():
            xb_ref[...] = x_ref[...].astype(BF16)
            acc_ref[...] = jnp.zeros(acc_ref.shape, F32)

        xb = xb_ref[...]
        gate = jnp.dot(xb, wg_ref[0], preferred_element_type=F32) + bg_ref[0]
        up = jnp.dot(xb, wu_ref[0], preferred_element_type=F32) + bu_ref[0]
        gate = jnp.minimum(gate, SWIGLU_LIMIT)
        up = jnp.clip(up, -SWIGLU_LIMIT, SWIGLU_LIMIT)
        glu = gate * jax.nn.sigmoid(SWIGLU_ALPHA * gate)
        act = ((up + 1.0) * glu).astype(BF16)
        acc_ref[...] += jnp.dot(act, wd_ref[0], preferred_element_type=F32)

        @pl.when(f == pl.num_programs(1) - 1)
        def _():
            o_ref[...] = acc_ref[...] + bd_ref[0]

    @pl.when(jnp.logical_and(b >= nu_ref[0], f == 0))
    def _():
        o_ref[...] = jnp.zeros(o_ref.shape, o_ref.dtype)


def _experts(block_expert, nused, xs, wg, bg, wu, bu, wd, bd, *, tm, tf=1024):
    n_slots, d = xs.shape
    nb = n_slots // tm
    dff = wg.shape[2]
    nf = dff // tf

    def blk(b, nu):
        return jnp.minimum(b, nu[0] - 1)

    def fidx(b, f, nu):
        return jnp.where(b < nu[0], f, nf - 1)

    return pl.pallas_call(
        _expert_kernel,
        grid_spec=pltpu.PrefetchScalarGridSpec(
            num_scalar_prefetch=2,
            grid=(nb, nf),
            in_specs=[
                pl.BlockSpec((tm, d), lambda b, f, be, nu: (blk(b, nu), 0)),
                pl.BlockSpec((1, d, tf), lambda b, f, be, nu: (be[blk(b, nu)], 0, fidx(b, f, nu))),
                pl.BlockSpec((1, 1, tf), lambda b, f, be, nu: (be[blk(b, nu)], 0, fidx(b, f, nu))),
                pl.BlockSpec((1, d, tf), lambda b, f, be, nu: (be[blk(b, nu)], 0, fidx(b, f, nu))),
                pl.BlockSpec((1, 1, tf), lambda b, f, be, nu: (be[blk(b, nu)], 0, fidx(b, f, nu))),
                pl.BlockSpec((1, tf, d), lambda b, f, be, nu: (be[blk(b, nu)], fidx(b, f, nu), 0)),
                pl.BlockSpec((1, 1, d), lambda b, f, be, nu: (be[blk(b, nu)], 0, 0)),
            ],
            out_specs=pl.BlockSpec((tm, d), lambda b, f, be, nu: (b, 0)),
            scratch_shapes=[pltpu.VMEM((tm, d), BF16), pltpu.VMEM((tm, d), F32)],
        ),
        out_shape=jax.ShapeDtypeStruct((n_slots, d), F32),
        compiler_params=_cparams(("arbitrary", "arbitrary")),
        name="experts",
    )(block_expert, nused, xs, wg, bg, wu, bu, wd, bd)


def _combine_kernel(pos_ref, x1_ref, wt_ref, ys_hbm, o_ref, buf_ref, sem, *, tm):
    def issue(r, c):
        for k in range(TOP_K):
            pltpu.make_async_copy(ys_hbm.at[pl.ds(pos_ref[0, 0, k * tm + r], 1)],
                                  buf_ref.at[k, pl.ds(r, 1)], sem).start()
        return c

    lax.fori_loop(0, tm, issue, 0)
    pltpu.make_async_copy(buf_ref, buf_ref, sem).wait()
    acc = x1_ref[...]
    for k in range(TOP_K):
        acc = acc + wt_ref[:, k:k + 1] * buf_ref[k]
    o_ref[...] = acc


def _combine(pos_km, x1, wt, ys, *, tm=128):
    t, d = x1.shape
    return pl.pallas_call(
        functools.partial(_combine_kernel, tm=tm),
        grid=(t // tm,),
        in_specs=[
            pl.BlockSpec((1, 1, TOP_K * tm), lambda i: (i, 0, 0), memory_space=pltpu.SMEM),
            pl.BlockSpec((tm, d), lambda i: (i, 0)),
            pl.BlockSpec((tm, LANES), lambda i: (i, 0)),
            pl.BlockSpec(memory_space=pl.ANY),
        ],
        out_specs=pl.BlockSpec((tm, d), lambda i: (i, 0)),
        out_shape=jax.ShapeDtypeStruct((t, d), F32),
        scratch_shapes=[pltpu.VMEM((TOP_K, tm, d), F32), pltpu.SemaphoreType.DMA(())],
        compiler_params=_cparams(("arbitrary",)),
        name="combine",
    )(pos_km, x1, wt, ys)


def _pad_heads(w, per_head):
    lead = w.shape[:-1]
    w = w.reshape(lead + (N_HEADS, per_head))
    w = jnp.pad(w, [(0, 0)] * len(lead) + [(0, 0), (0, HEAD_PAD - per_head)])
    return w.reshape(lead + (N_HEADS * HEAD_PAD,))


def _s5_discretise(lam_re, lam_im, log_dt, b_re, b_im):
    dt = jnp.exp(log_dt.astype(F32))[:, None]
    lr, li = lam_re.astype(F32), lam_im.astype(F32)
    mag = jnp.exp(lr * dt)
    ar, ai = mag * jnp.cos(li * dt), mag * jnp.sin(li * dt)
    den = lr * lr + li * li
    zr = ((ar - 1.0) * lr + ai * li) / den
    zi = (ai * lr - (ar - 1.0) * li) / den
    br, bi = b_re.astype(F32), b_im.astype(F32)
    bbr = zr[..., None] * br - zi[..., None] * bi
    bbi = zr[..., None] * bi + zi[..., None] * br
    return ar, ai, bbr, bbi


def _s5_pack(ar, ai, bbr, bbi, c_re, c_im, nb):
    eye = jnp.eye(S5_CLUSTER, dtype=F32)

    def pack_b(m):
        m4 = m.reshape(N_CLUSTERS, S5_CLUSTER, S5_STATE, S5_GROUP)
        return jnp.einsum('xgpc,gh->xgchp', m4, eye).reshape(N_CLUSTERS, S5_CLUSTER * S5_GROUP, CLUSTER_STATES)

    def pack_c(m):
        m4 = m.reshape(N_CLUSTERS, S5_CLUSTER, S5_GROUP, S5_STATE)
        return jnp.einsum('xgcp,gh->xgphc', m4, eye).reshape(N_CLUSTERS, CLUSTER_STATES, S5_CLUSTER * S5_GROUP)

    bb = jnp.concatenate([pack_b(bbr), pack_b(bbi)], axis=2).astype(BF16)
    cc = jnp.concatenate([pack_c(c_re.astype(F32)), -pack_c(c_im.astype(F32))], axis=1).astype(BF16)
    a = jnp.concatenate([ar.reshape(N_CLUSTERS, CLUSTER_STATES), ai.reshape(N_CLUSTERS, CLUSTER_STATES)], axis=1)
    a_bc = jnp.broadcast_to(a[:, None, :], (N_CLUSTERS, nb, 2 * CLUSTER_STATES))
    return bb, cc, a_bc


def _perm_matrix(nb):
    n = nb * SUB_T
    p = np.zeros((n, n), np.float32)
    for b in range(nb):
        for t in range(SUB_T):
            p[t * nb + b, b * SUB_T + t] = 1.0
    return p


def kernel(x, positions, norm1_g, w_in, b_gates, q_norm_g, w_uq, kv_norm_g, w_ukv, qk_norm_q_g, qk_norm_k_g, w_o_mla, s5_lambda_re, s5_lambda_im, s5_log_dt, s5_b_re, s5_b_im, s5_c_re, s5_c_im, s5_d, w_glu, w_o_s5, w_out, norm2_g, w_router, b_router, w_gate, b_gate, w_up, b_up, w_down, b_down):
    bsz, seq, d = x.shape
    t = bsz * seq
    depth = norm1_g.shape[0]
    o1 = Q_LORA_RANK
    o2 = o1 + KV_LORA_RANK
    o3 = o2 + QK_ROPE_DIM
    o4 = o3 + S5_WIDTH
    half = ROPE_HALF
    inv_freq = ROPE_THETA ** (-jnp.arange(half, dtype=F32) / half)
    lane = np.arange(LANES)
    freq = jnp.where(lane < QK_ROPE_DIM, jnp.tile(inv_freq, LANES // half), 0.0).reshape(1, LANES).astype(F32)
    sign = jnp.asarray(np.where(lane % QK_ROPE_DIM < half, -1.0, 1.0).reshape(1, LANES), F32)
    pos = positions.reshape(t, 1).astype(F32)
    sm_scale = 1.0 / math.sqrt(QK_HEAD_DIM)
    ind_q = jnp.asarray(np.equal.outer(np.arange(N_HEADS * HEAD_PAD) // HEAD_PAD, lane), BF16)
    ind_k = jnp.asarray(np.equal.outer(np.arange(N_HEADS * QK_NOPE_DIM) // QK_NOPE_DIM, lane), BF16)
    perm_np = _perm_matrix(bsz)
    perm = jnp.asarray(perm_np, BF16)
    permt = jnp.asarray(perm_np.T, BF16)
    tm_e = 512
    n_assign = t * TOP_K
    nb_e = n_assign // tm_e + N_EXPERTS
    n_slots = nb_e * tm_e
    tm_c = 128

    for l in range(depth):
        wi = w_in[l]
        w_in_p = jnp.concatenate(
            [wi[:, o4:], wi[:, o3:o4], wi[:, :o1], wi[:, o1:o2], wi[:, o2:o3],
             jnp.zeros((d, PROJ_W - wi.shape[1]), wi.dtype)], axis=1).astype(BF16)
        w_uq_p = _pad_heads(w_uq[l], QK_HEAD_DIM).astype(BF16)
        gq_full = _pad_heads(jnp.tile(qk_norm_q_g[l].astype(F32), N_HEADS) * sm_scale, QK_HEAD_DIM).reshape(1, -1)
        wkv = w_ukv[l].reshape(KV_LORA_RANK, N_HEADS, QK_NOPE_DIM + V_HEAD_DIM)
        w_kv_p = jnp.concatenate(
            [wkv[:, :, :QK_NOPE_DIM].reshape(KV_LORA_RANK, -1), wkv[:, :, QK_NOPE_DIM:].reshape(KV_LORA_RANK, -1)],
            axis=1).astype(BF16)
        gk = qk_norm_k_g[l].astype(F32)
        gk_nope = gk[:QK_NOPE_DIM].reshape(1, LANES)
        gk_rope = jnp.pad(gk[QK_NOPE_DIM:], (0, LANES - QK_ROPE_DIM)).reshape(1, LANES)
        ar, ai, bbr, bbi = _s5_discretise(s5_lambda_re[l], s5_lambda_im[l], s5_log_dt[l], s5_b_re[l], s5_b_im[l])
        bb, cc, a_bc = _s5_pack(ar, ai, bbr, bbi, s5_c_re[l], s5_c_im[l], bsz)
        wr = jnp.pad(w_router[l].astype(F32), ((0, 0), (0, LANES - N_EXPERTS)))
        wr_hi = wr.astype(BF16)
        wr_lo = (wr - wr_hi.astype(F32)).astype(BF16)
        br = jnp.pad(b_router[l].astype(F32), (0, LANES - N_EXPERTS)).reshape(1, LANES)

        x2 = x.reshape(t, d)
        proj = _inproj(x2, norm1_g[l].reshape(1, d), w_in_p)
        q = _qprep(proj, pos, q_norm_g[l].reshape(1, -1), w_uq_p, gq_full, ind_q, freq, sign)
        k, v = _kprep(proj, pos, kv_norm_g[l].reshape(1, -1), w_kv_p, gk_nope, gk_rope, ind_k, freq, sign)
        o = _attention(q.reshape(bsz, seq, -1), k.reshape(bsz, seq, -1), v.reshape(bsz, seq, -1))
        yb = _s5(proj.reshape(bsz, seq, PROJ_W), perm, permt, bb, cc, a_bc, s5_d[l].reshape(1, -1).astype(F32),
                 w_glu[l].astype(BF16), w_o_s5[l].astype(BF16))
        m = _merge(o.reshape(t, -1), proj, b_gates[l].reshape(1, -1).astype(F32), yb.reshape(t, d),
                   w_o_mla[l].astype(BF16))
        x1, h2, top_idx, top_w = _outproj(m, x2, w_out[l].astype(BF16), norm2_g[l].reshape(1, d), wr_hi, wr_lo, br)

        flat_e = top_idx[:, :TOP_K].reshape(-1)
        onehot = (flat_e[:, None] == jnp.arange(N_EXPERTS, dtype=jnp.int32)[None, :]).astype(jnp.int32)
        csum = jnp.cumsum(onehot, axis=0)
        rank = jnp.take_along_axis(csum, flat_e[:, None], axis=1)[:, 0] - 1
        counts = csum[-1]
        nblk = (counts + tm_e - 1) // tm_e
        blk_end = jnp.cumsum(nblk)
        blk_start = blk_end - nblk
        slot = blk_start[flat_e] * tm_e + rank
        nused = blk_end[-1:].astype(jnp.int32)
        block_expert = jnp.minimum(
            jnp.searchsorted(blk_end, jnp.arange(nb_e, dtype=jnp.int32), side='right'), N_EXPERTS - 1).astype(jnp.int32)
        slot_tok = jnp.zeros((n_slots,), jnp.int32).at[slot].set(jnp.arange(n_assign, dtype=jnp.int32) // TOP_K)
        pos_km = slot.reshape(t // tm_c, tm_c, TOP_K).transpose(0, 2, 1).reshape(t // tm_c, 1, TOP_K * tm_c)

        xs = _dispatch(slot_tok, h2, n_slots)
        ys = _experts(block_expert, nused, xs,
                      w_gate[l].astype(BF16), b_gate[l].reshape(N_EXPERTS, 1, -1).astype(F32),
                      w_up[l].astype(BF16), b_up[l].reshape(N_EXPERTS, 1, -1).astype(F32),
                      w_down[l].astype(BF16), b_down[l].reshape(N_EXPERTS, 1, -1).astype(F32), tm=tm_e)
        x = _combine(pos_km, x1, top_w, ys, tm=tm_c).reshape(bsz, seq, d)
    return x
```

```python
import functools
import math

import numpy as np
import jax
import jax.numpy as jnp
from jax import lax
from jax.experimental import pallas as pl
from jax.experimental.pallas import tpu as pltpu

F32 = jnp.float32
BF16 = jnp.bfloat16

D_MODEL = 2048
CHUNK = 64
EPS = 1e-6
N_HEADS = 16
QK_NOPE_DIM = 128
QK_ROPE_DIM = 64
QK_HEAD_DIM = QK_NOPE_DIM + QK_ROPE_DIM
V_HEAD_DIM = 128
Q_LORA_RANK = 512
KV_LORA_RANK = 256
ROPE_THETA = 10000.0
S5_WIDTH = 1024
S5_GROUP = 16
S5_GROUPS = S5_WIDTH // S5_GROUP
S5_STATE = 64
N_EXPERTS = 32
TOP_K = 4
D_FF = 2048
SWIGLU_LIMIT = 7.0
SWIGLU_ALPHA = 1.702

LANES = 128
HEAD_PAD = 256
ROPE_HALF = QK_ROPE_DIM // 2
S5_CLUSTER = 8
N_CLUSTERS = S5_GROUPS // S5_CLUSTER
CLUSTER_STATES = S5_CLUSTER * S5_STATE
SUB_T = 32
PROJ_W = 6144
VMEM_LIMIT = 56 * 1024 * 1024
NEG = -1e30


def _cparams(sem, vmem=VMEM_LIMIT, **kw):
    return pltpu.CompilerParams(dimension_semantics=sem, vmem_limit_bytes=vmem, **kw)


def _inproj_kernel(x_ref, g_ref, w_ref, o_ref, h_ref):
    @pl.when(pl.program_id(1) == 0)
    def _():
        x = x_ref[...]
        ms = jnp.mean(x * x, axis=-1, keepdims=True)
        h_ref[...] = (x * lax.rsqrt(ms + EPS) * g_ref[...]).astype(BF16)

    o_ref[...] = jnp.dot(h_ref[...], w_ref[...], preferred_element_type=F32).astype(o_ref.dtype)


def _inproj(x2, g, w, *, tm=1024, tn=1536):
    t, d = x2.shape
    n = w.shape[1]
    return pl.pallas_call(
        _inproj_kernel,
        grid=(t // tm, n // tn),
        in_specs=[
            pl.BlockSpec((tm, d), lambda i, j: (i, 0)),
            pl.BlockSpec((1, d), lambda i, j: (0, 0)),
            pl.BlockSpec((d, tn), lambda i, j: (0, j)),
        ],
        out_specs=pl.BlockSpec((tm, tn), lambda i, j: (i, j)),
        out_shape=jax.ShapeDtypeStruct((t, n), BF16),
        scratch_shapes=[pltpu.VMEM((tm, d), BF16)],
        compiler_params=_cparams(("parallel", "arbitrary")),
        name="inproj",
    )(x2, g, w)


def _rope_mid(mid, cos, sin_signed):
    lane = lax.broadcasted_iota(jnp.int32, mid.shape, 1)
    rot = jnp.where(lane < ROPE_HALF, pltpu.roll(mid, LANES - ROPE_HALF, 1), pltpu.roll(mid, ROPE_HALF, 1))
    return mid * cos + rot * sin_signed


def _qprep_kernel(ql_ref, pos_ref, gn_ref, w_ref, gfull_ref, ind_ref, freq_ref, sign_ref, o_ref):
    ql = ql_ref[...].astype(F32)
    ms = jnp.mean(ql * ql, axis=-1, keepdims=True)
    qn = (ql * lax.rsqrt(ms + EPS) * gn_ref[...]).astype(BF16)
    q = jnp.dot(qn, w_ref[...], preferred_element_type=F32)
    ssq = jnp.dot((q * q).astype(BF16), ind_ref[...], preferred_element_type=F32)
    sc = lax.rsqrt(ssq * (1.0 / QK_HEAD_DIM) + EPS)
    ang = pos_ref[...] * freq_ref[...]
    cos = jnp.cos(ang)
    sin_signed = jnp.sin(ang) * sign_ref[...]
    for h in range(N_HEADS):
        s_h = sc[:, h:h + 1]
        lo = h * HEAD_PAD
        nope = q[:, lo:lo + LANES] * s_h * gfull_ref[:, lo:lo + LANES]
        mid = q[:, lo + LANES:lo + HEAD_PAD] * s_h * gfull_ref[:, lo + LANES:lo + HEAD_PAD]
        o_ref[:, lo:lo + LANES] = nope.astype(o_ref.dtype)
        o_ref[:, lo + LANES:lo + HEAD_PAD] = _rope_mid(mid, cos, sin_signed).astype(o_ref.dtype)


def _qprep(proj, pos, gn, w, gfull, ind, freq, sign, *, tm=512):
    t = proj.shape[0]
    n = N_HEADS * HEAD_PAD
    full = lambda shape: pl.BlockSpec(shape, lambda i: (0, 0))
    return pl.pallas_call(
        _qprep_kernel,
        grid=(t // tm,),
        in_specs=[
            pl.BlockSpec((tm, Q_LORA_RANK), lambda i: (i, 10)),
            pl.BlockSpec((tm, 1), lambda i: (i, 0)),
            full((1, Q_LORA_RANK)),
            full((Q_LORA_RANK, n)),
            full((1, n)),
            full((n, LANES)),
            full((1, LANES)),
            full((1, LANES)),
        ],
        out_specs=pl.BlockSpec((tm, n), lambda i: (i, 0)),
        out_shape=jax.ShapeDtypeStruct((t, n), BF16),
        compiler_params=_cparams(("parallel",)),
        name="qprep",
    )(proj, pos, gn, w, gfull, ind, freq, sign)


def _kprep_kernel(kvl_ref, kr_ref, pos_ref, gn_ref, w_ref, gnope_ref, grope_ref, ind_ref, freq_ref, sign_ref,
                  k_ref, v_ref):
    kvl = kvl_ref[...].astype(F32)
    ms = jnp.mean(kvl * kvl, axis=-1, keepdims=True)
    kn = (kvl * lax.rsqrt(ms + EPS) * gn_ref[...]).astype(BF16)
    kv = jnp.dot(kn, w_ref[...], preferred_element_type=F32)
    nd = N_HEADS * QK_NOPE_DIM
    knope = kv[:, :nd]
    v_ref[...] = kv[:, nd:].astype(v_ref.dtype)
    kr = kr_ref[...].astype(F32)
    ssq = jnp.dot((knope * knope).astype(BF16), ind_ref[...], preferred_element_type=F32)
    ssq = ssq + jnp.sum(kr * kr, axis=-1, keepdims=True)
    sc = lax.rsqrt(ssq * (1.0 / QK_HEAD_DIM) + EPS)
    ang = pos_ref[...] * freq_ref[...]
    kr_rot = _rope_mid(kr * grope_ref[...], jnp.cos(ang), jnp.sin(ang) * sign_ref[...])
    for h in range(N_HEADS):
        s_h = sc[:, h:h + 1]
        lo = h * HEAD_PAD
        nope = knope[:, h * LANES:(h + 1) * LANES] * s_h * gnope_ref[...]
        k_ref[:, lo:lo + LANES] = nope.astype(k_ref.dtype)
        k_ref[:, lo + LANES:lo + HEAD_PAD] = (kr_rot * s_h).astype(k_ref.dtype)


def _kprep(proj, pos, gn, w, gnope, grope, ind, freq, sign, *, tm=512):
    t = proj.shape[0]
    n = N_HEADS * HEAD_PAD
    nd = N_HEADS * QK_NOPE_DIM
    full = lambda shape: pl.BlockSpec(shape, lambda i: (0, 0))
    return pl.pallas_call(
        _kprep_kernel,
        grid=(t // tm,),
        in_specs=[
            pl.BlockSpec((tm, KV_LORA_RANK), lambda i: (i, 22)),
            pl.BlockSpec((tm, LANES), lambda i: (i, 46)),
            pl.BlockSpec((tm, 1), lambda i: (i, 0)),
            full((1, KV_LORA_RANK)),
            full((KV_LORA_RANK, 2 * nd)),
            full((1, LANES)),
            full((1, LANES)),
            full((nd, LANES)),
            full((1, LANES)),
            full((1, LANES)),
        ],
        out_specs=[pl.BlockSpec((tm, n), lambda i: (i, 0)), pl.BlockSpec((tm, nd), lambda i: (i, 0))],
        out_shape=[jax.ShapeDtypeStruct((t, n), BF16), jax.ShapeDtypeStruct((t, nd), BF16)],
        compiler_params=_cparams(("parallel",)),
        name="kprep",
    )(proj, proj, pos, gn, w, gnope, grope, ind, freq, sign)


def _attn_kernel(q_ref, k_ref, v_ref, o_ref, m_ref, l_ref, acc_ref, *, tq, hg):
    qi = pl.program_id(2)
    m_ref[...] = jnp.full(m_ref.shape, NEG, F32)
    l_ref[...] = jnp.zeros(l_ref.shape, F32)
    acc_ref[...] = jnp.zeros(acc_ref.shape, F32)

    def block(j, masked):
        r0 = pl.multiple_of(j * tq, tq)
        for h in range(hg):
            q = q_ref[:, h * HEAD_PAD:(h + 1) * HEAD_PAD]
            kb = k_ref[pl.ds(r0, tq), h * HEAD_PAD:(h + 1) * HEAD_PAD]
            vb = v_ref[pl.ds(r0, tq), h * V_HEAD_DIM:(h + 1) * V_HEAD_DIM]
            st = lax.dot_general(kb, q, (((1,), (1,)), ((), ())), preferred_element_type=F32)
            if masked:
                key = lax.broadcasted_iota(jnp.int32, st.shape, 0) // CHUNK
                qry = lax.broadcasted_iota(jnp.int32, st.shape, 1) // CHUNK
                st = jnp.where(key <= qry, st, NEG)
            m_old = m_ref[h]
            m_new = jnp.maximum(m_old, jnp.max(st, axis=0, keepdims=True))
            alpha = jnp.exp2(m_old - m_new)
            p = jnp.exp2(st - m_new)
            l_ref[h] = alpha * l_ref[h] + jnp.sum(p, axis=0, keepdims=True)
            pv = lax.dot_general(vb, p.astype(BF16), (((0,), (0,)), ((), ())), preferred_element_type=F32)
            acc_ref[h] = alpha * acc_ref[h] + pv
            m_ref[h] = m_new

    def body(j, c):
        block(j, False)
        return c

    lax.fori_loop(0, qi, body, 0)
    block(qi, True)
    for h in range(hg):
        o = acc_ref[h] / l_ref[h]
        o_ref[:, h * V_HEAD_DIM:(h + 1) * V_HEAD_DIM] = o.T.astype(o_ref.dtype)


def _attention(q3, k3, v3, *, tq=512, hg=2):
    b, s, _ = q3.shape
    tq = min(tq, s)
    return pl.pallas_call(
        functools.partial(_attn_kernel, tq=tq, hg=hg),
        grid=(b, N_HEADS // hg, s // tq),
        in_specs=[
            pl.BlockSpec((None, tq, hg * HEAD_PAD), lambda bi, h, i: (bi, i, h)),
            pl.BlockSpec((None, s, hg * HEAD_PAD), lambda bi, h, i: (bi, 0, h)),
            pl.BlockSpec((None, s, hg * V_HEAD_DIM), lambda bi, h, i: (bi, 0, h)),
        ],
        out_specs=pl.BlockSpec((None, tq, hg * V_HEAD_DIM), lambda bi, h, i: (bi, i, h)),
        out_shape=jax.ShapeDtypeStruct((b, s, N_HEADS * V_HEAD_DIM), BF16),
        scratch_shapes=[pltpu.VMEM((hg, 1, tq), F32), pltpu.VMEM((hg, 1, tq), F32),
                        pltpu.VMEM((hg, V_HEAD_DIM, tq), F32)],
        compiler_params=_cparams(("parallel", "parallel", "arbitrary")),
        name="attention",
    )(q3, k3, v3)


def _s5_kernel(u_ref, perm_ref, permt_ref, bb_ref, cc_ref, a_ref, d_ref, wglu_ref, wo_ref, o_ref,
               ut_ref, bu_ref, y_ref, carry_ref, *, tc):
    nb = u_ref.shape[0]
    rows_sub = nb * SUB_T
    nsub = tc // SUB_T

    @pl.when(pl.program_id(0) == 0)
    def _():
        carry_ref[...] = jnp.zeros(carry_ref.shape, F32)

    for j in range(nsub):
        ub = u_ref[:, j * SUB_T:(j + 1) * SUB_T, :].reshape(rows_sub, S5_WIDTH)
        ut_ref[j * rows_sub:(j + 1) * rows_sub, :] = jnp.dot(
            perm_ref[...], ub, preferred_element_type=F32).astype(BF16)

    for c in range(N_CLUSTERS):
        bu_ref[...] = jnp.dot(ut_ref[:, c * LANES:(c + 1) * LANES], bb_ref[c], preferred_element_type=F32)
        ar = a_ref[c, :, :CLUSTER_STATES]
        ai = a_ref[c, :, CLUSTER_STATES:]

        def step(t, carry):
            xr, xi = carry
            r0 = pl.multiple_of(t * nb, nb)
            br = bu_ref[pl.ds(r0, nb), :CLUSTER_STATES]
            bi = bu_ref[pl.ds(r0, nb), CLUSTER_STATES:]
            nxr = ar * xr - ai * xi + br
            nxi = ar * xi + ai * xr + bi
            bu_ref[pl.ds(r0, nb), :CLUSTER_STATES] = nxr
            bu_ref[pl.ds(r0, nb), CLUSTER_STATES:] = nxi
            return nxr, nxi

        xr, xi = lax.fori_loop(0, tc, step, (carry_ref[c, :, :CLUSTER_STATES], carry_ref[c, :, CLUSTER_STATES:]),
                               unroll=8)
        carry_ref[c, :, :CLUSTER_STATES] = xr
        carry_ref[c, :, CLUSTER_STATES:] = xi
        y_ref[:, c * LANES:(c + 1) * LANES] = jnp.dot(bu_ref[...].astype(BF16), cc_ref[c], preferred_element_type=F32)

    y = y_ref[...] + d_ref[...] * ut_ref[...].astype(F32)
    g = jax.nn.gelu(y)
    z = g * jax.nn.sigmoid(jnp.dot(g.astype(BF16), wglu_ref[...], preferred_element_type=F32))
    yb = jnp.dot(z.astype(BF16), wo_ref[...], preferred_element_type=F32).astype(BF16)
    for j in range(nsub):
        blk = jnp.dot(permt_ref[...], yb[j * rows_sub:(j + 1) * rows_sub, :], preferred_element_type=F32)
        o_ref[:, j * SUB_T:(j + 1) * SUB_T, :] = blk.astype(o_ref.dtype).reshape(nb, SUB_T, o_ref.shape[2])


def _s5(proj3, perm, permt, bb, cc, a_bc, d, wglu, wo, *, tc=64):
    b, s, _ = proj3.shape
    rows = b * tc
    dm = wo.shape[1]
    c2 = lambda shape: pl.BlockSpec(shape, lambda i: (0, 0))
    c3 = lambda shape: pl.BlockSpec(shape, lambda i: (0, 0, 0))
    return pl.pallas_call(
        functools.partial(_s5_kernel, tc=tc),
        grid=(s // tc,),
        in_specs=[
            pl.BlockSpec((b, tc, S5_WIDTH), lambda i: (0, i, 4)),
            c2(perm.shape), c2(permt.shape), c3(bb.shape), c3(cc.shape), c3(a_bc.shape), c2(d.shape),
            c2(wglu.shape), c2(wo.shape),
        ],
        out_specs=pl.BlockSpec((b, tc, dm), lambda i: (0, i, 0)),
        out_shape=jax.ShapeDtypeStruct((b, s, dm), BF16),
        scratch_shapes=[
            pltpu.VMEM((rows, S5_WIDTH), BF16),
            pltpu.VMEM((rows, 2 * CLUSTER_STATES), F32),
            pltpu.VMEM((rows, S5_WIDTH), F32),
            pltpu.VMEM((N_CLUSTERS, b, 2 * CLUSTER_STATES), F32),
        ],
        compiler_params=_cparams(("arbitrary",)),
        name="s5",
    )(proj3, perm, permt, bb, cc, a_bc, d, wglu, wo)


def _merge_kernel(o_ref, gl_ref, bg_ref, yb_ref, w_ref, m_ref):
    d = o_ref.shape[1]
    ya = jnp.dot(o_ref[...], w_ref[...], preferred_element_type=F32)
    ga = jax.nn.sigmoid(gl_ref[:, :d].astype(F32) + bg_ref[:, :d])
    gb = jax.nn.sigmoid(gl_ref[:, d:].astype(F32) + bg_ref[:, d:])
    m_ref[...] = (ga * ya + gb * yb_ref[...].astype(F32)).astype(m_ref.dtype)


def _merge(o2, proj, bg, yb2, w, *, tm=512):
    t, d = o2.shape
    return pl.pallas_call(
        _merge_kernel,
        grid=(t // tm,),
        in_specs=[
            pl.BlockSpec((tm, d), lambda i: (i, 0)),
            pl.BlockSpec((tm, 2 * d), lambda i: (i, 0)),
            pl.BlockSpec((1, 2 * d), lambda i: (0, 0)),
            pl.BlockSpec((tm, d), lambda i: (i, 0)),
            pl.BlockSpec((d, d), lambda i: (0, 0)),
        ],
        out_specs=pl.BlockSpec((tm, d), lambda i: (i, 0)),
        out_shape=jax.ShapeDtypeStruct((t, d), BF16),
        compiler_params=_cparams(("parallel",)),
        name="merge",
    )(o2, proj, bg, yb2, w)


def _outproj_kernel(m_ref, x_ref, w_ref, g_ref, wrh_ref, wrl_ref, br_ref, x1_ref, h2_ref, idx_ref, wt_ref):
    x1 = x_ref[...] + jnp.dot(m_ref[...], w_ref[...], preferred_element_type=F32)
    x1_ref[...] = x1
    ms = jnp.mean(x1 * x1, axis=-1, keepdims=True)
    h2 = x1 * lax.rsqrt(ms + EPS) * g_ref[...]
    h_hi = h2.astype(BF16)
    h2_ref[...] = h_hi.astype(h2_ref.dtype)
    h_lo = (h2 - h_hi.astype(F32)).astype(BF16)
    logits = (jnp.dot(h_hi, wrh_ref[...], preferred_element_type=F32)
              + jnp.dot(h_lo, wrh_ref[...], preferred_element_type=F32)
              + jnp.dot(h_hi, wrl_ref[...], preferred_element_type=F32)) + br_ref[...]
    lane = lax.broadcasted_iota(jnp.int32, logits.shape, 1)
    work = jnp.where(lane < N_EXPERTS, logits, -jnp.inf)
    idx_out = jnp.zeros(logits.shape, jnp.int32)
    val_out = jnp.zeros(logits.shape, F32)
    v0 = None
    denom = None
    for k in range(TOP_K):
        mx = jnp.max(work, axis=-1, keepdims=True)
        sel = jnp.min(jnp.where(work == mx, lane, LANES), axis=-1, keepdims=True)
        if k == 0:
            v0 = mx
        e = jnp.exp(mx - v0)
        denom = e if k == 0 else denom + e
        idx_out = jnp.where(lane == k, sel, idx_out)
        val_out = jnp.where(lane == k, e, val_out)
        work = jnp.where(lane == sel, -jnp.inf, work)
    idx_ref[...] = idx_out
    wt_ref[...] = val_out / denom


def _outproj(m2, x2, w, g, wrh, wrl, br, *, tm=512):
    t, d = x2.shape
    c2 = lambda shape: pl.BlockSpec(shape, lambda i: (0, 0))
    row = lambda width: pl.BlockSpec((tm, width), lambda i: (i, 0))
    return pl.pallas_call(
        _outproj_kernel,
        grid=(t // tm,),
        in_specs=[row(d), row(d), c2((d, d)), c2((1, d)), c2((d, LANES)), c2((d, LANES)), c2((1, LANES))],
        out_specs=[row(d), row(d), row(LANES), row(LANES)],
        out_shape=[
            jax.ShapeDtypeStruct((t, d), F32),
            jax.ShapeDtypeStruct((t, d), F32),
            jax.ShapeDtypeStruct((t, LANES), jnp.int32),
            jax.ShapeDtypeStruct((t, LANES), F32),
        ],
        compiler_params=_cparams(("parallel",)),
        name="outproj",
    )(m2, x2, w, g, wrh, wrl, br)


def _dispatch_kernel(tok_ref, h_hbm, o_ref, sem, *, g):
    def issue(r, c):
        pltpu.make_async_copy(h_hbm.at[pl.ds(tok_ref[0, 0, r], 1)], o_ref.at[pl.ds(r, 1)], sem).start()
        return c

    lax.fori_loop(0, g, issue, 0, unroll=8)
    pltpu.make_async_copy(o_ref, o_ref, sem).wait()


def _dispatch(slot_tok, h2, n_slots, *, g=512):
    d = h2.shape[1]
    tok3 = slot_tok.reshape(n_slots // g, 1, g)
    return pl.pallas_call(
        functools.partial(_dispatch_kernel, g=g),
        grid=(n_slots // g,),
        in_specs=[
            pl.BlockSpec((1, 1, g), lambda i: (i, 0, 0), memory_space=pltpu.SMEM),
            pl.BlockSpec(memory_space=pl.ANY),
        ],
        out_specs=pl.BlockSpec((g, d), lambda i: (i, 0)),
        scratch_shapes=[pltpu.SemaphoreType.DMA(())],
        out_shape=jax.ShapeDtypeStruct((n_slots, d), h2.dtype),
        compiler_params=_cparams(("arbitrary",)),
        name="dispatch",
    )(tok3, h2)


def _expert_kernel(be_ref, nu_ref, x_ref, wg_ref, bg_ref, wu_ref, bu_ref, wd_ref, bd_ref, o_ref, xb_ref, acc_ref):
    b = pl.program_id(0)
    f = pl.program_id(1)

    @pl.when(b < nu_ref[0])
    def _():
        @pl.when(f == 0)
        def _():
            xb_ref[...] = x_ref[...].astype(BF16)
            acc_ref[...] = jnp.zeros(acc_ref.shape, F32)

        xb = xb_ref[...]
        gate = jnp.dot(xb, wg_ref[0], preferred_element_type=F32) + bg_ref[0]
        up = jnp.dot(xb, wu_ref[0], preferred_element_type=F32) + bu_ref[0]
        gate = jnp.minimum(gate, SWIGLU_LIMIT)
        up = jnp.clip(up, -SWIGLU_LIMIT, SWIGLU_LIMIT)
        glu = gate * jax.nn.sigmoid(SWIGLU_ALPHA * gate)
        act = ((up + 1.0) * glu).astype(BF16)
        acc_ref[...] += jnp.dot(act, wd_ref[0], preferred_element_type=F32)

        @pl.when(f == pl.num_programs(1) - 1)
        def _():
            o_ref[...] = acc_ref[...] + bd_ref[0]

    @pl.when(jnp.logical_and(b >= nu_ref[0], f == 0))
    def _():
        o_ref[...] = jnp.zeros(o_ref.shape, o_ref.dtype)


def _experts(block_expert, nused, xs, wg, bg, wu, bu, wd, bd, *, tm, tf=1024):
    n_slots, d = xs.shape
    nb = n_slots // tm
    dff = wg.shape[2]
    nf = dff // tf

    def blk(b, nu):
        return jnp.minimum(b, nu[0] - 1)

    def fidx(b, f, nu):
        return jnp.where(b < nu[0], f, nf - 1)

    return pl.pallas_call(
        _expert_kernel,
        grid_spec=pltpu.PrefetchScalarGridSpec(
            num_scalar_prefetch=2,
            grid=(nb, nf),
            in_specs=[
                pl.BlockSpec((tm, d), lambda b, f, be, nu: (blk(b, nu), 0)),
                pl.BlockSpec((1, d, tf), lambda b, f, be, nu: (be[blk(b, nu)], 0, fidx(b, f, nu))),
                pl.BlockSpec((1, 1, tf), lambda b, f, be, nu: (be[blk(b, nu)], 0, fidx(b, f, nu))),
                pl.BlockSpec((1, d, tf), lambda b, f, be, nu: (be[blk(b, nu)], 0, fidx(b, f, nu))),
                pl.BlockSpec((1, 1, tf), lambda b, f, be, nu: (be[blk(b, nu)], 0, fidx(b, f, nu))),
                pl.BlockSpec((1, tf, d), lambda b, f, be, nu: (be[blk(b, nu)], fidx(b, f, nu), 0)),
                pl.BlockSpec((1, 1, d), lambda b, f, be, nu: (be[blk(b, nu)], 0, 0)),
            ],
            out_specs=pl.BlockSpec((tm, d), lambda b, f, be, nu: (b, 0)),
            scratch_shapes=[pltpu.VMEM((tm, d), BF16), pltpu.VMEM((tm, d), F32)],
        ),
        out_shape=jax.ShapeDtypeStruct((n_slots, d), F32),
        compiler_params=_cparams(("arbitrary", "arbitrary")),
        name="experts",
    )(block_expert, nused, xs, wg, bg, wu, bu, wd, bd)


def _combine_kernel(pos_ref, x1_ref, wt_ref, ys_hbm, o_ref, buf_ref, sem, *, tm):
    def issue(r, c):
        for k in range(TOP_K):
            pltpu.make_async_copy(ys_hbm.at[pl.ds(pos_ref[0, 0, k * tm + r], 1)],
                                  buf_ref.at[k, pl.ds(r, 1)], sem).start()
        return c

    lax.fori_loop(0, tm, issue, 0)
    pltpu.make_async_copy(buf_ref, buf_ref, sem).wait()
    acc = x1_ref[...]
    for k in range(TOP_K):
        acc = acc + wt_ref[:, k:k + 1] * buf_ref[k]
    o_ref[...] = acc


def _combine(pos_km, x1, wt, ys, *, tm=128):
    t, d = x1.shape
    return pl.pallas_call(
        functools.partial(_combine_kernel, tm=tm),
        grid=(t // tm,),
        in_specs=[
            pl.BlockSpec((1, 1, TOP_K * tm), lambda i: (i, 0, 0), memory_space=pltpu.SMEM),
            pl.BlockSpec((tm, d), lambda i: (i, 0)),
            pl.BlockSpec((tm, LANES), lambda i: (i, 0)),
            pl.BlockSpec(memory_space=pl.ANY),
        ],
        out_specs=pl.BlockSpec((tm, d), lambda i: (i, 0)),
        out_shape=jax.ShapeDtypeStruct((t, d), F32),
        scratch_shapes=[pltpu.VMEM((TOP_K, tm, d), F32), pltpu.SemaphoreType.DMA(())],
        compiler_params=_cparams(("arbitrary",)),
        name="combine",
    )(pos_km, x1, wt, ys)


def _pad_heads(w, per_head):
    lead = w.shape[:-1]
    w = w.reshape(lead + (N_HEADS, per_head))
    w = jnp.pad(w, [(0, 0)] * len(lead) + [(0, 0), (0, HEAD_PAD - per_head)])
    return w.reshape(lead + (N_HEADS * HEAD_PAD,))


def _s5_discretise(lam_re, lam_im, log_dt, b_re, b_im):
    dt = jnp.exp(log_dt.astype(F32))[:, None]
    lr, li = lam_re.astype(F32), lam_im.astype(F32)
    mag = jnp.exp(lr * dt)
    ar, ai = mag * jnp.cos(li * dt), mag * jnp.sin(li * dt)
    den = lr * lr + li * li
    zr = ((ar - 1.0) * lr + ai * li) / den
    zi = (ai * lr - (ar - 1.0) * li) / den
    br, bi = b_re.astype(F32), b_im.astype(F32)
    bbr = zr[..., None] * br - zi[..., None] * bi
    bbi = zr[..., None] * bi + zi[..., None] * br
    return ar, ai, bbr, bbi


def _s5_pack(ar, ai, bbr, bbi, c_re, c_im, nb):
    eye = jnp.eye(S5_CLUSTER, dtype=F32)

    def pack_b(m):
        m4 = m.reshape(N_CLUSTERS, S5_CLUSTER, S5_STATE, S5_GROUP)
        return jnp.einsum('xgpc,gh->xgchp', m4, eye).reshape(N_CLUSTERS, S5_CLUSTER * S5_GROUP, CLUSTER_STATES)

    def pack_c(m):
        m4 = m.reshape(N_CLUSTERS, S5_CLUSTER, S5_GROUP, S5_STATE)
        return jnp.einsum('xgcp,gh->xgphc', m4, eye).reshape(N_CLUSTERS, CLUSTER_STATES, S5_CLUSTER * S5_GROUP)

    bb = jnp.concatenate([pack_b(bbr), pack_b(bbi)], axis=2).astype(BF16)
    cc = jnp.concatenate([pack_c(c_re.astype(F32)), -pack_c(c_im.astype(F32))], axis=1).astype(BF16)
    a = jnp.concatenate([ar.reshape(N_CLUSTERS, CLUSTER_STATES), ai.reshape(N_CLUSTERS, CLUSTER_STATES)], axis=1)
    a_bc = jnp.broadcast_to(a[:, None, :], (N_CLUSTERS, nb, 2 * CLUSTER_STATES))
    return bb, cc, a_bc


def _perm_matrix(nb):
    n = nb * SUB_T
    p = np.zeros((n, n), np.float32)
    for b in range(nb):
        for t in range(SUB_T):
            p[t * nb + b, b * SUB_T + t] = 1.0
    return p


def kernel(x, positions, norm1_g, w_in, b_gates, q_norm_g, w_uq, kv_norm_g, w_ukv, qk_norm_q_g, qk_norm_k_g, w_o_mla, s5_lambda_re, s5_lambda_im, s5_log_dt, s5_b_re, s5_b_im, s5_c_re, s5_c_im, s5_d, w_glu, w_o_s5, w_out, norm2_g, w_router, b_router, w_gate, b_gate, w_up, b_up, w_down, b_down):
    bsz, seq, d = x.shape
    t = bsz * seq
    depth = norm1_g.shape[0]
    o1 = Q_LORA_RANK
    o2 = o1 + KV_LORA_RANK
    o3 = o2 + QK_ROPE_DIM
    o4 = o3 + S5_WIDTH
    half = ROPE_HALF
    inv_freq = ROPE_THETA ** (-jnp.arange(half, dtype=F32) / half)
    lane = np.arange(LANES)
    freq = jnp.where(lane < QK_ROPE_DIM, jnp.tile(inv_freq, LANES // half), 0.0).reshape(1, LANES).astype(F32)
    sign = jnp.asarray(np.where(lane % QK_ROPE_DIM < half, -1.0, 1.0).reshape(1, LANES), F32)
    pos = positions.reshape(t, 1).astype(F32)
    sm_scale = math.log2(math.e) / math.sqrt(QK_HEAD_DIM)
    ind_q = jnp.asarray(np.equal.outer(np.arange(N_HEADS * HEAD_PAD) // HEAD_PAD, lane), BF16)
    ind_k = jnp.asarray(np.equal.outer(np.arange(N_HEADS * QK_NOPE_DIM) // QK_NOPE_DIM, lane), BF16)
    perm_np = _perm_matrix(bsz)
    perm = jnp.asarray(perm_np, BF16)
    permt = jnp.asarray(perm_np.T, BF16)
    tm_e = 512
    n_assign = t * TOP_K
    nb_e = n_assign // tm_e + N_EXPERTS
    n_slots = nb_e * tm_e
    tm_c = 128

    for l in range(depth):
        wi = w_in[l]
        w_in_p = jnp.concatenate(
            [wi[:, o4:], wi[:, o3:o4], wi[:, :o1], wi[:, o1:o2], wi[:, o2:o3],
             jnp.zeros((d, PROJ_W - wi.shape[1]), wi.dtype)], axis=1).astype(BF16)
        w_uq_p = _pad_heads(w_uq[l], QK_HEAD_DIM).astype(BF16)
        gq_full = _pad_heads(jnp.tile(qk_norm_q_g[l].astype(F32), N_HEADS) * sm_scale, QK_HEAD_DIM).reshape(1, -1)
        wkv = w_ukv[l].reshape(KV_LORA_RANK, N_HEADS, QK_NOPE_DIM + V_HEAD_DIM)
        w_kv_p = jnp.concatenate(
            [wkv[:, :, :QK_NOPE_DIM].reshape(KV_LORA_RANK, -1), wkv[:, :, QK_NOPE_DIM:].reshape(KV_LORA_RANK, -1)],
            axis=1).astype(BF16)
        gk = qk_norm_k_g[l].astype(F32)
        gk_nope = gk[:QK_NOPE_DIM].reshape(1, LANES)
        gk_rope = jnp.pad(gk[QK_NOPE_DIM:], (0, LANES - QK_ROPE_DIM)).reshape(1, LANES)
        ar, ai, bbr, bbi = _s5_discretise(s5_lambda_re[l], s5_lambda_im[l], s5_log_dt[l], s5_b_re[l], s5_b_im[l])
        bb, cc, a_bc = _s5_pack(ar, ai, bbr, bbi, s5_c_re[l], s5_c_im[l], bsz)
        wr = jnp.pad(w_router[l].astype(F32), ((0, 0), (0, LANES - N_EXPERTS)))
        wr_hi = wr.astype(BF16)
        wr_lo = (wr - wr_hi.astype(F32)).astype(BF16)
        br = jnp.pad(b_router[l].astype(F32), (0, LANES - N_EXPERTS)).reshape(1, LANES)

        x2 = x.reshape(t, d)
        proj = _inproj(x2, norm1_g[l].reshape(1, d), w_in_p)
        q = _qprep(proj, pos, q_norm_g[l].reshape(1, -1), w_uq_p, gq_full, ind_q, freq, sign)
        k, v = _kprep(proj, pos, kv_norm_g[l].reshape(1, -1), w_kv_p, gk_nope, gk_rope, ind_k, freq, sign)
        o = _attention(q.reshape(bsz, seq, -1), k.reshape(bsz, seq, -1), v.reshape(bsz, seq, -1))
        yb = _s5(proj.reshape(bsz, seq, PROJ_W), perm, permt, bb, cc, a_bc, s5_d[l].reshape(1, -1).astype(F32),
                 w_glu[l].astype(BF16), w_o_s5[l].astype(BF16))
        m = _merge(o.reshape(t, -1), proj, b_gates[l].reshape(1, -1).astype(F32), yb.reshape(t, d),
                   w_o_mla[l].astype(BF16))
        x1, h2, top_idx, top_w = _outproj(m, x2, w_out[l].astype(BF16), norm2_g[l].reshape(1, d), wr_hi, wr_lo, br)

        flat_e = top_idx[:, :TOP_K].reshape(-1)
        onehot = (flat_e[:, None] == jnp.arange(N_EXPERTS, dtype=jnp.int32)[None, :]).astype(jnp.int32)
        csum = jnp.cumsum(onehot, axis=0)
        rank = jnp.take_along_axis(csum, flat_e[:, None], axis=1)[:, 0] - 1
        counts = csum[-1]
        nblk = (counts + tm_e - 1) // tm_e
        blk_end = jnp.cumsum(nblk)
        blk_start = blk_end - nblk
        slot = blk_start[flat_e] * tm_e + rank
        nused = blk_end[-1:].astype(jnp.int32)
        block_expert = jnp.minimum(
            jnp.searchsorted(blk_end, jnp.arange(nb_e, dtype=jnp.int32), side='right'), N_EXPERTS - 1).astype(jnp.int32)
        slot_tok = jnp.zeros((n_slots,), jnp.int32).at[slot].set(jnp.arange(n_assign, dtype=jnp.int32) // TOP_K)
        pos_km = slot.reshape(t // tm_c, tm_c, TOP_K).transpose(0, 2, 1).reshape(t // tm_c, 1, TOP_K * tm_c)

        xs = _dispatch(slot_tok, h2, n_slots)
        ys = _experts(block_expert, nused, xs,
                      w_gate[l].astype(BF16), b_gate[l].reshape(N_EXPERTS, 1, -1).astype(F32),
                      w_up[l].astype(BF16), b_up[l].reshape(N_EXPERTS, 1, -1).astype(F32),
                      w_down[l].astype(BF16), b_down[l].reshape(N_EXPERTS, 1, -1).astype(F32), tm=tm_e)
        x = _combine(pos_km, x1, top_w, ys, tm=tm_c).reshape(bsz, seq, d)
    return x
```

```python
import functools
import math

import numpy as np
import jax
import jax.numpy as jnp
from jax import lax
from jax.experimental import pallas as pl
from jax.experimental.pallas import tpu as pltpu

F32 = jnp.float32
BF16 = jnp.bfloat16

D_MODEL = 2048
CHUNK = 64
EPS = 1e-6
N_HEADS = 16
QK_NOPE_DIM = 128
QK_ROPE_DIM = 64
QK_HEAD_DIM = QK_NOPE_DIM + QK_ROPE_DIM
V_HEAD_DIM = 128
Q_LORA_RANK = 512
KV_LORA_RANK = 256
ROPE_THETA = 10000.0
S5_WIDTH = 1024
S5_GROUP = 16
S5_GROUPS = S5_WIDTH // S5_GROUP
S5_STATE = 64
N_EXPERTS = 32
TOP_K = 4
D_FF = 2048
SWIGLU_LIMIT = 7.0
SWIGLU_ALPHA = 1.702

LANES = 128
HEAD_PAD = 256
ROPE_HALF = QK_ROPE_DIM // 2
S5_CLUSTER = 8
N_CLUSTERS = S5_GROUPS // S5_CLUSTER
CLUSTER_STATES = S5_CLUSTER * S5_STATE
SUB_T = 32
PROJ_W = 6144
VMEM_LIMIT = 56 * 1024 * 1024
NEG = -1e30


def _cparams(sem, vmem=VMEM_LIMIT, **kw):
    return pltpu.CompilerParams(dimension_semantics=sem, vmem_limit_bytes=vmem, **kw)


def _inproj_kernel(x_ref, g_ref, w_ref, o_ref, h_ref):
    @pl.when(pl.program_id(1) == 0)
    def _():
        x = x_ref[...]
        ms = jnp.mean(x * x, axis=-1, keepdims=True)
        h_ref[...] = (x * lax.rsqrt(ms + EPS) * g_ref[...]).astype(BF16)

    o_ref[...] = jnp.dot(h_ref[...], w_ref[...], preferred_element_type=F32).astype(o_ref.dtype)


def _inproj(x2, g, w, *, tm=1024, tn=1536):
    t, d = x2.shape
    n = w.shape[1]
    return pl.pallas_call(
        _inproj_kernel,
        grid=(t // tm, n // tn),
        in_specs=[
            pl.BlockSpec((tm, d), lambda i, j: (i, 0)),
            pl.BlockSpec((1, d), lambda i, j: (0, 0)),
            pl.BlockSpec((d, tn), lambda i, j: (0, j)),
        ],
        out_specs=pl.BlockSpec((tm, tn), lambda i, j: (i, j)),
        out_shape=jax.ShapeDtypeStruct((t, n), BF16),
        scratch_shapes=[pltpu.VMEM((tm, d), BF16)],
        compiler_params=_cparams(("parallel", "arbitrary")),
        name="inproj",
    )(x2, g, w)


def _rope_mid(mid, cos, sin_signed):
    lane = lax.broadcasted_iota(jnp.int32, mid.shape, 1)
    rot = jnp.where(lane < ROPE_HALF, pltpu.roll(mid, LANES - ROPE_HALF, 1), pltpu.roll(mid, ROPE_HALF, 1))
    return mid * cos + rot * sin_signed


def _qprep_kernel(ql_ref, pos_ref, gn_ref, w_ref, gfull_ref, ind_ref, freq_ref, sign_ref, o_ref):
    ql = ql_ref[...].astype(F32)
    ms = jnp.mean(ql * ql, axis=-1, keepdims=True)
    qn = (ql * lax.rsqrt(ms + EPS) * gn_ref[...]).astype(BF16)
    q = jnp.dot(qn, w_ref[...], preferred_element_type=F32)
    ssq = jnp.dot((q * q).astype(BF16), ind_ref[...], preferred_element_type=F32)
    sc = lax.rsqrt(ssq * (1.0 / QK_HEAD_DIM) + EPS)
    ang = pos_ref[...] * freq_ref[...]
    cos = jnp.cos(ang)
    sin_signed = jnp.sin(ang) * sign_ref[...]
    for h in range(N_HEADS):
        s_h = sc[:, h:h + 1]
        lo = h * HEAD_PAD
        nope = q[:, lo:lo + LANES] * s_h * gfull_ref[:, lo:lo + LANES]
        mid = q[:, lo + LANES:lo + HEAD_PAD] * s_h * gfull_ref[:, lo + LANES:lo + HEAD_PAD]
        o_ref[:, lo:lo + LANES] = nope.astype(o_ref.dtype)
        o_ref[:, lo + LANES:lo + HEAD_PAD] = _rope_mid(mid, cos, sin_signed).astype(o_ref.dtype)


def _qprep(proj, pos, gn, w, gfull, ind, freq, sign, *, tm=512):
    t = proj.shape[0]
    n = N_HEADS * HEAD_PAD
    full = lambda shape: pl.BlockSpec(shape, lambda i: (0, 0))
    return pl.pallas_call(
        _qprep_kernel,
        grid=(t // tm,),
        in_specs=[
            pl.BlockSpec((tm, Q_LORA_RANK), lambda i: (i, 10)),
            pl.BlockSpec((tm, 1), lambda i: (i, 0)),
            full((1, Q_LORA_RANK)),
            full((Q_LORA_RANK, n)),
            full((1, n)),
            full((n, LANES)),
            full((1, LANES)),
            full((1, LANES)),
        ],
        out_specs=pl.BlockSpec((tm, n), lambda i: (i, 0)),
        out_shape=jax.ShapeDtypeStruct((t, n), BF16),
        compiler_params=_cparams(("parallel",)),
        name="qprep",
    )(proj, pos, gn, w, gfull, ind, freq, sign)


def _kprep_kernel(kvl_ref, kr_ref, pos_ref, gn_ref, w_ref, gnope_ref, grope_ref, ind_ref, freq_ref, sign_ref,
                  k_ref, v_ref):
    kvl = kvl_ref[...].astype(F32)
    ms = jnp.mean(kvl * kvl, axis=-1, keepdims=True)
    kn = (kvl * lax.rsqrt(ms + EPS) * gn_ref[...]).astype(BF16)
    kv = jnp.dot(kn, w_ref[...], preferred_element_type=F32)
    nd = N_HEADS * QK_NOPE_DIM
    knope = kv[:, :nd]
    v_ref[...] = kv[:, nd:].astype(v_ref.dtype)
    kr = kr_ref[...].astype(F32)
    ssq = jnp.dot((knope * knope).astype(BF16), ind_ref[...], preferred_element_type=F32)
    ssq = ssq + jnp.sum(kr * kr, axis=-1, keepdims=True)
    sc = lax.rsqrt(ssq * (1.0 / QK_HEAD_DIM) + EPS)
    ang = pos_ref[...] * freq_ref[...]
    kr_rot = _rope_mid(kr * grope_ref[...], jnp.cos(ang), jnp.sin(ang) * sign_ref[...])
    for h in range(N_HEADS):
        s_h = sc[:, h:h + 1]
        lo = h * HEAD_PAD
        nope = knope[:, h * LANES:(h + 1) * LANES] * s_h * gnope_ref[...]
        k_ref[:, lo:lo + LANES] = nope.astype(k_ref.dtype)
        k_ref[:, lo + LANES:lo + HEAD_PAD] = (kr_rot * s_h).astype(k_ref.dtype)


def _kprep(proj, pos, gn, w, gnope, grope, ind, freq, sign, *, tm=512):
    t = proj.shape[0]
    n = N_HEADS * HEAD_PAD
    nd = N_HEADS * QK_NOPE_DIM
    full = lambda shape: pl.BlockSpec(shape, lambda i: (0, 0))
    return pl.pallas_call(
        _kprep_kernel,
        grid=(t // tm,),
        in_specs=[
            pl.BlockSpec((tm, KV_LORA_RANK), lambda i: (i, 22)),
            pl.BlockSpec((tm, LANES), lambda i: (i, 46)),
            pl.BlockSpec((tm, 1), lambda i: (i, 0)),
            full((1, KV_LORA_RANK)),
            full((KV_LORA_RANK, 2 * nd)),
            full((1, LANES)),
            full((1, LANES)),
            full((nd, LANES)),
            full((1, LANES)),
            full((1, LANES)),
        ],
        out_specs=[pl.BlockSpec((tm, n), lambda i: (i, 0)), pl.BlockSpec((tm, nd), lambda i: (i, 0))],
        out_shape=[jax.ShapeDtypeStruct((t, n), BF16), jax.ShapeDtypeStruct((t, nd), BF16)],
        compiler_params=_cparams(("parallel",)),
        name="kprep",
    )(proj, proj, pos, gn, w, gnope, grope, ind, freq, sign)


def _attn_kernel(q_ref, k_ref, v_ref, o_ref, m_ref, l_ref, acc_ref, *, tq, hg):
    qi = pl.program_id(2)
    m_ref[...] = jnp.full(m_ref.shape, NEG, F32)
    l_ref[...] = jnp.zeros(l_ref.shape, F32)
    acc_ref[...] = jnp.zeros(acc_ref.shape, F32)

    def block(j, masked):
        r0 = pl.multiple_of(j * tq, tq)
        for h in range(hg):
            q = q_ref[:, h * HEAD_PAD:(h + 1) * HEAD_PAD]
            kb = k_ref[pl.ds(r0, tq), h * HEAD_PAD:(h + 1) * HEAD_PAD]
            vb = v_ref[pl.ds(r0, tq), h * V_HEAD_DIM:(h + 1) * V_HEAD_DIM]
            st = lax.dot_general(kb, q, (((1,), (1,)), ((), ())), preferred_element_type=F32)
            if masked:
                key = lax.broadcasted_iota(jnp.int32, st.shape, 0) // CHUNK
                qry = lax.broadcasted_iota(jnp.int32, st.shape, 1) // CHUNK
                st = jnp.where(key <= qry, st, NEG)
            m_old = m_ref[h]
            m_new = jnp.maximum(m_old, jnp.max(st, axis=0, keepdims=True))
            alpha = jnp.exp2(m_old - m_new)
            p = jnp.exp2(st - m_new)
            l_ref[h] = alpha * l_ref[h] + jnp.sum(p, axis=0, keepdims=True)
            pv = lax.dot_general(vb, p.astype(BF16), (((0,), (0,)), ((), ())), preferred_element_type=F32)
            acc_ref[h] = alpha * acc_ref[h] + pv
            m_ref[h] = m_new

    def body(j, c):
        block(j, False)
        return c

    lax.fori_loop(0, qi, body, 0)
    block(qi, True)
    for h in range(hg):
        o = acc_ref[h] / l_ref[h]
        o_ref[:, h * V_HEAD_DIM:(h + 1) * V_HEAD_DIM] = o.T.astype(o_ref.dtype)


def _attention(q3, k3, v3, *, tq=512, hg=2):
    b, s, _ = q3.shape
    tq = min(tq, s)
    return pl.pallas_call(
        functools.partial(_attn_kernel, tq=tq, hg=hg),
        grid=(b, N_HEADS // hg, s // tq),
        in_specs=[
            pl.BlockSpec((None, tq, hg * HEAD_PAD), lambda bi, h, i: (bi, i, h)),
            pl.BlockSpec((None, s, hg * HEAD_PAD), lambda bi, h, i: (bi, 0, h)),
            pl.BlockSpec((None, s, hg * V_HEAD_DIM), lambda bi, h, i: (bi, 0, h)),
        ],
        out_specs=pl.BlockSpec((None, tq, hg * V_HEAD_DIM), lambda bi, h, i: (bi, i, h)),
        out_shape=jax.ShapeDtypeStruct((b, s, N_HEADS * V_HEAD_DIM), BF16),
        scratch_shapes=[pltpu.VMEM((hg, 1, tq), F32), pltpu.VMEM((hg, 1, tq), F32),
                        pltpu.VMEM((hg, V_HEAD_DIM, tq), F32)],
        compiler_params=_cparams(("parallel", "parallel", "arbitrary")),
        name="attention",
    )(q3, k3, v3)


def _s5_kernel(u_ref, perm_ref, permt_ref, bb_ref, cc_ref, a_ref, d_ref, wglu_ref, wo_ref, o_ref,
               ut_ref, bu_ref, y_ref, carry_ref, *, tc):
    nb = u_ref.shape[0]
    rows_sub = nb * SUB_T
    nsub = tc // SUB_T

    @pl.when(pl.program_id(0) == 0)
    def _():
        carry_ref[...] = jnp.zeros(carry_ref.shape, F32)

    for j in range(nsub):
        ub = u_ref[:, j * SUB_T:(j + 1) * SUB_T, :].reshape(rows_sub, S5_WIDTH)
        ut_ref[j * rows_sub:(j + 1) * rows_sub, :] = jnp.dot(
            perm_ref[...], ub, preferred_element_type=F32).astype(BF16)

    for c in range(N_CLUSTERS):
        bu_ref[...] = jnp.dot(ut_ref[:, c * LANES:(c + 1) * LANES], bb_ref[c], preferred_element_type=F32)
        ar = a_ref[c, :, :CLUSTER_STATES]
        ai = a_ref[c, :, CLUSTER_STATES:]

        def step(t, carry):
            xr, xi = carry
            r0 = pl.multiple_of(t * nb, nb)
            br = bu_ref[pl.ds(r0, nb), :CLUSTER_STATES]
            bi = bu_ref[pl.ds(r0, nb), CLUSTER_STATES:]
            nxr = ar * xr - ai * xi + br
            nxi = ar * xi + ai * xr + bi
            bu_ref[pl.ds(r0, nb), :CLUSTER_STATES] = nxr
            bu_ref[pl.ds(r0, nb), CLUSTER_STATES:] = nxi
            return nxr, nxi

        xr, xi = lax.fori_loop(0, tc, step, (carry_ref[c, :, :CLUSTER_STATES], carry_ref[c, :, CLUSTER_STATES:]),
                               unroll=8)
        carry_ref[c, :, :CLUSTER_STATES] = xr
        carry_ref[c, :, CLUSTER_STATES:] = xi
        y_ref[:, c * LANES:(c + 1) * LANES] = jnp.dot(bu_ref[...].astype(BF16), cc_ref[c], preferred_element_type=F32)

    y = y_ref[...] + d_ref[...] * ut_ref[...].astype(F32)
    g = jax.nn.gelu(y)
    z = g * jax.nn.sigmoid(jnp.dot(g.astype(BF16), wglu_ref[...], preferred_element_type=F32))
    yb = jnp.dot(z.astype(BF16), wo_ref[...], preferred_element_type=F32).astype(BF16)
    for j in range(nsub):
        blk = jnp.dot(permt_ref[...], yb[j * rows_sub:(j + 1) * rows_sub, :], preferred_element_type=F32)
        o_ref[:, j * SUB_T:(j + 1) * SUB_T, :] = blk.astype(o_ref.dtype).reshape(nb, SUB_T, o_ref.shape[2])


def _s5(proj3, perm, permt, bb, cc, a_bc, d, wglu, wo, *, tc=64):
    b, s, _ = proj3.shape
    rows = b * tc
    dm = wo.shape[1]
    c2 = lambda shape: pl.BlockSpec(shape, lambda i: (0, 0))
    c3 = lambda shape: pl.BlockSpec(shape, lambda i: (0, 0, 0))
    return pl.pallas_call(
        functools.partial(_s5_kernel, tc=tc),
        grid=(s // tc,),
        in_specs=[
            pl.BlockSpec((b, tc, S5_WIDTH), lambda i: (0, i, 4)),
            c2(perm.shape), c2(permt.shape), c3(bb.shape), c3(cc.shape), c3(a_bc.shape), c2(d.shape),
            c2(wglu.shape), c2(wo.shape),
        ],
        out_specs=pl.BlockSpec((b, tc, dm), lambda i: (0, i, 0)),
        out_shape=jax.ShapeDtypeStruct((b, s, dm), BF16),
        scratch_shapes=[
            pltpu.VMEM((rows, S5_WIDTH), BF16),
            pltpu.VMEM((rows, 2 * CLUSTER_STATES), F32),
            pltpu.VMEM((rows, S5_WIDTH), F32),
            pltpu.VMEM((N_CLUSTERS, b, 2 * CLUSTER_STATES), F32),
        ],
        compiler_params=_cparams(("arbitrary",)),
        name="s5",
    )(proj3, perm, permt, bb, cc, a_bc, d, wglu, wo)


def _merge_kernel(o_ref, gl_ref, bg_ref, yb_ref, w_ref, m_ref):
    d = o_ref.shape[1]
    ya = jnp.dot(o_ref[...], w_ref[...], preferred_element_type=F32)
    ga = jax.nn.sigmoid(gl_ref[:, :d].astype(F32) + bg_ref[:, :d])
    gb = jax.nn.sigmoid(gl_ref[:, d:].astype(F32) + bg_ref[:, d:])
    m_ref[...] = (ga * ya + gb * yb_ref[...].astype(F32)).astype(m_ref.dtype)


def _merge(o2, proj, bg, yb2, w, *, tm=512):
    t, d = o2.shape
    return pl.pallas_call(
        _merge_kernel,
        grid=(t // tm,),
        in_specs=[
            pl.BlockSpec((tm, d), lambda i: (i, 0)),
            pl.BlockSpec((tm, 2 * d), lambda i: (i, 0)),
            pl.BlockSpec((1, 2 * d), lambda i: (0, 0)),
            pl.BlockSpec((tm, d), lambda i: (i, 0)),
            pl.BlockSpec((d, d), lambda i: (0, 0)),
        ],
        out_specs=pl.BlockSpec((tm, d), lambda i: (i, 0)),
        out_shape=jax.ShapeDtypeStruct((t, d), BF16),
        compiler_params=_cparams(("parallel",)),
        name="merge",
    )(o2, proj, bg, yb2, w)


def _to_row_tiles(x):
    return x.reshape(x.shape[0], x.shape[1] // LANES, LANES)


def _from_row_tiles(x):
    return x.reshape(x.shape[0], x.shape[1] * LANES)


def _outproj_kernel(m_ref, x_ref, w_ref, g_ref, wrh_ref, wrl_ref, br_ref, x1_ref, h2_ref, idx_ref, wt_ref):
    x1 = x_ref[...] + jnp.dot(m_ref[...], w_ref[...], preferred_element_type=F32)
    x1_ref[...] = x1
    ms = jnp.mean(x1 * x1, axis=-1, keepdims=True)
    h2 = x1 * lax.rsqrt(ms + EPS) * g_ref[...]
    h_hi = h2.astype(BF16)
    h2_ref[...] = _to_row_tiles(h_hi)
    h_lo = (h2 - h_hi.astype(F32)).astype(BF16)
    logits = (jnp.dot(h_hi, wrh_ref[...], preferred_element_type=F32)
              + jnp.dot(h_lo, wrh_ref[...], preferred_element_type=F32)
              + jnp.dot(h_hi, wrl_ref[...], preferred_element_type=F32)) + br_ref[...]
    lane = lax.broadcasted_iota(jnp.int32, logits.shape, 1)
    work = jnp.where(lane < N_EXPERTS, logits, -jnp.inf)
    idx_out = jnp.zeros(logits.shape, jnp.int32)
    val_out = jnp.zeros(logits.shape, F32)
    v0 = None
    denom = None
    for k in range(TOP_K):
        mx = jnp.max(work, axis=-1, keepdims=True)
        sel = jnp.min(jnp.where(work == mx, lane, LANES), axis=-1, keepdims=True)
        if k == 0:
            v0 = mx
        e = jnp.exp(mx - v0)
        denom = e if k == 0 else denom + e
        idx_out = jnp.where(lane == k, sel, idx_out)
        val_out = jnp.where(lane == k, e, val_out)
        work = jnp.where(lane == sel, -jnp.inf, work)
    idx_ref[...] = idx_out
    wt_ref[...] = val_out / denom


def _outproj(m2, x2, w, g, wrh, wrl, br, *, tm=512):
    t, d = x2.shape
    c2 = lambda shape: pl.BlockSpec(shape, lambda i: (0, 0))
    row = lambda width: pl.BlockSpec((tm, width), lambda i: (i, 0))
    return pl.pallas_call(
        _outproj_kernel,
        grid=(t // tm,),
        in_specs=[row(d), row(d), c2((d, d)), c2((1, d)), c2((d, LANES)), c2((d, LANES)), c2((1, LANES))],
        out_specs=[row(d), pl.BlockSpec((tm, d // LANES, LANES), lambda i: (i, 0, 0)), row(LANES), row(LANES)],
        out_shape=[
            jax.ShapeDtypeStruct((t, d), F32),
            jax.ShapeDtypeStruct((t, d // LANES, LANES), BF16),
            jax.ShapeDtypeStruct((t, LANES), jnp.int32),
            jax.ShapeDtypeStruct((t, LANES), F32),
        ],
        compiler_params=_cparams(("parallel",)),
        name="outproj",
    )(m2, x2, w, g, wrh, wrl, br)


def _dispatch_kernel(nu_ref, tok_ref, h_hbm, o_ref, sem, *, g):
    i = pl.program_id(0)

    @pl.when(i < nu_ref[0])
    def _():
        def issue(r, c):
            pltpu.make_async_copy(h_hbm.at[pl.ds(tok_ref[0, 0, r], 1)], o_ref.at[pl.ds(r, 1)], sem).start()
            return c

        lax.fori_loop(0, g, issue, 0, unroll=8)
        pltpu.make_async_copy(o_ref, o_ref, sem).wait()

    @pl.when(i >= nu_ref[0])
    def _():
        o_ref[...] = jnp.zeros(o_ref.shape, o_ref.dtype)


def _dispatch(nused, slot_tok, h2, n_slots, *, g):
    tile = h2.shape[1:]
    tok3 = slot_tok.reshape(n_slots // g, 1, g)
    return pl.pallas_call(
        functools.partial(_dispatch_kernel, g=g),
        grid_spec=pltpu.PrefetchScalarGridSpec(
            num_scalar_prefetch=1,
            grid=(n_slots // g,),
            in_specs=[
                pl.BlockSpec((1, 1, g), lambda i, nu: (i, 0, 0), memory_space=pltpu.SMEM),
                pl.BlockSpec(memory_space=pl.ANY),
            ],
            out_specs=pl.BlockSpec((g,) + tile, lambda i, nu: (i, 0, 0)),
            scratch_shapes=[pltpu.SemaphoreType.DMA(())],
        ),
        out_shape=jax.ShapeDtypeStruct((n_slots,) + tile, h2.dtype),
        compiler_params=_cparams(("arbitrary",)),
        name="dispatch",
    )(nused, tok3, h2)


def _expert_kernel(be_ref, nu_ref, x_ref, wg_ref, bg_ref, wu_ref, bu_ref, wd_ref, bd_ref, o_ref, xb_ref, acc_ref):
    b = pl.program_id(0)
    f = pl.program_id(1)

    @pl.when(b < nu_ref[0])
    def _():
        @pl.when(f == 0)
        def _():
            xb_ref[...] = _from_row_tiles(x_ref[...])
            acc_ref[...] = jnp.zeros(acc_ref.shape, F32)

        xb = xb_ref[...]
        gate = jnp.dot(xb, wg_ref[0], preferred_element_type=F32) + bg_ref[0]
        up = jnp.dot(xb, wu_ref[0], preferred_element_type=F32) + bu_ref[0]
        gate = jnp.minimum(gate, SWIGLU_LIMIT)
        up = jnp.clip(up, -SWIGLU_LIMIT, SWIGLU_LIMIT)
        glu = gate * jax.nn.sigmoid(SWIGLU_ALPHA * gate)
        act = ((up + 1.0) * glu).astype(BF16)
        acc_ref[...] += jnp.dot(act, wd_ref[0], preferred_element_type=F32)

        @pl.when(f == pl.num_programs(1) - 1)
        def _():
            o_ref[...] = _to_row_tiles((acc_ref[...] + bd_ref[0]).astype(o_ref.dtype))

    @pl.when(jnp.logical_and(b >= nu_ref[0], f == 0))
    def _():
        o_ref[...] = jnp.zeros(o_ref.shape, o_ref.dtype)


def _experts(block_expert, nused, xs, wg, bg, wu, bu, wd, bd, *, tm, tf=1024):
    n_slots = xs.shape[0]
    tile = xs.shape[1:]
    d = tile[0] * tile[1]
    nb = n_slots // tm
    dff = wg.shape[2]
    nf = dff // tf

    def blk(b, nu):
        return jnp.minimum(b, nu[0] - 1)

    def fidx(b, f, nu):
        return jnp.where(b < nu[0], f, nf - 1)

    return pl.pallas_call(
        _expert_kernel,
        grid_spec=pltpu.PrefetchScalarGridSpec(
            num_scalar_prefetch=2,
            grid=(nb, nf),
            in_specs=[
                pl.BlockSpec((tm,) + tile, lambda b, f, be, nu: (blk(b, nu), 0, 0)),
                pl.BlockSpec((1, d, tf), lambda b, f, be, nu: (be[blk(b, nu)], 0, fidx(b, f, nu))),
                pl.BlockSpec((1, 1, tf), lambda b, f, be, nu: (be[blk(b, nu)], 0, fidx(b, f, nu))),
                pl.BlockSpec((1, d, tf), lambda b, f, be, nu: (be[blk(b, nu)], 0, fidx(b, f, nu))),
                pl.BlockSpec((1, 1, tf), lambda b, f, be, nu: (be[blk(b, nu)], 0, fidx(b, f, nu))),
                pl.BlockSpec((1, tf, d), lambda b, f, be, nu: (be[blk(b, nu)], fidx(b, f, nu), 0)),
                pl.BlockSpec((1, 1, d), lambda b, f, be, nu: (be[blk(b, nu)], 0, 0)),
            ],
            out_specs=pl.BlockSpec((tm,) + tile, lambda b, f, be, nu: (b, 0, 0)),
            scratch_shapes=[pltpu.VMEM((tm, d), BF16), pltpu.VMEM((tm, d), F32)],
        ),
        out_shape=jax.ShapeDtypeStruct((n_slots,) + tile, BF16),
        compiler_params=_cparams(("arbitrary", "arbitrary")),
        name="experts",
    )(block_expert, nused, xs, wg, bg, wu, bu, wd, bd)


def _combine_kernel(pos_ref, x1_ref, wt_ref, ys_hbm, o_ref, buf_ref, sem, *, tm):
    def issue(r, c):
        for k in range(TOP_K):
            pltpu.make_async_copy(ys_hbm.at[pl.ds(pos_ref[0, 0, k * tm + r], 1)],
                                  buf_ref.at[k, pl.ds(r, 1)], sem).start()
        return c

    lax.fori_loop(0, tm, issue, 0, unroll=4)
    pltpu.make_async_copy(buf_ref, buf_ref, sem).wait()
    acc = x1_ref[...]
    for k in range(TOP_K):
        acc = acc + wt_ref[:, k:k + 1] * _from_row_tiles(buf_ref[k]).astype(F32)
    o_ref[...] = acc


def _combine(pos_km, x1, wt, ys, *, tm=128):
    t, d = x1.shape
    return pl.pallas_call(
        functools.partial(_combine_kernel, tm=tm),
        grid=(t // tm,),
        in_specs=[
            pl.BlockSpec((1, 1, TOP_K * tm), lambda i: (i, 0, 0), memory_space=pltpu.SMEM),
            pl.BlockSpec((tm, d), lambda i: (i, 0)),
            pl.BlockSpec((tm, LANES), lambda i: (i, 0)),
            pl.BlockSpec(memory_space=pl.ANY),
        ],
        out_specs=pl.BlockSpec((tm, d), lambda i: (i, 0)),
        out_shape=jax.ShapeDtypeStruct((t, d), F32),
        scratch_shapes=[pltpu.VMEM((TOP_K, tm) + ys.shape[1:], ys.dtype), pltpu.SemaphoreType.DMA(())],
        compiler_params=_cparams(("arbitrary",)),
        name="combine",
    )(pos_km, x1, wt, ys)


def _pad_heads(w, per_head):
    lead = w.shape[:-1]
    w = w.reshape(lead + (N_HEADS, per_head))
    w = jnp.pad(w, [(0, 0)] * len(lead) + [(0, 0), (0, HEAD_PAD - per_head)])
    return w.reshape(lead + (N_HEADS * HEAD_PAD,))


def _s5_discretise(lam_re, lam_im, log_dt, b_re, b_im):
    dt = jnp.exp(log_dt.astype(F32))[:, None]
    lr, li = lam_re.astype(F32), lam_im.astype(F32)
    mag = jnp.exp(lr * dt)
    ar, ai = mag * jnp.cos(li * dt), mag * jnp.sin(li * dt)
    den = lr * lr + li * li
    zr = ((ar - 1.0) * lr + ai * li) / den
    zi = (ai * lr - (ar - 1.0) * li) / den
    br, bi = b_re.astype(F32), b_im.astype(F32)
    bbr = zr[..., None] * br - zi[..., None] * bi
    bbi = zr[..., None] * bi + zi[..., None] * br
    return ar, ai, bbr, bbi


def _s5_pack(ar, ai, bbr, bbi, c_re, c_im, nb):
    eye = jnp.eye(S5_CLUSTER, dtype=F32)

    def pack_b(m):
        m4 = m.reshape(N_CLUSTERS, S5_CLUSTER, S5_STATE, S5_GROUP)
        return jnp.einsum('xgpc,gh->xgchp', m4, eye).reshape(N_CLUSTERS, S5_CLUSTER * S5_GROUP, CLUSTER_STATES)

    def pack_c(m):
        m4 = m.reshape(N_CLUSTERS, S5_CLUSTER, S5_GROUP, S5_STATE)
        return jnp.einsum('xgcp,gh->xgphc', m4, eye).reshape(N_CLUSTERS, CLUSTER_STATES, S5_CLUSTER * S5_GROUP)

    bb = jnp.concatenate([pack_b(bbr), pack_b(bbi)], axis=2).astype(BF16)
    cc = jnp.concatenate([pack_c(c_re.astype(F32)), -pack_c(c_im.astype(F32))], axis=1).astype(BF16)
    a = jnp.concatenate([ar.reshape(N_CLUSTERS, CLUSTER_STATES), ai.reshape(N_CLUSTERS, CLUSTER_STATES)], axis=1)
    a_bc = jnp.broadcast_to(a[:, None, :], (N_CLUSTERS, nb, 2 * CLUSTER_STATES))
    return bb, cc, a_bc


def _perm_matrix(nb):
    n = nb * SUB_T
    p = np.zeros((n, n), np.float32)
    for b in range(nb):
        for t in range(SUB_T):
            p[t * nb + b, b * SUB_T + t] = 1.0
    return p


def kernel(x, positions, norm1_g, w_in, b_gates, q_norm_g, w_uq, kv_norm_g, w_ukv, qk_norm_q_g, qk_norm_k_g, w_o_mla, s5_lambda_re, s5_lambda_im, s5_log_dt, s5_b_re, s5_b_im, s5_c_re, s5_c_im, s5_d, w_glu, w_o_s5, w_out, norm2_g, w_router, b_router, w_gate, b_gate, w_up, b_up, w_down, b_down):
    bsz, seq, d = x.shape
    t = bsz * seq
    depth = norm1_g.shape[0]
    o1 = Q_LORA_RANK
    o2 = o1 + KV_LORA_RANK
    o3 = o2 + QK_ROPE_DIM
    o4 = o3 + S5_WIDTH
    half = ROPE_HALF
    inv_freq = ROPE_THETA ** (-jnp.arange(half, dtype=F32) / half)
    lane = np.arange(LANES)
    freq = jnp.where(lane < QK_ROPE_DIM, jnp.tile(inv_freq, LANES // half), 0.0).reshape(1, LANES).astype(F32)
    sign = jnp.asarray(np.where(lane % QK_ROPE_DIM < half, -1.0, 1.0).reshape(1, LANES), F32)
    pos = positions.reshape(t, 1).astype(F32)
    sm_scale = math.log2(math.e) / math.sqrt(QK_HEAD_DIM)
    ind_q = jnp.asarray(np.equal.outer(np.arange(N_HEADS * HEAD_PAD) // HEAD_PAD, lane), BF16)
    ind_k = jnp.asarray(np.equal.outer(np.arange(N_HEADS * QK_NOPE_DIM) // QK_NOPE_DIM, lane), BF16)
    perm_np = _perm_matrix(bsz)
    perm = jnp.asarray(perm_np, BF16)
    permt = jnp.asarray(perm_np.T, BF16)
    tm_e = 512
    n_assign = t * TOP_K
    nb_e = n_assign // tm_e + N_EXPERTS
    n_slots = nb_e * tm_e
    tm_c = 128

    for l in range(depth):
        wi = w_in[l]
        w_in_p = jnp.concatenate(
            [wi[:, o4:], wi[:, o3:o4], wi[:, :o1], wi[:, o1:o2], wi[:, o2:o3],
             jnp.zeros((d, PROJ_W - wi.shape[1]), wi.dtype)], axis=1).astype(BF16)
        w_uq_p = _pad_heads(w_uq[l], QK_HEAD_DIM).astype(BF16)
        gq_full = _pad_heads(jnp.tile(qk_norm_q_g[l].astype(F32), N_HEADS) * sm_scale, QK_HEAD_DIM).reshape(1, -1)
        wkv = w_ukv[l].reshape(KV_LORA_RANK, N_HEADS, QK_NOPE_DIM + V_HEAD_DIM)
        w_kv_p = jnp.concatenate(
            [wkv[:, :, :QK_NOPE_DIM].reshape(KV_LORA_RANK, -1), wkv[:, :, QK_NOPE_DIM:].reshape(KV_LORA_RANK, -1)],
            axis=1).astype(BF16)
        gk = qk_norm_k_g[l].astype(F32)
        gk_nope = gk[:QK_NOPE_DIM].reshape(1, LANES)
        gk_rope = jnp.pad(gk[QK_NOPE_DIM:], (0, LANES - QK_ROPE_DIM)).reshape(1, LANES)
        ar, ai, bbr, bbi = _s5_discretise(s5_lambda_re[l], s5_lambda_im[l], s5_log_dt[l], s5_b_re[l], s5_b_im[l])
        bb, cc, a_bc = _s5_pack(ar, ai, bbr, bbi, s5_c_re[l], s5_c_im[l], bsz)
        wr = jnp.pad(w_router[l].astype(F32), ((0, 0), (0, LANES - N_EXPERTS)))
        wr_hi = wr.astype(BF16)
        wr_lo = (wr - wr_hi.astype(F32)).astype(BF16)
        br = jnp.pad(b_router[l].astype(F32), (0, LANES - N_EXPERTS)).reshape(1, LANES)

        x2 = x.reshape(t, d)
        proj = _inproj(x2, norm1_g[l].reshape(1, d), w_in_p)
        q = _qprep(proj, pos, q_norm_g[l].reshape(1, -1), w_uq_p, gq_full, ind_q, freq, sign)
        k, v = _kprep(proj, pos, kv_norm_g[l].reshape(1, -1), w_kv_p, gk_nope, gk_rope, ind_k, freq, sign)
        o = _attention(q.reshape(bsz, seq, -1), k.reshape(bsz, seq, -1), v.reshape(bsz, seq, -1))
        yb = _s5(proj.reshape(bsz, seq, PROJ_W), perm, permt, bb, cc, a_bc, s5_d[l].reshape(1, -1).astype(F32),
                 w_glu[l].astype(BF16), w_o_s5[l].astype(BF16))
        m = _merge(o.reshape(t, -1), proj, b_gates[l].reshape(1, -1).astype(F32), yb.reshape(t, d),
                   w_o_mla[l].astype(BF16))
        x1, h2, top_idx, top_w = _outproj(m, x2, w_out[l].astype(BF16), norm2_g[l].reshape(1, d), wr_hi, wr_lo, br)

        flat_e = top_idx[:, :TOP_K].reshape(-1)
        onehot = (flat_e[:, None] == jnp.arange(N_EXPERTS, dtype=jnp.int32)[None, :]).astype(jnp.int32)
        csum = jnp.cumsum(onehot, axis=0)
        rank = jnp.take_along_axis(csum, flat_e[:, None], axis=1)[:, 0] - 1
        counts = csum[-1]
        nblk = (counts + tm_e - 1) // tm_e
        blk_end = jnp.cumsum(nblk)
        blk_start = blk_end - nblk
        slot = blk_start[flat_e] * tm_e + rank
        nused = blk_end[-1:].astype(jnp.int32)
        block_expert = jnp.minimum(
            jnp.sum((blk_end[None, :] <= jnp.arange(nb_e, dtype=jnp.int32)[:, None]).astype(jnp.int32), axis=1),
            N_EXPERTS - 1).astype(jnp.int32)
        slot_tok = jnp.zeros((n_slots,), jnp.int32).at[slot].set(jnp.arange(n_assign, dtype=jnp.int32) // TOP_K)
        pos_km = slot.reshape(t // tm_c, tm_c, TOP_K).transpose(0, 2, 1).reshape(t // tm_c, 1, TOP_K * tm_c)

        xs = _dispatch(nused, slot_tok, h2, n_slots, g=tm_e)
        ys = _experts(block_expert, nused, xs,
                      w_gate[l].astype(BF16), b_gate[l].reshape(N_EXPERTS, 1, -1).astype(F32),
                      w_up[l].astype(BF16), b_up[l].reshape(N_EXPERTS, 1, -1).astype(F32),
                      w_down[l].astype(BF16), b_down[l].reshape(N_EXPERTS, 1, -1).astype(F32), tm=tm_e)
        x = _combine(pos_km, x1, top_w, ys, tm=tm_c).reshape(bsz, seq, d)
    return x
```

```python
import functools
import math

import numpy as np
import jax
import jax.numpy as jnp
from jax import lax
from jax.experimental import pallas as pl
from jax.experimental.pallas import tpu as pltpu

F32 = jnp.float32
BF16 = jnp.bfloat16

D_MODEL = 2048
CHUNK = 64
EPS = 1e-6
N_HEADS = 16
QK_NOPE_DIM = 128
QK_ROPE_DIM = 64
QK_HEAD_DIM = QK_NOPE_DIM + QK_ROPE_DIM
V_HEAD_DIM = 128
Q_LORA_RANK = 512
KV_LORA_RANK = 256
ROPE_THETA = 10000.0
S5_WIDTH = 1024
S5_GROUP = 16
S5_GROUPS = S5_WIDTH // S5_GROUP
S5_STATE = 64
N_EXPERTS = 32
TOP_K = 4
D_FF = 2048
SWIGLU_LIMIT = 7.0
SWIGLU_ALPHA = 1.702

LANES = 128
HEAD_PAD = 256
ROPE_HALF = QK_ROPE_DIM // 2
S5_CLUSTER = 8
N_CLUSTERS = S5_GROUPS // S5_CLUSTER
CLUSTER_STATES = S5_CLUSTER * S5_STATE
SUB_T = 32
PROJ_W = 6144
VMEM_LIMIT = 56 * 1024 * 1024
NEG = -1e30


def _cparams(sem, vmem=VMEM_LIMIT, **kw):
    return pltpu.CompilerParams(dimension_semantics=sem, vmem_limit_bytes=vmem, **kw)


def _inproj_kernel(x_ref, g_ref, w_ref, o_ref, h_ref):
    @pl.when(pl.program_id(1) == 0)
    def _():
        x = x_ref[...]
        ms = jnp.mean(x * x, axis=-1, keepdims=True)
        h_ref[...] = (x * lax.rsqrt(ms + EPS) * g_ref[...]).astype(BF16)

    o_ref[...] = jnp.dot(h_ref[...], w_ref[...], preferred_element_type=F32).astype(o_ref.dtype)


def _inproj(x2, g, w, *, tm=1024, tn=1536):
    t, d = x2.shape
    n = w.shape[1]
    return pl.pallas_call(
        _inproj_kernel,
        grid=(t // tm, n // tn),
        in_specs=[
            pl.BlockSpec((tm, d), lambda i, j: (i, 0)),
            pl.BlockSpec((1, d), lambda i, j: (0, 0)),
            pl.BlockSpec((d, tn), lambda i, j: (0, j)),
        ],
        out_specs=pl.BlockSpec((tm, tn), lambda i, j: (i, j)),
        out_shape=jax.ShapeDtypeStruct((t, n), BF16),
        scratch_shapes=[pltpu.VMEM((tm, d), BF16)],
        compiler_params=_cparams(("parallel", "arbitrary")),
        name="inproj",
    )(x2, g, w)


def _rope_mid(mid, cos, sin_signed):
    lane = lax.broadcasted_iota(jnp.int32, mid.shape, 1)
    rot = jnp.where(lane < ROPE_HALF, pltpu.roll(mid, LANES - ROPE_HALF, 1), pltpu.roll(mid, ROPE_HALF, 1))
    return mid * cos + rot * sin_signed


def _qprep_kernel(ql_ref, pos_ref, gn_ref, w_ref, gfull_ref, ind_ref, freq_ref, sign_ref, o_ref):
    ql = ql_ref[...].astype(F32)
    ms = jnp.mean(ql * ql, axis=-1, keepdims=True)
    qn = (ql * lax.rsqrt(ms + EPS) * gn_ref[...]).astype(BF16)
    q = jnp.dot(qn, w_ref[...], preferred_element_type=F32)
    ssq = jnp.dot((q * q).astype(BF16), ind_ref[...], preferred_element_type=F32)
    sc = lax.rsqrt(ssq * (1.0 / QK_HEAD_DIM) + EPS)
    ang = pos_ref[...] * freq_ref[...]
    cos = jnp.cos(ang)
    sin_signed = jnp.sin(ang) * sign_ref[...]
    for h in range(N_HEADS):
        s_h = sc[:, h:h + 1]
        lo = h * HEAD_PAD
        nope = q[:, lo:lo + LANES] * s_h * gfull_ref[:, lo:lo + LANES]
        mid = q[:, lo + LANES:lo + HEAD_PAD] * s_h * gfull_ref[:, lo + LANES:lo + HEAD_PAD]
        o_ref[:, lo:lo + LANES] = nope.astype(o_ref.dtype)
        o_ref[:, lo + LANES:lo + HEAD_PAD] = _rope_mid(mid, cos, sin_signed).astype(o_ref.dtype)


def _qprep(proj, pos, gn, w, gfull, ind, freq, sign, *, tm=512):
    t = proj.shape[0]
    n = N_HEADS * HEAD_PAD
    full = lambda shape: pl.BlockSpec(shape, lambda i: (0, 0))
    return pl.pallas_call(
        _qprep_kernel,
        grid=(t // tm,),
        in_specs=[
            pl.BlockSpec((tm, Q_LORA_RANK), lambda i: (i, 10)),
            pl.BlockSpec((tm, 1), lambda i: (i, 0)),
            full((1, Q_LORA_RANK)),
            full((Q_LORA_RANK, n)),
            full((1, n)),
            full((n, LANES)),
            full((1, LANES)),
            full((1, LANES)),
        ],
        out_specs=pl.BlockSpec((tm, n), lambda i: (i, 0)),
        out_shape=jax.ShapeDtypeStruct((t, n), BF16),
        compiler_params=_cparams(("parallel",)),
        name="qprep",
    )(proj, pos, gn, w, gfull, ind, freq, sign)


def _kprep_kernel(kvl_ref, kr_ref, pos_ref, gn_ref, w_ref, gnope_ref, grope_ref, ind_ref, freq_ref, sign_ref,
                  k_ref, v_ref):
    kvl = kvl_ref[...].astype(F32)
    ms = jnp.mean(kvl * kvl, axis=-1, keepdims=True)
    kn = (kvl * lax.rsqrt(ms + EPS) * gn_ref[...]).astype(BF16)
    kv = jnp.dot(kn, w_ref[...], preferred_element_type=F32)
    nd = N_HEADS * QK_NOPE_DIM
    knope = kv[:, :nd]
    v_ref[...] = kv[:, nd:].astype(v_ref.dtype)
    kr = kr_ref[...].astype(F32)
    ssq = jnp.dot((knope * knope).astype(BF16), ind_ref[...], preferred_element_type=F32)
    ssq = ssq + jnp.sum(kr * kr, axis=-1, keepdims=True)
    sc = lax.rsqrt(ssq * (1.0 / QK_HEAD_DIM) + EPS)
    ang = pos_ref[...] * freq_ref[...]
    kr_rot = _rope_mid(kr * grope_ref[...], jnp.cos(ang), jnp.sin(ang) * sign_ref[...])
    for h in range(N_HEADS):
        s_h = sc[:, h:h + 1]
        lo = h * HEAD_PAD
        nope = knope[:, h * LANES:(h + 1) * LANES] * s_h * gnope_ref[...]
        k_ref[:, lo:lo + LANES] = nope.astype(k_ref.dtype)
        k_ref[:, lo + LANES:lo + HEAD_PAD] = (kr_rot * s_h).astype(k_ref.dtype)


def _kprep(proj, pos, gn, w, gnope, grope, ind, freq, sign, *, tm=512):
    t = proj.shape[0]
    n = N_HEADS * HEAD_PAD
    nd = N_HEADS * QK_NOPE_DIM
    full = lambda shape: pl.BlockSpec(shape, lambda i: (0, 0))
    return pl.pallas_call(
        _kprep_kernel,
        grid=(t // tm,),
        in_specs=[
            pl.BlockSpec((tm, KV_LORA_RANK), lambda i: (i, 22)),
            pl.BlockSpec((tm, LANES), lambda i: (i, 46)),
            pl.BlockSpec((tm, 1), lambda i: (i, 0)),
            full((1, KV_LORA_RANK)),
            full((KV_LORA_RANK, 2 * nd)),
            full((1, LANES)),
            full((1, LANES)),
            full((nd, LANES)),
            full((1, LANES)),
            full((1, LANES)),
        ],
        out_specs=[pl.BlockSpec((tm, n), lambda i: (i, 0)), pl.BlockSpec((tm, nd), lambda i: (i, 0))],
        out_shape=[jax.ShapeDtypeStruct((t, n), BF16), jax.ShapeDtypeStruct((t, nd), BF16)],
        compiler_params=_cparams(("parallel",)),
        name="kprep",
    )(proj, proj, pos, gn, w, gnope, grope, ind, freq, sign)


def _attn_kernel(q_ref, k_ref, v_ref, wg_ref, wu_ref, wd_ref, o_ref, wgo_ref, wuo_ref, wdo_ref,
                 m_ref, l_ref, acc_ref, *, tq, hg):
    wgo_ref[...] = wg_ref[...].astype(wgo_ref.dtype)
    wuo_ref[...] = wu_ref[...].astype(wuo_ref.dtype)
    wdo_ref[...] = wd_ref[...].astype(wdo_ref.dtype)

    qi = pl.program_id(2)
    m_ref[...] = jnp.full(m_ref.shape, NEG, F32)
    l_ref[...] = jnp.zeros(l_ref.shape, F32)
    acc_ref[...] = jnp.zeros(acc_ref.shape, F32)

    def block(j, masked):
        r0 = pl.multiple_of(j * tq, tq)
        for h in range(hg):
            q = q_ref[:, h * HEAD_PAD:(h + 1) * HEAD_PAD]
            kb = k_ref[pl.ds(r0, tq), h * HEAD_PAD:(h + 1) * HEAD_PAD]
            vb = v_ref[pl.ds(r0, tq), h * V_HEAD_DIM:(h + 1) * V_HEAD_DIM]
            st = lax.dot_general(kb, q, (((1,), (1,)), ((), ())), preferred_element_type=F32)
            if masked:
                key = lax.broadcasted_iota(jnp.int32, st.shape, 0) // CHUNK
                qry = lax.broadcasted_iota(jnp.int32, st.shape, 1) // CHUNK
                st = jnp.where(key <= qry, st, NEG)
            m_old = m_ref[h]
            m_new = jnp.maximum(m_old, jnp.max(st, axis=0, keepdims=True))
            alpha = jnp.exp2(m_old - m_new)
            p = jnp.exp2(st - m_new)
            l_ref[h] = alpha * l_ref[h] + jnp.sum(p, axis=0, keepdims=True)
            pv = lax.dot_general(vb, p.astype(BF16), (((0,), (0,)), ((), ())), preferred_element_type=F32)
            acc_ref[h] = alpha * acc_ref[h] + pv
            m_ref[h] = m_new

    def body(j, c):
        block(j, False)
        return c

    lax.fori_loop(0, qi, body, 0)
    block(qi, True)
    for h in range(hg):
        o = acc_ref[h] / l_ref[h]
        o_ref[:, h * V_HEAD_DIM:(h + 1) * V_HEAD_DIM] = o.T.astype(o_ref.dtype)


def _attention(q3, k3, v3, wg, wu, wd, *, tq=512, hg=2):
    b, s, _ = q3.shape
    tq = min(tq, s)
    ng, nq = N_HEADS // hg, s // tq
    steps = b * ng * nq
    w2 = [w.reshape(-1, w.shape[-1]) for w in (wg, wu, wd)]
    slab = [w.shape[0] // steps for w in w2]
    step = lambda bi, h, i: ((bi * ng + h) * nq + i, 0)
    wspec = [pl.BlockSpec((r, w.shape[1]), step) for r, w in zip(slab, w2)]
    outs = pl.pallas_call(
        functools.partial(_attn_kernel, tq=tq, hg=hg),
        grid=(b, ng, nq),
        in_specs=[
            pl.BlockSpec((None, tq, hg * HEAD_PAD), lambda bi, h, i: (bi, i, h)),
            pl.BlockSpec((None, s, hg * HEAD_PAD), lambda bi, h, i: (bi, 0, h)),
            pl.BlockSpec((None, s, hg * V_HEAD_DIM), lambda bi, h, i: (bi, 0, h)),
        ] + wspec,
        out_specs=[pl.BlockSpec((None, tq, hg * V_HEAD_DIM), lambda bi, h, i: (bi, i, h))] + wspec,
        out_shape=[jax.ShapeDtypeStruct((b, s, N_HEADS * V_HEAD_DIM), BF16)]
        + [jax.ShapeDtypeStruct(w.shape, BF16) for w in w2],
        scratch_shapes=[pltpu.VMEM((hg, 1, tq), F32), pltpu.VMEM((hg, 1, tq), F32),
                        pltpu.VMEM((hg, V_HEAD_DIM, tq), F32)],
        compiler_params=_cparams(("parallel", "parallel", "arbitrary")),
        name="attention",
    )(q3, k3, v3, *w2)
    return (outs[0],) + tuple(o.reshape(w.shape) for o, w in zip(outs[1:], (wg, wu, wd)))


def _s5_kernel(u_ref, perm_ref, permt_ref, bb_ref, cc_ref, a_ref, d_ref, wglu_ref, wo_ref, o_ref,
               ut_ref, bu_ref, y_ref, carry_ref, *, tc):
    nb = u_ref.shape[0]
    rows_sub = nb * SUB_T
    nsub = tc // SUB_T

    @pl.when(pl.program_id(0) == 0)
    def _():
        carry_ref[...] = jnp.zeros(carry_ref.shape, F32)

    for j in range(nsub):
        ub = u_ref[:, j * SUB_T:(j + 1) * SUB_T, :].reshape(rows_sub, S5_WIDTH)
        ut_ref[j * rows_sub:(j + 1) * rows_sub, :] = jnp.dot(
            perm_ref[...], ub, preferred_element_type=F32).astype(BF16)

    for c in range(N_CLUSTERS):
        bu_ref[...] = jnp.dot(ut_ref[:, c * LANES:(c + 1) * LANES], bb_ref[c], preferred_element_type=F32)
        ar = a_ref[c, :, :CLUSTER_STATES]
        ai = a_ref[c, :, CLUSTER_STATES:]

        def step(t, carry):
            xr, xi = carry
            r0 = pl.multiple_of(t * nb, nb)
            br = bu_ref[pl.ds(r0, nb), :CLUSTER_STATES]
            bi = bu_ref[pl.ds(r0, nb), CLUSTER_STATES:]
            nxr = ar * xr - ai * xi + br
            nxi = ar * xi + ai * xr + bi
            bu_ref[pl.ds(r0, nb), :CLUSTER_STATES] = nxr
            bu_ref[pl.ds(r0, nb), CLUSTER_STATES:] = nxi
            return nxr, nxi

        xr, xi = lax.fori_loop(0, tc, step, (carry_ref[c, :, :CLUSTER_STATES], carry_ref[c, :, CLUSTER_STATES:]),
                               unroll=8)
        carry_ref[c, :, :CLUSTER_STATES] = xr
        carry_ref[c, :, CLUSTER_STATES:] = xi
        y_ref[:, c * LANES:(c + 1) * LANES] = jnp.dot(bu_ref[...].astype(BF16), cc_ref[c], preferred_element_type=F32)

    y = y_ref[...] + d_ref[...] * ut_ref[...].astype(F32)
    g = jax.nn.gelu(y)
    z = g * jax.nn.sigmoid(jnp.dot(g.astype(BF16), wglu_ref[...], preferred_element_type=F32))
    yb = jnp.dot(z.astype(BF16), wo_ref[...], preferred_element_type=F32).astype(BF16)
    for j in range(nsub):
        blk = jnp.dot(permt_ref[...], yb[j * rows_sub:(j + 1) * rows_sub, :], preferred_element_type=F32)
        o_ref[:, j * SUB_T:(j + 1) * SUB_T, :] = blk.astype(o_ref.dtype).reshape(nb, SUB_T, o_ref.shape[2])


def _s5(proj3, perm, permt, bb, cc, a_bc, d, wglu, wo, *, tc=64):
    b, s, _ = proj3.shape
    rows = b * tc
    dm = wo.shape[1]
    c2 = lambda shape: pl.BlockSpec(shape, lambda i: (0, 0))
    c3 = lambda shape: pl.BlockSpec(shape, lambda i: (0, 0, 0))
    return pl.pallas_call(
        functools.partial(_s5_kernel, tc=tc),
        grid=(s // tc,),
        in_specs=[
            pl.BlockSpec((b, tc, S5_WIDTH), lambda i: (0, i, 4)),
            c2(perm.shape), c2(permt.shape), c3(bb.shape), c3(cc.shape), c3(a_bc.shape), c2(d.shape),
            c2(wglu.shape), c2(wo.shape),
        ],
        out_specs=pl.BlockSpec((b, tc, dm), lambda i: (0, i, 0)),
        out_shape=jax.ShapeDtypeStruct((b, s, dm), BF16),
        scratch_shapes=[
            pltpu.VMEM((rows, S5_WIDTH), BF16),
            pltpu.VMEM((rows, 2 * CLUSTER_STATES), F32),
            pltpu.VMEM((rows, S5_WIDTH), F32),
            pltpu.VMEM((N_CLUSTERS, b, 2 * CLUSTER_STATES), F32),
        ],
        compiler_params=_cparams(("arbitrary",)),
        name="s5",
    )(proj3, perm, permt, bb, cc, a_bc, d, wglu, wo)


def _merge_kernel(o_ref, gl_ref, bg_ref, yb_ref, w_ref, m_ref):
    d = o_ref.shape[1]
    ya = jnp.dot(o_ref[...], w_ref[...], preferred_element_type=F32)
    ga = jax.nn.sigmoid(gl_ref[:, :d].astype(F32) + bg_ref[:, :d])
    gb = jax.nn.sigmoid(gl_ref[:, d:].astype(F32) + bg_ref[:, d:])
    m_ref[...] = (ga * ya + gb * yb_ref[...].astype(F32)).astype(m_ref.dtype)


def _merge(o2, proj, bg, yb2, w, *, tm=512):
    t, d = o2.shape
    return pl.pallas_call(
        _merge_kernel,
        grid=(t // tm,),
        in_specs=[
            pl.BlockSpec((tm, d), lambda i: (i, 0)),
            pl.BlockSpec((tm, 2 * d), lambda i: (i, 0)),
            pl.BlockSpec((1, 2 * d), lambda i: (0, 0)),
            pl.BlockSpec((tm, d), lambda i: (i, 0)),
            pl.BlockSpec((d, d), lambda i: (0, 0)),
        ],
        out_specs=pl.BlockSpec((tm, d), lambda i: (i, 0)),
        out_shape=jax.ShapeDtypeStruct((t, d), BF16),
        compiler_params=_cparams(("parallel",)),
        name="merge",
    )(o2, proj, bg, yb2, w)


def _to_row_tiles(x):
    return x.reshape(x.shape[0], x.shape[1] // LANES, LANES)


def _from_row_tiles(x):
    return x.reshape(x.shape[0], x.shape[1] * LANES)


def _outproj_kernel(m_ref, x_ref, w_ref, g_ref, wrh_ref, wrl_ref, br_ref, x1_ref, h2_ref, idx_ref, wt_ref):
    x1 = x_ref[...] + jnp.dot(m_ref[...], w_ref[...], preferred_element_type=F32)
    x1_ref[...] = x1
    ms = jnp.mean(x1 * x1, axis=-1, keepdims=True)
    h2 = x1 * lax.rsqrt(ms + EPS) * g_ref[...]
    h_hi = h2.astype(BF16)
    h2_ref[...] = _to_row_tiles(h_hi)
    h_lo = (h2 - h_hi.astype(F32)).astype(BF16)
    logits = (jnp.dot(h_hi, wrh_ref[...], preferred_element_type=F32)
              + jnp.dot(h_lo, wrh_ref[...], preferred_element_type=F32)
              + jnp.dot(h_hi, wrl_ref[...], preferred_element_type=F32)) + br_ref[...]
    lane = lax.broadcasted_iota(jnp.int32, logits.shape, 1)
    work = jnp.where(lane < N_EXPERTS, logits, -jnp.inf)
    idx_out = jnp.zeros(logits.shape, jnp.int32)
    val_out = jnp.zeros(logits.shape, F32)
    v0 = None
    denom = None
    for k in range(TOP_K):
        mx = jnp.max(work, axis=-1, keepdims=True)
        sel = jnp.min(jnp.where(work == mx, lane, LANES), axis=-1, keepdims=True)
        if k == 0:
            v0 = mx
        e = jnp.exp(mx - v0)
        denom = e if k == 0 else denom + e
        idx_out = jnp.where(lane == k, sel, idx_out)
        val_out = jnp.where(lane == k, e, val_out)
        work = jnp.where(lane == sel, -jnp.inf, work)
    idx_ref[...] = idx_out
    wt_ref[...] = val_out / denom


def _outproj(m2, x2, w, g, wrh, wrl, br, *, tm=512):
    t, d = x2.shape
    c2 = lambda shape: pl.BlockSpec(shape, lambda i: (0, 0))
    row = lambda width: pl.BlockSpec((tm, width), lambda i: (i, 0))
    return pl.pallas_call(
        _outproj_kernel,
        grid=(t // tm,),
        in_specs=[row(d), row(d), c2((d, d)), c2((1, d)), c2((d, LANES)), c2((d, LANES)), c2((1, LANES))],
        out_specs=[row(d), pl.BlockSpec((tm, d // LANES, LANES), lambda i: (i, 0, 0)), row(LANES), row(LANES)],
        out_shape=[
            jax.ShapeDtypeStruct((t, d), F32),
            jax.ShapeDtypeStruct((t, d // LANES, LANES), BF16),
            jax.ShapeDtypeStruct((t, LANES), jnp.int32),
            jax.ShapeDtypeStruct((t, LANES), F32),
        ],
        compiler_params=_cparams(("parallel",)),
        name="outproj",
    )(m2, x2, w, g, wrh, wrl, br)


N_DMA_PRIORITIES = 2


def _start_row_gather(idx_ref, n_rows, src_hbm, dst_ref, sem):
    def issue(r2, c):
        for u in range(N_DMA_PRIORITIES):
            r = r2 * N_DMA_PRIORITIES + u
            pltpu.make_async_copy(src_hbm.at[pl.ds(idx_ref[0, 0, r], 1)], dst_ref.at[pl.ds(r, 1)], sem).start(priority=u)
        return c

    lax.fori_loop(0, n_rows // N_DMA_PRIORITIES, issue, 0, unroll=4)


def _wait_rows(dst_ref, sem):
    pltpu.make_async_copy(dst_ref, dst_ref, sem).wait()


def _dispatch_kernel(nu_ref, tokc_ref, tokn_ref, h_hbm, o_ref, buf_ref, sem, *, g):
    i = pl.program_id(0)
    nxt = i + 1

    @pl.when(jnp.logical_and(i == 0, nu_ref[0] > 0))
    def _():
        _start_row_gather(tokc_ref, g, h_hbm, buf_ref.at[0], sem.at[0])

    @pl.when(jnp.logical_and(nxt < pl.num_programs(0), nxt < nu_ref[0]))
    def _():
        _start_row_gather(tokn_ref, g, h_hbm, buf_ref.at[nxt % 2], sem.at[nxt % 2])

    @pl.when(i < nu_ref[0])
    def _():
        _wait_rows(buf_ref.at[i % 2], sem.at[i % 2])
        o_ref[...] = buf_ref[i % 2]

    @pl.when(i >= nu_ref[0])
    def _():
        o_ref[...] = jnp.zeros(o_ref.shape, o_ref.dtype)


def _dispatch(nused, slot_tok, h2, n_slots, *, g):
    tile = h2.shape[1:]
    n = n_slots // g
    tok3 = slot_tok.reshape(n, 1, g)
    return pl.pallas_call(
        functools.partial(_dispatch_kernel, g=g),
        grid_spec=pltpu.PrefetchScalarGridSpec(
            num_scalar_prefetch=1,
            grid=(n,),
            in_specs=[
                pl.BlockSpec((1, 1, g), lambda i, nu: (i, 0, 0), memory_space=pltpu.SMEM),
                pl.BlockSpec((1, 1, g), lambda i, nu: (jnp.minimum(i + 1, n - 1), 0, 0), memory_space=pltpu.SMEM),
                pl.BlockSpec(memory_space=pl.ANY),
            ],
            out_specs=pl.BlockSpec((g,) + tile, lambda i, nu: (i, 0, 0)),
            scratch_shapes=[pltpu.VMEM((2, g) + tile, h2.dtype), pltpu.SemaphoreType.DMA((2,))],
        ),
        out_shape=jax.ShapeDtypeStruct((n_slots,) + tile, h2.dtype),
        compiler_params=_cparams(("arbitrary",)),
        name="dispatch",
    )(nused, tok3, tok3, h2)


def _expert_kernel(be_ref, nu_ref, x_ref, wg_ref, bg_ref, wu_ref, bu_ref, wd_ref, bd_ref, o_ref, xb_ref, acc_ref):
    b = pl.program_id(0)
    f = pl.program_id(1)

    @pl.when(b < nu_ref[0])
    def _():
        @pl.when(f == 0)
        def _():
            xb_ref[...] = _from_row_tiles(x_ref[...])
            acc_ref[...] = jnp.zeros(acc_ref.shape, F32)

        xb = xb_ref[...]
        gate = jnp.dot(xb, wg_ref[0], preferred_element_type=F32) + bg_ref[0]
        up = jnp.dot(xb, wu_ref[0], preferred_element_type=F32) + bu_ref[0]
        gate = jnp.minimum(gate, SWIGLU_LIMIT)
        up = jnp.clip(up, -SWIGLU_LIMIT, SWIGLU_LIMIT)
        glu = gate * jax.nn.sigmoid(SWIGLU_ALPHA * gate)
        act = ((up + 1.0) * glu).astype(BF16)
        acc_ref[...] += jnp.dot(act, wd_ref[0], preferred_element_type=F32)

        @pl.when(f == pl.num_programs(1) - 1)
        def _():
            o_ref[...] = _to_row_tiles((acc_ref[...] + bd_ref[0]).astype(o_ref.dtype))

    @pl.when(jnp.logical_and(b >= nu_ref[0], f == 0))
    def _():
        o_ref[...] = jnp.zeros(o_ref.shape, o_ref.dtype)


def _experts(block_expert, nused, xs, wg, bg, wu, bu, wd, bd, *, tm, tf=1024):
    n_slots = xs.shape[0]
    tile = xs.shape[1:]
    d = tile[0] * tile[1]
    nb = n_slots // tm
    dff = wg.shape[2]
    nf = dff // tf

    def blk(b, nu):
        return jnp.minimum(b, nu[0] - 1)

    def fidx(b, f, nu):
        return jnp.where(b < nu[0], f, nf - 1)

    return pl.pallas_call(
        _expert_kernel,
        grid_spec=pltpu.PrefetchScalarGridSpec(
            num_scalar_prefetch=2,
            grid=(nb, nf),
            in_specs=[
                pl.BlockSpec((tm,) + tile, lambda b, f, be, nu: (blk(b, nu), 0, 0)),
                pl.BlockSpec((1, d, tf), lambda b, f, be, nu: (be[blk(b, nu)], 0, fidx(b, f, nu))),
                pl.BlockSpec((1, 1, tf), lambda b, f, be, nu: (be[blk(b, nu)], 0, fidx(b, f, nu))),
                pl.BlockSpec((1, d, tf), lambda b, f, be, nu: (be[blk(b, nu)], 0, fidx(b, f, nu))),
                pl.BlockSpec((1, 1, tf), lambda b, f, be, nu: (be[blk(b, nu)], 0, fidx(b, f, nu))),
                pl.BlockSpec((1, tf, d), lambda b, f, be, nu: (be[blk(b, nu)], fidx(b, f, nu), 0)),
                pl.BlockSpec((1, 1, d), lambda b, f, be, nu: (be[blk(b, nu)], 0, 0)),
            ],
            out_specs=pl.BlockSpec((tm,) + tile, lambda b, f, be, nu: (b, 0, 0)),
            scratch_shapes=[pltpu.VMEM((tm, d), BF16), pltpu.VMEM((tm, d), F32)],
        ),
        out_shape=jax.ShapeDtypeStruct((n_slots,) + tile, BF16),
        compiler_params=_cparams(("arbitrary", "arbitrary")),
        name="experts",
    )(block_expert, nused, xs, wg, bg, wu, bu, wd, bd)


def _combine_kernel(posc_ref, posn_ref, x1_ref, wt_ref, ys_hbm, o_ref, buf_ref, sem, *, tm):
    i = pl.program_id(0)
    nxt = i + 1

    @pl.when(i == 0)
    def _():
        _start_row_gather(posc_ref, TOP_K * tm, ys_hbm, buf_ref.at[0], sem.at[0])

    @pl.when(nxt < pl.num_programs(0))
    def _():
        _start_row_gather(posn_ref, TOP_K * tm, ys_hbm, buf_ref.at[nxt % 2], sem.at[nxt % 2])

    slot = i % 2
    _wait_rows(buf_ref.at[slot], sem.at[slot])
    acc = x1_ref[...]
    for k in range(TOP_K):
        rows = buf_ref[slot, k * tm:(k + 1) * tm]
        acc = acc + wt_ref[:, k:k + 1] * _from_row_tiles(rows).astype(F32)
    o_ref[...] = acc


def _combine(pos_km, x1, wt, ys, *, tm=128):
    t, d = x1.shape
    n = t // tm
    return pl.pallas_call(
        functools.partial(_combine_kernel, tm=tm),
        grid=(n,),
        in_specs=[
            pl.BlockSpec((1, 1, TOP_K * tm), lambda i: (i, 0, 0), memory_space=pltpu.SMEM),
            pl.BlockSpec((1, 1, TOP_K * tm), lambda i: (jnp.minimum(i + 1, n - 1), 0, 0), memory_space=pltpu.SMEM),
            pl.BlockSpec((tm, d), lambda i: (i, 0)),
            pl.BlockSpec((tm, LANES), lambda i: (i, 0)),
            pl.BlockSpec(memory_space=pl.ANY),
        ],
        out_specs=pl.BlockSpec((tm, d), lambda i: (i, 0)),
        out_shape=jax.ShapeDtypeStruct((t, d), F32),
        scratch_shapes=[pltpu.VMEM((2, TOP_K * tm) + ys.shape[1:], ys.dtype), pltpu.SemaphoreType.DMA((2,))],
        compiler_params=_cparams(("arbitrary",)),
        name="combine",
    )(pos_km, pos_km, x1, wt, ys)


def _pad_heads(w, per_head):
    lead = w.shape[:-1]
    w = w.reshape(lead + (N_HEADS, per_head))
    w = jnp.pad(w, [(0, 0)] * len(lead) + [(0, 0), (0, HEAD_PAD - per_head)])
    return w.reshape(lead + (N_HEADS * HEAD_PAD,))


def _s5_discretise(lam_re, lam_im, log_dt, b_re, b_im):
    dt = jnp.exp(log_dt.astype(F32))[:, None]
    lr, li = lam_re.astype(F32), lam_im.astype(F32)
    mag = jnp.exp(lr * dt)
    ar, ai = mag * jnp.cos(li * dt), mag * jnp.sin(li * dt)
    den = lr * lr + li * li
    zr = ((ar - 1.0) * lr + ai * li) / den
    zi = (ai * lr - (ar - 1.0) * li) / den
    br, bi = b_re.astype(F32), b_im.astype(F32)
    bbr = zr[..., None] * br - zi[..., None] * bi
    bbi = zr[..., None] * bi + zi[..., None] * br
    return ar, ai, bbr, bbi


def _s5_pack(ar, ai, bbr, bbi, c_re, c_im, nb):
    eye = jnp.eye(S5_CLUSTER, dtype=F32)

    def pack_b(m):
        m4 = m.reshape(N_CLUSTERS, S5_CLUSTER, S5_STATE, S5_GROUP)
        return jnp.einsum('xgpc,gh->xgchp', m4, eye).reshape(N_CLUSTERS, S5_CLUSTER * S5_GROUP, CLUSTER_STATES)

    def pack_c(m):
        m4 = m.reshape(N_CLUSTERS, S5_CLUSTER, S5_GROUP, S5_STATE)
        return jnp.einsum('xgcp,gh->xgphc', m4, eye).reshape(N_CLUSTERS, CLUSTER_STATES, S5_CLUSTER * S5_GROUP)

    bb = jnp.concatenate([pack_b(bbr), pack_b(bbi)], axis=2).astype(BF16)
    cc = jnp.concatenate([pack_c(c_re.astype(F32)), -pack_c(c_im.astype(F32))], axis=1).astype(BF16)
    a = jnp.concatenate([ar.reshape(N_CLUSTERS, CLUSTER_STATES), ai.reshape(N_CLUSTERS, CLUSTER_STATES)], axis=1)
    a_bc = jnp.broadcast_to(a[:, None, :], (N_CLUSTERS, nb, 2 * CLUSTER_STATES))
    return bb, cc, a_bc


def _perm_matrix(nb):
    n = nb * SUB_T
    p = np.zeros((n, n), np.float32)
    for b in range(nb):
        for t in range(SUB_T):
            p[t * nb + b, b * SUB_T + t] = 1.0
    return p


def kernel(x, positions, norm1_g, w_in, b_gates, q_norm_g, w_uq, kv_norm_g, w_ukv, qk_norm_q_g, qk_norm_k_g, w_o_mla, s5_lambda_re, s5_lambda_im, s5_log_dt, s5_b_re, s5_b_im, s5_c_re, s5_c_im, s5_d, w_glu, w_o_s5, w_out, norm2_g, w_router, b_router, w_gate, b_gate, w_up, b_up, w_down, b_down):
    bsz, seq, d = x.shape
    t = bsz * seq
    depth = norm1_g.shape[0]
    o1 = Q_LORA_RANK
    o2 = o1 + KV_LORA_RANK
    o3 = o2 + QK_ROPE_DIM
    o4 = o3 + S5_WIDTH
    half = ROPE_HALF
    inv_freq = ROPE_THETA ** (-jnp.arange(half, dtype=F32) / half)
    lane = np.arange(LANES)
    freq = jnp.where(lane < QK_ROPE_DIM, jnp.tile(inv_freq, LANES // half), 0.0).reshape(1, LANES).astype(F32)
    sign = jnp.asarray(np.where(lane % QK_ROPE_DIM < half, -1.0, 1.0).reshape(1, LANES), F32)
    pos = positions.reshape(t, 1).astype(F32)
    sm_scale = math.log2(math.e) / math.sqrt(QK_HEAD_DIM)
    ind_q = jnp.asarray(np.equal.outer(np.arange(N_HEADS * HEAD_PAD) // HEAD_PAD, lane), BF16)
    ind_k = jnp.asarray(np.equal.outer(np.arange(N_HEADS * QK_NOPE_DIM) // QK_NOPE_DIM, lane), BF16)
    perm_np = _perm_matrix(bsz)
    perm = jnp.asarray(perm_np, BF16)
    permt = jnp.asarray(perm_np.T, BF16)
    tm_e = 512
    n_assign = t * TOP_K
    nb_e = n_assign // tm_e + N_EXPERTS
    n_slots = nb_e * tm_e
    tm_c = 128

    for l in range(depth):
        wi = w_in[l]
        w_in_p = jnp.concatenate(
            [wi[:, o4:], wi[:, o3:o4], wi[:, :o1], wi[:, o1:o2], wi[:, o2:o3],
             jnp.zeros((d, PROJ_W - wi.shape[1]), wi.dtype)], axis=1).astype(BF16)
        w_uq_p = _pad_heads(w_uq[l], QK_HEAD_DIM).astype(BF16)
        gq_full = _pad_heads(jnp.tile(qk_norm_q_g[l].astype(F32), N_HEADS) * sm_scale, QK_HEAD_DIM).reshape(1, -1)
        wkv = w_ukv[l].reshape(KV_LORA_RANK, N_HEADS, QK_NOPE_DIM + V_HEAD_DIM)
        w_kv_p = jnp.concatenate(
            [wkv[:, :, :QK_NOPE_DIM].reshape(KV_LORA_RANK, -1), wkv[:, :, QK_NOPE_DIM:].reshape(KV_LORA_RANK, -1)],
            axis=1).astype(BF16)
        gk = qk_norm_k_g[l].astype(F32)
        gk_nope = gk[:QK_NOPE_DIM].reshape(1, LANES)
        gk_rope = jnp.pad(gk[QK_NOPE_DIM:], (0, LANES - QK_ROPE_DIM)).reshape(1, LANES)
        ar, ai, bbr, bbi = _s5_discretise(s5_lambda_re[l], s5_lambda_im[l], s5_log_dt[l], s5_b_re[l], s5_b_im[l])
        bb, cc, a_bc = _s5_pack(ar, ai, bbr, bbi, s5_c_re[l], s5_c_im[l], bsz)
        wr = jnp.pad(w_router[l].astype(F32), ((0, 0), (0, LANES - N_EXPERTS)))
        wr_hi = wr.astype(BF16)
        wr_lo = (wr - wr_hi.astype(F32)).astype(BF16)
        br = jnp.pad(b_router[l].astype(F32), (0, LANES - N_EXPERTS)).reshape(1, LANES)

        x2 = x.reshape(t, d)
        proj = _inproj(x2, norm1_g[l].reshape(1, d), w_in_p)
        q = _qprep(proj, pos, q_norm_g[l].reshape(1, -1), w_uq_p, gq_full, ind_q, freq, sign)
        k, v = _kprep(proj, pos, kv_norm_g[l].reshape(1, -1), w_kv_p, gk_nope, gk_rope, ind_k, freq, sign)
        o, wg_b, wu_b, wd_b = _attention(q.reshape(bsz, seq, -1), k.reshape(bsz, seq, -1), v.reshape(bsz, seq, -1),
                                         w_gate[l], w_up[l], w_down[l])
        yb = _s5(proj.reshape(bsz, seq, PROJ_W), perm, permt, bb, cc, a_bc, s5_d[l].reshape(1, -1).astype(F32),
                 w_glu[l].astype(BF16), w_o_s5[l].astype(BF16))
        m = _merge(o.reshape(t, -1), proj, b_gates[l].reshape(1, -1).astype(F32), yb.reshape(t, d),
                   w_o_mla[l].astype(BF16))
        x1, h2, top_idx, top_w = _outproj(m, x2, w_out[l].astype(BF16), norm2_g[l].reshape(1, d), wr_hi, wr_lo, br)

        flat_e = top_idx[:, :TOP_K].reshape(-1)
        onehot = (flat_e[:, None] == jnp.arange(N_EXPERTS, dtype=jnp.int32)[None, :]).astype(jnp.int32)
        csum = jnp.cumsum(onehot, axis=0)
        rank = jnp.take_along_axis(csum, flat_e[:, None], axis=1)[:, 0] - 1
        counts = csum[-1]
        nblk = (counts + tm_e - 1) // tm_e
        blk_end = jnp.cumsum(nblk)
        blk_start = blk_end - nblk
        slot = blk_start[flat_e] * tm_e + rank
        nused = blk_end[-1:].astype(jnp.int32)
        block_expert = jnp.minimum(
            jnp.sum((blk_end[None, :] <= jnp.arange(nb_e, dtype=jnp.int32)[:, None]).astype(jnp.int32), axis=1),
            N_EXPERTS - 1).astype(jnp.int32)
        slot_tok = jnp.zeros((n_slots,), jnp.int32).at[slot].set(jnp.arange(n_assign, dtype=jnp.int32) // TOP_K)
        pos_km = slot.reshape(t // tm_c, tm_c, TOP_K).transpose(0, 2, 1).reshape(t // tm_c, 1, TOP_K * tm_c)

        xs = _dispatch(nused, slot_tok, h2, n_slots, g=tm_e)
        ys = _experts(block_expert, nused, xs,
                      wg_b, b_gate[l].reshape(N_EXPERTS, 1, -1).astype(F32),
                      wu_b, b_up[l].reshape(N_EXPERTS, 1, -1).astype(F32),
                      wd_b, b_down[l].reshape(N_EXPERTS, 1, -1).astype(F32), tm=tm_e)
        x = _combine(pos_km, x1, top_w, ys, tm=tm_c).reshape(bsz, seq, d)
    return x
```

```python
import functools
import math

import numpy as np
import jax
import jax.numpy as jnp
from jax import lax
from jax.experimental import pallas as pl
from jax.experimental.pallas import tpu as pltpu

F32 = jnp.float32
BF16 = jnp.bfloat16

D_MODEL = 2048
CHUNK = 64
EPS = 1e-6
N_HEADS = 16
QK_NOPE_DIM = 128
QK_ROPE_DIM = 64
QK_HEAD_DIM = QK_NOPE_DIM + QK_ROPE_DIM
V_HEAD_DIM = 128
Q_LORA_RANK = 512
KV_LORA_RANK = 256
ROPE_THETA = 10000.0
S5_WIDTH = 1024
S5_GROUP = 16
S5_GROUPS = S5_WIDTH // S5_GROUP
S5_STATE = 64
N_EXPERTS = 32
TOP_K = 4
D_FF = 2048
SWIGLU_LIMIT = 7.0
SWIGLU_ALPHA = 1.702

LANES = 128
HEAD_PAD = 256
ROPE_HALF = QK_ROPE_DIM // 2
S5_CLUSTER = 8
N_CLUSTERS = S5_GROUPS // S5_CLUSTER
CLUSTER_STATES = S5_CLUSTER * S5_STATE
SUB_T = 32
PROJ_W = 6144
VMEM_LIMIT = 56 * 1024 * 1024
NEG = -1e30


def _cparams(sem, vmem=VMEM_LIMIT, **kw):
    return pltpu.CompilerParams(dimension_semantics=sem, vmem_limit_bytes=vmem, **kw)


def _inproj_kernel(x_ref, g_ref, w_ref, o_ref, h_ref):
    @pl.when(pl.program_id(1) == 0)
    def _():
        x = x_ref[...]
        ms = jnp.mean(x * x, axis=-1, keepdims=True)
        h_ref[...] = (x * lax.rsqrt(ms + EPS) * g_ref[...]).astype(BF16)

    o_ref[...] = jnp.dot(h_ref[...], w_ref[...], preferred_element_type=F32).astype(o_ref.dtype)


def _inproj(x2, g, w, *, tm=1024, tn=1536):
    t, d = x2.shape
    n = w.shape[1]
    return pl.pallas_call(
        _inproj_kernel,
        grid=(t // tm, n // tn),
        in_specs=[
            pl.BlockSpec((tm, d), lambda i, j: (i, 0)),
            pl.BlockSpec((1, d), lambda i, j: (0, 0)),
            pl.BlockSpec((d, tn), lambda i, j: (0, j)),
        ],
        out_specs=pl.BlockSpec((tm, tn), lambda i, j: (i, j)),
        out_shape=jax.ShapeDtypeStruct((t, n), BF16),
        scratch_shapes=[pltpu.VMEM((tm, d), BF16)],
        compiler_params=_cparams(("parallel", "arbitrary")),
        name="inproj",
    )(x2, g, w)


def _rope_mid(mid, cos, sin_signed):
    lane = lax.broadcasted_iota(jnp.int32, mid.shape, 1)
    rot = jnp.where(lane < ROPE_HALF, pltpu.roll(mid, LANES - ROPE_HALF, 1), pltpu.roll(mid, ROPE_HALF, 1))
    return mid * cos + rot * sin_signed


def _qprep_kernel(ql_ref, pos_ref, gn_ref, w_ref, gfull_ref, ind_ref, freq_ref, sign_ref, o_ref):
    ql = ql_ref[...].astype(F32)
    ms = jnp.mean(ql * ql, axis=-1, keepdims=True)
    qn = (ql * lax.rsqrt(ms + EPS) * gn_ref[...]).astype(BF16)
    q = jnp.dot(qn, w_ref[...], preferred_element_type=F32)
    ssq = jnp.dot((q * q).astype(BF16), ind_ref[...], preferred_element_type=F32)
    sc = lax.rsqrt(ssq * (1.0 / QK_HEAD_DIM) + EPS)
    ang = pos_ref[...] * freq_ref[...]
    cos = jnp.cos(ang)
    sin_signed = jnp.sin(ang) * sign_ref[...]
    for h in range(N_HEADS):
        s_h = sc[:, h:h + 1]
        lo = h * HEAD_PAD
        nope = q[:, lo:lo + LANES] * s_h * gfull_ref[:, lo:lo + LANES]
        mid = q[:, lo + LANES:lo + HEAD_PAD] * s_h * gfull_ref[:, lo + LANES:lo + HEAD_PAD]
        o_ref[:, lo:lo + LANES] = nope.astype(o_ref.dtype)
        o_ref[:, lo + LANES:lo + HEAD_PAD] = _rope_mid(mid, cos, sin_signed).astype(o_ref.dtype)


def _qprep(proj, pos, gn, w, gfull, ind, freq, sign, *, tm=512):
    t = proj.shape[0]
    n = N_HEADS * HEAD_PAD
    full = lambda shape: pl.BlockSpec(shape, lambda i: (0, 0))
    return pl.pallas_call(
        _qprep_kernel,
        grid=(t // tm,),
        in_specs=[
            pl.BlockSpec((tm, Q_LORA_RANK), lambda i: (i, 10)),
            pl.BlockSpec((tm, 1), lambda i: (i, 0)),
            full((1, Q_LORA_RANK)),
            full((Q_LORA_RANK, n)),
            full((1, n)),
            full((n, LANES)),
            full((1, LANES)),
            full((1, LANES)),
        ],
        out_specs=pl.BlockSpec((tm, n), lambda i: (i, 0)),
        out_shape=jax.ShapeDtypeStruct((t, n), BF16),
        compiler_params=_cparams(("parallel",)),
        name="qprep",
    )(proj, pos, gn, w, gfull, ind, freq, sign)


def _kprep_kernel(kvl_ref, kr_ref, pos_ref, gn_ref, w_ref, gnope_ref, grope_ref, ind_ref, freq_ref, sign_ref,
                  k_ref, v_ref):
    kvl = kvl_ref[...].astype(F32)
    ms = jnp.mean(kvl * kvl, axis=-1, keepdims=True)
    kn = (kvl * lax.rsqrt(ms + EPS) * gn_ref[...]).astype(BF16)
    kv = jnp.dot(kn, w_ref[...], preferred_element_type=F32)
    nd = N_HEADS * QK_NOPE_DIM
    knope = kv[:, :nd]
    v_ref[...] = kv[:, nd:].astype(v_ref.dtype)
    kr = kr_ref[...].astype(F32)
    ssq = jnp.dot((knope * knope).astype(BF16), ind_ref[...], preferred_element_type=F32)
    ssq = ssq + jnp.sum(kr * kr, axis=-1, keepdims=True)
    sc = lax.rsqrt(ssq * (1.0 / QK_HEAD_DIM) + EPS)
    ang = pos_ref[...] * freq_ref[...]
    kr_rot = _rope_mid(kr * grope_ref[...], jnp.cos(ang), jnp.sin(ang) * sign_ref[...])
    for h in range(N_HEADS):
        s_h = sc[:, h:h + 1]
        lo = h * HEAD_PAD
        nope = knope[:, h * LANES:(h + 1) * LANES] * s_h * gnope_ref[...]
        k_ref[:, lo:lo + LANES] = nope.astype(k_ref.dtype)
        k_ref[:, lo + LANES:lo + HEAD_PAD] = (kr_rot * s_h).astype(k_ref.dtype)


def _kprep(proj, pos, gn, w, gnope, grope, ind, freq, sign, *, tm=512):
    t = proj.shape[0]
    n = N_HEADS * HEAD_PAD
    nd = N_HEADS * QK_NOPE_DIM
    full = lambda shape: pl.BlockSpec(shape, lambda i: (0, 0))
    return pl.pallas_call(
        _kprep_kernel,
        grid=(t // tm,),
        in_specs=[
            pl.BlockSpec((tm, KV_LORA_RANK), lambda i: (i, 22)),
            pl.BlockSpec((tm, LANES), lambda i: (i, 46)),
            pl.BlockSpec((tm, 1), lambda i: (i, 0)),
            full((1, KV_LORA_RANK)),
            full((KV_LORA_RANK, 2 * nd)),
            full((1, LANES)),
            full((1, LANES)),
            full((nd, LANES)),
            full((1, LANES)),
            full((1, LANES)),
        ],
        out_specs=[pl.BlockSpec((tm, n), lambda i: (i, 0)), pl.BlockSpec((tm, nd), lambda i: (i, 0))],
        out_shape=[jax.ShapeDtypeStruct((t, n), BF16), jax.ShapeDtypeStruct((t, nd), BF16)],
        compiler_params=_cparams(("parallel",)),
        name="kprep",
    )(proj, proj, pos, gn, w, gnope, grope, ind, freq, sign)


def _attn_kernel(q_ref, k_ref, v_ref, wg_ref, wu_ref, wd_ref, o_ref, wgo_ref, wuo_ref, wdo_ref,
                 m_ref, l_ref, acc_ref, *, tq, hg):
    wgo_ref[...] = wg_ref[...].astype(wgo_ref.dtype)
    wuo_ref[...] = wu_ref[...].astype(wuo_ref.dtype)
    wdo_ref[...] = wd_ref[...].astype(wdo_ref.dtype)

    qi = pl.program_id(2)
    m_ref[...] = jnp.full(m_ref.shape, NEG, F32)
    l_ref[...] = jnp.zeros(l_ref.shape, F32)
    acc_ref[...] = jnp.zeros(acc_ref.shape, F32)

    def block(j, masked):
        r0 = pl.multiple_of(j * tq, tq)
        for h in range(hg):
            q = q_ref[:, h * HEAD_PAD:(h + 1) * HEAD_PAD]
            kb = k_ref[pl.ds(r0, tq), h * HEAD_PAD:(h + 1) * HEAD_PAD]
            vb = v_ref[pl.ds(r0, tq), h * V_HEAD_DIM:(h + 1) * V_HEAD_DIM]
            st = lax.dot_general(kb, q, (((1,), (1,)), ((), ())), preferred_element_type=F32)
            if masked:
                key = lax.broadcasted_iota(jnp.int32, st.shape, 0) // CHUNK
                qry = lax.broadcasted_iota(jnp.int32, st.shape, 1) // CHUNK
                st = jnp.where(key <= qry, st, NEG)
            m_old = m_ref[h]
            m_new = jnp.maximum(m_old, jnp.max(st, axis=0, keepdims=True))
            alpha = jnp.exp2(m_old - m_new)
            p = jnp.exp2(st - m_new)
            l_ref[h] = alpha * l_ref[h] + jnp.sum(p, axis=0, keepdims=True)
            pv = lax.dot_general(vb, p.astype(BF16), (((0,), (0,)), ((), ())), preferred_element_type=F32)
            acc_ref[h] = alpha * acc_ref[h] + pv
            m_ref[h] = m_new

    def body(j, c):
        block(j, False)
        return c

    lax.fori_loop(0, qi, body, 0)
    block(qi, True)
    for h in range(hg):
        o = acc_ref[h] / l_ref[h]
        o_ref[:, h * V_HEAD_DIM:(h + 1) * V_HEAD_DIM] = o.T.astype(o_ref.dtype)


def _attention(q3, k3, v3, wg, wu, wd, *, tq=512, hg=2):
    b, s, _ = q3.shape
    tq = min(tq, s)
    ng, nq = N_HEADS // hg, s // tq
    steps = b * ng * nq
    w2 = [w.reshape(-1, w.shape[-1]) for w in (wg, wu, wd)]
    slab = [w.shape[0] // steps for w in w2]
    step = lambda bi, h, i: ((bi * ng + h) * nq + i, 0)
    wspec = [pl.BlockSpec((r, w.shape[1]), step) for r, w in zip(slab, w2)]
    outs = pl.pallas_call(
        functools.partial(_attn_kernel, tq=tq, hg=hg),
        grid=(b, ng, nq),
        in_specs=[
            pl.BlockSpec((None, tq, hg * HEAD_PAD), lambda bi, h, i: (bi, i, h)),
            pl.BlockSpec((None, s, hg * HEAD_PAD), lambda bi, h, i: (bi, 0, h)),
            pl.BlockSpec((None, s, hg * V_HEAD_DIM), lambda bi, h, i: (bi, 0, h)),
        ] + wspec,
        out_specs=[pl.BlockSpec((None, tq, hg * V_HEAD_DIM), lambda bi, h, i: (bi, i, h))] + wspec,
        out_shape=[jax.ShapeDtypeStruct((b, s, N_HEADS * V_HEAD_DIM), BF16)]
        + [jax.ShapeDtypeStruct(w.shape, BF16) for w in w2],
        scratch_shapes=[pltpu.VMEM((hg, 1, tq), F32), pltpu.VMEM((hg, 1, tq), F32),
                        pltpu.VMEM((hg, V_HEAD_DIM, tq), F32)],
        compiler_params=_cparams(("parallel", "parallel", "arbitrary")),
        name="attention",
    )(q3, k3, v3, *w2)
    return (outs[0],) + tuple(o.reshape(w.shape) for o, w in zip(outs[1:], (wg, wu, wd)))


def _s5_kernel(u_ref, perm_ref, permt_ref, bb_ref, cc_ref, a_ref, d_ref, wglu_ref, wo_ref, o_ref,
               ut_ref, bu_ref, y_ref, carry_ref, *, tc):
    nb = u_ref.shape[0]
    rows_sub = nb * SUB_T
    nsub = tc // SUB_T

    @pl.when(pl.program_id(0) == 0)
    def _():
        carry_ref[...] = jnp.zeros(carry_ref.shape, F32)

    for j in range(nsub):
        ub = u_ref[:, j * SUB_T:(j + 1) * SUB_T, :].reshape(rows_sub, S5_WIDTH)
        ut_ref[j * rows_sub:(j + 1) * rows_sub, :] = jnp.dot(
            perm_ref[...], ub, preferred_element_type=F32).astype(BF16)

    for c in range(N_CLUSTERS):
        bu_ref[...] = jnp.dot(ut_ref[:, c * LANES:(c + 1) * LANES], bb_ref[c], preferred_element_type=F32)
        ar = a_ref[c, :, :CLUSTER_STATES]
        ai = a_ref[c, :, CLUSTER_STATES:]

        def step(t, carry):
            xr, xi = carry
            r0 = pl.multiple_of(t * nb, nb)
            br = bu_ref[pl.ds(r0, nb), :CLUSTER_STATES]
            bi = bu_ref[pl.ds(r0, nb), CLUSTER_STATES:]
            nxr = ar * xr - ai * xi + br
            nxi = ar * xi + ai * xr + bi
            bu_ref[pl.ds(r0, nb), :CLUSTER_STATES] = nxr
            bu_ref[pl.ds(r0, nb), CLUSTER_STATES:] = nxi
            return nxr, nxi

        xr, xi = lax.fori_loop(0, tc, step, (carry_ref[c, :, :CLUSTER_STATES], carry_ref[c, :, CLUSTER_STATES:]),
                               unroll=8)
        carry_ref[c, :, :CLUSTER_STATES] = xr
        carry_ref[c, :, CLUSTER_STATES:] = xi
        y_ref[:, c * LANES:(c + 1) * LANES] = jnp.dot(bu_ref[...].astype(BF16), cc_ref[c], preferred_element_type=F32)

    y = y_ref[...] + d_ref[...] * ut_ref[...].astype(F32)
    g = jax.nn.gelu(y)
    z = g * jax.nn.sigmoid(jnp.dot(g.astype(BF16), wglu_ref[...], preferred_element_type=F32))
    yb = jnp.dot(z.astype(BF16), wo_ref[...], preferred_element_type=F32).astype(BF16)
    for j in range(nsub):
        blk = jnp.dot(permt_ref[...], yb[j * rows_sub:(j + 1) * rows_sub, :], preferred_element_type=F32)
        o_ref[:, j * SUB_T:(j + 1) * SUB_T, :] = blk.astype(o_ref.dtype).reshape(nb, SUB_T, o_ref.shape[2])


def _s5(proj3, perm, permt, bb, cc, a_bc, d, wglu, wo, *, tc=64):
    b, s, _ = proj3.shape
    rows = b * tc
    dm = wo.shape[1]
    c2 = lambda shape: pl.BlockSpec(shape, lambda i: (0, 0))
    c3 = lambda shape: pl.BlockSpec(shape, lambda i: (0, 0, 0))
    return pl.pallas_call(
        functools.partial(_s5_kernel, tc=tc),
        grid=(s // tc,),
        in_specs=[
            pl.BlockSpec((b, tc, S5_WIDTH), lambda i: (0, i, 4)),
            c2(perm.shape), c2(permt.shape), c3(bb.shape), c3(cc.shape), c3(a_bc.shape), c2(d.shape),
            c2(wglu.shape), c2(wo.shape),
        ],
        out_specs=pl.BlockSpec((b, tc, dm), lambda i: (0, i, 0)),
        out_shape=jax.ShapeDtypeStruct((b, s, dm), BF16),
        scratch_shapes=[
            pltpu.VMEM((rows, S5_WIDTH), BF16),
            pltpu.VMEM((rows, 2 * CLUSTER_STATES), F32),
            pltpu.VMEM((rows, S5_WIDTH), F32),
            pltpu.VMEM((N_CLUSTERS, b, 2 * CLUSTER_STATES), F32),
        ],
        compiler_params=_cparams(("arbitrary",)),
        name="s5",
    )(proj3, perm, permt, bb, cc, a_bc, d, wglu, wo)


def _merge_kernel(o_ref, gl_ref, bg_ref, yb_ref, w_ref, m_ref):
    d = o_ref.shape[1]
    ya = jnp.dot(o_ref[...], w_ref[...], preferred_element_type=F32)
    ga = jax.nn.sigmoid(gl_ref[:, :d].astype(F32) + bg_ref[:, :d])
    gb = jax.nn.sigmoid(gl_ref[:, d:].astype(F32) + bg_ref[:, d:])
    m_ref[...] = (ga * ya + gb * yb_ref[...].astype(F32)).astype(m_ref.dtype)


def _merge(o2, proj, bg, yb2, w, *, tm=512):
    t, d = o2.shape
    return pl.pallas_call(
        _merge_kernel,
        grid=(t // tm,),
        in_specs=[
            pl.BlockSpec((tm, d), lambda i: (i, 0)),
            pl.BlockSpec((tm, 2 * d), lambda i: (i, 0)),
            pl.BlockSpec((1, 2 * d), lambda i: (0, 0)),
            pl.BlockSpec((tm, d), lambda i: (i, 0)),
            pl.BlockSpec((d, d), lambda i: (0, 0)),
        ],
        out_specs=pl.BlockSpec((tm, d), lambda i: (i, 0)),
        out_shape=jax.ShapeDtypeStruct((t, d), BF16),
        compiler_params=_cparams(("parallel",)),
        name="merge",
    )(o2, proj, bg, yb2, w)


def _to_row_tiles(x):
    return x.reshape(x.shape[0], x.shape[1] // LANES, LANES)


def _from_row_tiles(x):
    return x.reshape(x.shape[0], x.shape[1] * LANES)


def _outproj_kernel(m_ref, x_ref, w_ref, g_ref, wrh_ref, wrl_ref, br_ref, x1_ref, h2_ref, idx_ref, wt_ref):
    x1 = x_ref[...] + jnp.dot(m_ref[...], w_ref[...], preferred_element_type=F32)
    x1_ref[...] = x1
    ms = jnp.mean(x1 * x1, axis=-1, keepdims=True)
    h2 = x1 * lax.rsqrt(ms + EPS) * g_ref[...]
    h_hi = h2.astype(BF16)
    h2_ref[...] = _to_row_tiles(h_hi)
    h_lo = (h2 - h_hi.astype(F32)).astype(BF16)
    logits = (jnp.dot(h_hi, wrh_ref[...], preferred_element_type=F32)
              + jnp.dot(h_lo, wrh_ref[...], preferred_element_type=F32)
              + jnp.dot(h_hi, wrl_ref[...], preferred_element_type=F32)) + br_ref[...]
    lane = lax.broadcasted_iota(jnp.int32, logits.shape, 1)
    work = jnp.where(lane < N_EXPERTS, logits, -jnp.inf)
    idx_out = jnp.zeros(logits.shape, jnp.int32)
    val_out = jnp.zeros(logits.shape, F32)
    v0 = None
    denom = None
    for k in range(TOP_K):
        mx = jnp.max(work, axis=-1, keepdims=True)
        sel = jnp.min(jnp.where(work == mx, lane, LANES), axis=-1, keepdims=True)
        if k == 0:
            v0 = mx
        e = jnp.exp(mx - v0)
        denom = e if k == 0 else denom + e
        idx_out = jnp.where(lane == k, sel, idx_out)
        val_out = jnp.where(lane == k, e, val_out)
        work = jnp.where(lane == sel, -jnp.inf, work)
    idx_ref[...] = idx_out
    wt_ref[...] = val_out / denom


def _outproj(m2, x2, w, g, wrh, wrl, br, *, tm=512):
    t, d = x2.shape
    c2 = lambda shape: pl.BlockSpec(shape, lambda i: (0, 0))
    row = lambda width: pl.BlockSpec((tm, width), lambda i: (i, 0))
    return pl.pallas_call(
        _outproj_kernel,
        grid=(t // tm,),
        in_specs=[row(d), row(d), c2((d, d)), c2((1, d)), c2((d, LANES)), c2((d, LANES)), c2((1, LANES))],
        out_specs=[row(d), pl.BlockSpec((tm, d // LANES, LANES), lambda i: (i, 0, 0)), row(LANES), row(LANES)],
        out_shape=[
            jax.ShapeDtypeStruct((t, d), F32),
            jax.ShapeDtypeStruct((t, d // LANES, LANES), BF16),
            jax.ShapeDtypeStruct((t, LANES), jnp.int32),
            jax.ShapeDtypeStruct((t, LANES), F32),
        ],
        compiler_params=_cparams(("parallel",)),
        name="outproj",
    )(m2, x2, w, g, wrh, wrl, br)


N_DMA_PRIORITIES = 2


def _start_row_gather(idx_ref, n_rows, src_hbm, dst_ref, sem):
    def issue(r2, c):
        for u in range(N_DMA_PRIORITIES):
            r = r2 * N_DMA_PRIORITIES + u
            pltpu.make_async_copy(src_hbm.at[pl.ds(idx_ref[0, 0, r], 1)], dst_ref.at[pl.ds(r, 1)], sem).start(priority=u)
        return c

    lax.fori_loop(0, n_rows // N_DMA_PRIORITIES, issue, 0, unroll=4)


def _wait_rows(dst_ref, sem):
    pltpu.make_async_copy(dst_ref, dst_ref, sem).wait()


def _expert_kernel(be_ref, nu_ref, tokc_ref, tokn_ref, h_hbm, wg_ref, bg_ref, wu_ref, bu_ref, wd_ref, bd_ref, o_ref,
                   xbuf_ref, xb_ref, acc_ref, sem, *, tm):
    b = pl.program_id(0)
    f = pl.program_id(1)
    nxt = b + 1

    @pl.when(jnp.logical_and(f == 0, jnp.logical_and(b == 0, nu_ref[0] > 0)))
    def _():
        _start_row_gather(tokc_ref, tm, h_hbm, xbuf_ref.at[0], sem.at[0])

    @pl.when(jnp.logical_and(f == 0, jnp.logical_and(nxt < pl.num_programs(0), nxt < nu_ref[0])))
    def _():
        _start_row_gather(tokn_ref, tm, h_hbm, xbuf_ref.at[nxt % 2], sem.at[nxt % 2])

    @pl.when(b < nu_ref[0])
    def _():
        @pl.when(f == 0)
        def _():
            _wait_rows(xbuf_ref.at[b % 2], sem.at[b % 2])
            xb_ref[...] = _from_row_tiles(xbuf_ref[b % 2])
            acc_ref[...] = jnp.zeros(acc_ref.shape, F32)

        xb = xb_ref[...]
        gate = jnp.dot(xb, wg_ref[0], preferred_element_type=F32) + bg_ref[0]
        up = jnp.dot(xb, wu_ref[0], preferred_element_type=F32) + bu_ref[0]
        gate = jnp.minimum(gate, SWIGLU_LIMIT)
        up = jnp.clip(up, -SWIGLU_LIMIT, SWIGLU_LIMIT)
        glu = gate * jax.nn.sigmoid(SWIGLU_ALPHA * gate)
        act = ((up + 1.0) * glu).astype(BF16)
        acc_ref[...] += jnp.dot(act, wd_ref[0], preferred_element_type=F32)

        @pl.when(f == pl.num_programs(1) - 1)
        def _():
            o_ref[...] = _to_row_tiles((acc_ref[...] + bd_ref[0]).astype(o_ref.dtype))

    @pl.when(jnp.logical_and(b >= nu_ref[0], f == 0))
    def _():
        o_ref[...] = jnp.zeros(o_ref.shape, o_ref.dtype)


def _experts(block_expert, nused, slot_tok, h2, wg, bg, wu, bu, wd, bd, *, tm, tf=1024):
    n_slots = slot_tok.shape[0]
    tile = h2.shape[1:]
    d = tile[0] * tile[1]
    nb = n_slots // tm
    dff = wg.shape[2]
    nf = dff // tf
    tok3 = slot_tok.reshape(nb, 1, tm)

    def blk(b, nu):
        return jnp.minimum(b, nu[0] - 1)

    def fidx(b, f, nu):
        return jnp.where(b < nu[0], f, nf - 1)

    return pl.pallas_call(
        functools.partial(_expert_kernel, tm=tm),
        grid_spec=pltpu.PrefetchScalarGridSpec(
            num_scalar_prefetch=2,
            grid=(nb, nf),
            in_specs=[
                pl.BlockSpec((1, 1, tm), lambda b, f, be, nu: (b, 0, 0), memory_space=pltpu.SMEM),
                pl.BlockSpec((1, 1, tm), lambda b, f, be, nu: (jnp.minimum(b + 1, nb - 1), 0, 0),
                             memory_space=pltpu.SMEM),
                pl.BlockSpec(memory_space=pl.ANY),
                pl.BlockSpec((1, d, tf), lambda b, f, be, nu: (be[blk(b, nu)], 0, fidx(b, f, nu))),
                pl.BlockSpec((1, 1, tf), lambda b, f, be, nu: (be[blk(b, nu)], 0, fidx(b, f, nu))),
                pl.BlockSpec((1, d, tf), lambda b, f, be, nu: (be[blk(b, nu)], 0, fidx(b, f, nu))),
                pl.BlockSpec((1, 1, tf), lambda b, f, be, nu: (be[blk(b, nu)], 0, fidx(b, f, nu))),
                pl.BlockSpec((1, tf, d), lambda b, f, be, nu: (be[blk(b, nu)], fidx(b, f, nu), 0)),
                pl.BlockSpec((1, 1, d), lambda b, f, be, nu: (be[blk(b, nu)], 0, 0)),
            ],
            out_specs=pl.BlockSpec((tm,) + tile, lambda b, f, be, nu: (b, 0, 0)),
            scratch_shapes=[pltpu.VMEM((2, tm) + tile, h2.dtype), pltpu.VMEM((tm, d), BF16), pltpu.VMEM((tm, d), F32),
                            pltpu.SemaphoreType.DMA((2,))],
        ),
        out_shape=jax.ShapeDtypeStruct((n_slots,) + tile, BF16),
        compiler_params=_cparams(("arbitrary", "arbitrary")),
        name="experts",
    )(block_expert, nused, tok3, tok3, h2, wg, bg, wu, bu, wd, bd)


def _combine_kernel(posc_ref, posn_ref, x1_ref, wt_ref, ys_hbm, o_ref, buf_ref, sem, *, tm):
    i = pl.program_id(0)
    nxt = i + 1

    @pl.when(i == 0)
    def _():
        _start_row_gather(posc_ref, TOP_K * tm, ys_hbm, buf_ref.at[0], sem.at[0])

    @pl.when(nxt < pl.num_programs(0))
    def _():
        _start_row_gather(posn_ref, TOP_K * tm, ys_hbm, buf_ref.at[nxt % 2], sem.at[nxt % 2])

    slot = i % 2
    _wait_rows(buf_ref.at[slot], sem.at[slot])
    acc = x1_ref[...]
    for k in range(TOP_K):
        rows = buf_ref[slot, k * tm:(k + 1) * tm]
        acc = acc + wt_ref[:, k:k + 1] * _from_row_tiles(rows).astype(F32)
    o_ref[...] = acc


def _combine(pos_km, x1, wt, ys, *, tm=128):
    t, d = x1.shape
    n = t // tm
    return pl.pallas_call(
        functools.partial(_combine_kernel, tm=tm),
        grid=(n,),
        in_specs=[
            pl.BlockSpec((1, 1, TOP_K * tm), lambda i: (i, 0, 0), memory_space=pltpu.SMEM),
            pl.BlockSpec((1, 1, TOP_K * tm), lambda i: (jnp.minimum(i + 1, n - 1), 0, 0), memory_space=pltpu.SMEM),
            pl.BlockSpec((tm, d), lambda i: (i, 0)),
            pl.BlockSpec((tm, LANES), lambda i: (i, 0)),
            pl.BlockSpec(memory_space=pl.ANY),
        ],
        out_specs=pl.BlockSpec((tm, d), lambda i: (i, 0)),
        out_shape=jax.ShapeDtypeStruct((t, d), F32),
        scratch_shapes=[pltpu.VMEM((2, TOP_K * tm) + ys.shape[1:], ys.dtype), pltpu.SemaphoreType.DMA((2,))],
        compiler_params=_cparams(("arbitrary",)),
        name="combine",
    )(pos_km, pos_km, x1, wt, ys)


def _pad_heads(w, per_head):
    lead = w.shape[:-1]
    w = w.reshape(lead + (N_HEADS, per_head))
    w = jnp.pad(w, [(0, 0)] * len(lead) + [(0, 0), (0, HEAD_PAD - per_head)])
    return w.reshape(lead + (N_HEADS * HEAD_PAD,))


def _s5_discretise(lam_re, lam_im, log_dt, b_re, b_im):
    dt = jnp.exp(log_dt.astype(F32))[:, None]
    lr, li = lam_re.astype(F32), lam_im.astype(F32)
    mag = jnp.exp(lr * dt)
    ar, ai = mag * jnp.cos(li * dt), mag * jnp.sin(li * dt)
    den = lr * lr + li * li
    zr = ((ar - 1.0) * lr + ai * li) / den
    zi = (ai * lr - (ar - 1.0) * li) / den
    br, bi = b_re.astype(F32), b_im.astype(F32)
    bbr = zr[..., None] * br - zi[..., None] * bi
    bbi = zr[..., None] * bi + zi[..., None] * br
    return ar, ai, bbr, bbi


def _s5_pack(ar, ai, bbr, bbi, c_re, c_im, nb):
    eye = jnp.eye(S5_CLUSTER, dtype=F32)

    def pack_b(m):
        m4 = m.reshape(N_CLUSTERS, S5_CLUSTER, S5_STATE, S5_GROUP)
        return jnp.einsum('xgpc,gh->xgchp', m4, eye).reshape(N_CLUSTERS, S5_CLUSTER * S5_GROUP, CLUSTER_STATES)

    def pack_c(m):
        m4 = m.reshape(N_CLUSTERS, S5_CLUSTER, S5_GROUP, S5_STATE)
        return jnp.einsum('xgcp,gh->xgphc', m4, eye).reshape(N_CLUSTERS, CLUSTER_STATES, S5_CLUSTER * S5_GROUP)

    bb = jnp.concatenate([pack_b(bbr), pack_b(bbi)], axis=2).astype(BF16)
    cc = jnp.concatenate([pack_c(c_re.astype(F32)), -pack_c(c_im.astype(F32))], axis=1).astype(BF16)
    a = jnp.concatenate([ar.reshape(N_CLUSTERS, CLUSTER_STATES), ai.reshape(N_CLUSTERS, CLUSTER_STATES)], axis=1)
    a_bc = jnp.broadcast_to(a[:, None, :], (N_CLUSTERS, nb, 2 * CLUSTER_STATES))
    return bb, cc, a_bc


def _perm_matrix(nb):
    n = nb * SUB_T
    p = np.zeros((n, n), np.float32)
    for b in range(nb):
        for t in range(SUB_T):
            p[t * nb + b, b * SUB_T + t] = 1.0
    return p


def kernel(x, positions, norm1_g, w_in, b_gates, q_norm_g, w_uq, kv_norm_g, w_ukv, qk_norm_q_g, qk_norm_k_g, w_o_mla, s5_lambda_re, s5_lambda_im, s5_log_dt, s5_b_re, s5_b_im, s5_c_re, s5_c_im, s5_d, w_glu, w_o_s5, w_out, norm2_g, w_router, b_router, w_gate, b_gate, w_up, b_up, w_down, b_down):
    bsz, seq, d = x.shape
    t = bsz * seq
    depth = norm1_g.shape[0]
    o1 = Q_LORA_RANK
    o2 = o1 + KV_LORA_RANK
    o3 = o2 + QK_ROPE_DIM
    o4 = o3 + S5_WIDTH
    half = ROPE_HALF
    inv_freq = ROPE_THETA ** (-jnp.arange(half, dtype=F32) / half)
    lane = np.arange(LANES)
    freq = jnp.where(lane < QK_ROPE_DIM, jnp.tile(inv_freq, LANES // half), 0.0).reshape(1, LANES).astype(F32)
    sign = jnp.asarray(np.where(lane % QK_ROPE_DIM < half, -1.0, 1.0).reshape(1, LANES), F32)
    pos = positions.reshape(t, 1).astype(F32)
    sm_scale = math.log2(math.e) / math.sqrt(QK_HEAD_DIM)
    ind_q = jnp.asarray(np.equal.outer(np.arange(N_HEADS * HEAD_PAD) // HEAD_PAD, lane), BF16)
    ind_k = jnp.asarray(np.equal.outer(np.arange(N_HEADS * QK_NOPE_DIM) // QK_NOPE_DIM, lane), BF16)
    perm_np = _perm_matrix(bsz)
    perm = jnp.asarray(perm_np, BF16)
    permt = jnp.asarray(perm_np.T, BF16)
    tm_e = 512
    n_assign = t * TOP_K
    nb_e = n_assign // tm_e + N_EXPERTS
    n_slots = nb_e * tm_e
    tm_c = 128

    for l in range(depth):
        wi = w_in[l]
        w_in_p = jnp.concatenate(
            [wi[:, o4:], wi[:, o3:o4], wi[:, :o1], wi[:, o1:o2], wi[:, o2:o3],
             jnp.zeros((d, PROJ_W - wi.shape[1]), wi.dtype)], axis=1).astype(BF16)
        w_uq_p = _pad_heads(w_uq[l], QK_HEAD_DIM).astype(BF16)
        gq_full = _pad_heads(jnp.tile(qk_norm_q_g[l].astype(F32), N_HEADS) * sm_scale, QK_HEAD_DIM).reshape(1, -1)
        wkv = w_ukv[l].reshape(KV_LORA_RANK, N_HEADS, QK_NOPE_DIM + V_HEAD_DIM)
        w_kv_p = jnp.concatenate(
            [wkv[:, :, :QK_NOPE_DIM].reshape(KV_LORA_RANK, -1), wkv[:, :, QK_NOPE_DIM:].reshape(KV_LORA_RANK, -1)],
            axis=1).astype(BF16)
        gk = qk_norm_k_g[l].astype(F32)
        gk_nope = gk[:QK_NOPE_DIM].reshape(1, LANES)
        gk_rope = jnp.pad(gk[QK_NOPE_DIM:], (0, LANES - QK_ROPE_DIM)).reshape(1, LANES)
        ar, ai, bbr, bbi = _s5_discretise(s5_lambda_re[l], s5_lambda_im[l], s5_log_dt[l], s5_b_re[l], s5_b_im[l])
        bb, cc, a_bc = _s5_pack(ar, ai, bbr, bbi, s5_c_re[l], s5_c_im[l], bsz)
        wr = jnp.pad(w_router[l].astype(F32), ((0, 0), (0, LANES - N_EXPERTS)))
        wr_hi = wr.astype(BF16)
        wr_lo = (wr - wr_hi.astype(F32)).astype(BF16)
        br = jnp.pad(b_router[l].astype(F32), (0, LANES - N_EXPERTS)).reshape(1, LANES)

        x2 = x.reshape(t, d)
        proj = _inproj(x2, norm1_g[l].reshape(1, d), w_in_p)
        q = _qprep(proj, pos, q_norm_g[l].reshape(1, -1), w_uq_p, gq_full, ind_q, freq, sign)
        k, v = _kprep(proj, pos, kv_norm_g[l].reshape(1, -1), w_kv_p, gk_nope, gk_rope, ind_k, freq, sign)
        o, wg_b, wu_b, wd_b = _attention(q.reshape(bsz, seq, -1), k.reshape(bsz, seq, -1), v.reshape(bsz, seq, -1),
                                         w_gate[l], w_up[l], w_down[l])
        yb = _s5(proj.reshape(bsz, seq, PROJ_W), perm, permt, bb, cc, a_bc, s5_d[l].reshape(1, -1).astype(F32),
                 w_glu[l].astype(BF16), w_o_s5[l].astype(BF16))
        m = _merge(o.reshape(t, -1), proj, b_gates[l].reshape(1, -1).astype(F32), yb.reshape(t, d),
                   w_o_mla[l].astype(BF16))
        x1, h2, top_idx, top_w = _outproj(m, x2, w_out[l].astype(BF16), norm2_g[l].reshape(1, d), wr_hi, wr_lo, br)

        flat_e = top_idx[:, :TOP_K].reshape(-1)
        onehot = (flat_e[:, None] == jnp.arange(N_EXPERTS, dtype=jnp.int32)[None, :]).astype(jnp.int32)
        csum = jnp.cumsum(onehot, axis=0)
        rank = jnp.take_along_axis(csum, flat_e[:, None], axis=1)[:, 0] - 1
        counts = csum[-1]
        nblk = (counts + tm_e - 1) // tm_e
        blk_end = jnp.cumsum(nblk)
        blk_start = blk_end - nblk
        slot = blk_start[flat_e] * tm_e + rank
        nused = blk_end[-1:].astype(jnp.int32)
        block_expert = jnp.minimum(
            jnp.sum((blk_end[None, :] <= jnp.arange(nb_e, dtype=jnp.int32)[:, None]).astype(jnp.int32), axis=1),
            N_EXPERTS - 1).astype(jnp.int32)
        slot_tok = jnp.zeros((n_slots,), jnp.int32).at[slot].set(jnp.arange(n_assign, dtype=jnp.int32) // TOP_K)
        pos_km = slot.reshape(t // tm_c, tm_c, TOP_K).transpose(0, 2, 1).reshape(t // tm_c, 1, TOP_K * tm_c)

        ys = _experts(block_expert, nused, slot_tok, h2,
                      wg_b, b_gate[l].reshape(N_EXPERTS, 1, -1).astype(F32),
                      wu_b, b_up[l].reshape(N_EXPERTS, 1, -1).astype(F32),
                      wd_b, b_down[l].reshape(N_EXPERTS, 1, -1).astype(F32), tm=tm_e)
        x = _combine(pos_km, x1, top_w, ys, tm=tm_c).reshape(bsz, seq, d)
    return x
```

```python
import functools
import math

import numpy as np
import jax
import jax.numpy as jnp
from jax import lax
from jax.experimental import pallas as pl
from jax.experimental.pallas import tpu as pltpu

F32 = jnp.float32
BF16 = jnp.bfloat16

D_MODEL = 2048
CHUNK = 64
EPS = 1e-6
N_HEADS = 16
QK_NOPE_DIM = 128
QK_ROPE_DIM = 64
QK_HEAD_DIM = QK_NOPE_DIM + QK_ROPE_DIM
V_HEAD_DIM = 128
Q_LORA_RANK = 512
KV_LORA_RANK = 256
ROPE_THETA = 10000.0
S5_WIDTH = 1024
S5_GROUP = 16
S5_GROUPS = S5_WIDTH // S5_GROUP
S5_STATE = 64
N_EXPERTS = 32
TOP_K = 4
D_FF = 2048
SWIGLU_LIMIT = 7.0
SWIGLU_ALPHA = 1.702

LANES = 128
HEAD_PAD = 256
ROPE_HALF = QK_ROPE_DIM // 2
S5_CLUSTER = 8
N_CLUSTERS = S5_GROUPS // S5_CLUSTER
CLUSTER_STATES = S5_CLUSTER * S5_STATE
SUB_T = 32
PROJ_W = 6144
VMEM_LIMIT = 56 * 1024 * 1024
NEG = -1e30
DENOM_ROWS = 8


def _cparams(sem, vmem=VMEM_LIMIT, **kw):
    return pltpu.CompilerParams(dimension_semantics=sem, vmem_limit_bytes=vmem, **kw)


def _inproj_kernel(x_ref, g_ref, w_ref, o_ref, h_ref):
    @pl.when(pl.program_id(1) == 0)
    def _():
        x = x_ref[...]
        ms = jnp.mean(x * x, axis=-1, keepdims=True)
        h_ref[...] = (x * lax.rsqrt(ms + EPS) * g_ref[...]).astype(BF16)

    o_ref[...] = jnp.dot(h_ref[...], w_ref[...], preferred_element_type=F32).astype(o_ref.dtype)


def _inproj(x2, g, w, *, tm=1024, tn=1536):
    t, d = x2.shape
    n = w.shape[1]
    return pl.pallas_call(
        _inproj_kernel,
        grid=(t // tm, n // tn),
        in_specs=[
            pl.BlockSpec((tm, d), lambda i, j: (i, 0)),
            pl.BlockSpec((1, d), lambda i, j: (0, 0)),
            pl.BlockSpec((d, tn), lambda i, j: (0, j)),
        ],
        out_specs=pl.BlockSpec((tm, tn), lambda i, j: (i, j)),
        out_shape=jax.ShapeDtypeStruct((t, n), BF16),
        scratch_shapes=[pltpu.VMEM((tm, d), BF16)],
        compiler_params=_cparams(("parallel", "arbitrary")),
        name="inproj",
    )(x2, g, w)


def _rope_mid(mid, cos, sin_signed):
    lane = lax.broadcasted_iota(jnp.int32, mid.shape, 1)
    rot = jnp.where(lane < ROPE_HALF, pltpu.roll(mid, LANES - ROPE_HALF, 1), pltpu.roll(mid, ROPE_HALF, 1))
    return mid * cos + rot * sin_signed


def _qprep_kernel(ql_ref, pos_ref, gn_ref, w_ref, gfull_ref, ind_ref, freq_ref, sign_ref, o_ref):
    ql = ql_ref[...].astype(F32)
    ms = jnp.mean(ql * ql, axis=-1, keepdims=True)
    qn = (ql * lax.rsqrt(ms + EPS) * gn_ref[...]).astype(BF16)
    q = jnp.dot(qn, w_ref[...], preferred_element_type=F32)
    ssq = jnp.dot((q * q).astype(BF16), ind_ref[...], preferred_element_type=F32)
    sc = lax.rsqrt(ssq * (1.0 / QK_HEAD_DIM) + EPS)
    ang = pos_ref[...] * freq_ref[...]
    cos = jnp.cos(ang)
    sin_signed = jnp.sin(ang) * sign_ref[...]
    for h in range(N_HEADS):
        s_h = sc[:, h:h + 1]
        lo = h * HEAD_PAD
        nope = q[:, lo:lo + LANES] * s_h * gfull_ref[:, lo:lo + LANES]
        mid = q[:, lo + LANES:lo + HEAD_PAD] * s_h * gfull_ref[:, lo + LANES:lo + HEAD_PAD]
        o_ref[:, lo:lo + LANES] = nope.astype(o_ref.dtype)
        o_ref[:, lo + LANES:lo + HEAD_PAD] = _rope_mid(mid, cos, sin_signed).astype(o_ref.dtype)


def _qprep(proj, pos, gn, w, gfull, ind, freq, sign, *, tm=512):
    t = proj.shape[0]
    n = N_HEADS * HEAD_PAD
    full = lambda shape: pl.BlockSpec(shape, lambda i: (0, 0))
    return pl.pallas_call(
        _qprep_kernel,
        grid=(t // tm,),
        in_specs=[
            pl.BlockSpec((tm, Q_LORA_RANK), lambda i: (i, 10)),
            pl.BlockSpec((tm, 1), lambda i: (i, 0)),
            full((1, Q_LORA_RANK)),
            full((Q_LORA_RANK, n)),
            full((1, n)),
            full((n, LANES)),
            full((1, LANES)),
            full((1, LANES)),
        ],
        out_specs=pl.BlockSpec((tm, n), lambda i: (i, 0)),
        out_shape=jax.ShapeDtypeStruct((t, n), BF16),
        compiler_params=_cparams(("parallel",)),
        name="qprep",
    )(proj, pos, gn, w, gfull, ind, freq, sign)


def _kprep_kernel(kvl_ref, kr_ref, pos_ref, gn_ref, w_ref, gnope_ref, grope_ref, ind_ref, freq_ref, sign_ref,
                  k_ref, v_ref):
    kvl = kvl_ref[...].astype(F32)
    ms = jnp.mean(kvl * kvl, axis=-1, keepdims=True)
    kn = (kvl * lax.rsqrt(ms + EPS) * gn_ref[...]).astype(BF16)
    kv = jnp.dot(kn, w_ref[...], preferred_element_type=F32)
    nd = N_HEADS * QK_NOPE_DIM
    knope = kv[:, :nd]
    v_ref[...] = kv[:, nd:].astype(v_ref.dtype)
    kr = kr_ref[...].astype(F32)
    ssq = jnp.dot((knope * knope).astype(BF16), ind_ref[...], preferred_element_type=F32)
    ssq = ssq + jnp.sum(kr * kr, axis=-1, keepdims=True)
    sc = lax.rsqrt(ssq * (1.0 / QK_HEAD_DIM) + EPS)
    ang = pos_ref[...] * freq_ref[...]
    kr_rot = _rope_mid(kr * grope_ref[...], jnp.cos(ang), jnp.sin(ang) * sign_ref[...])
    for h in range(N_HEADS):
        s_h = sc[:, h:h + 1]
        lo = h * HEAD_PAD
        nope = knope[:, h * LANES:(h + 1) * LANES] * s_h * gnope_ref[...]
        k_ref[:, lo:lo + LANES] = nope.astype(k_ref.dtype)
        k_ref[:, lo + LANES:lo + HEAD_PAD] = (kr_rot * s_h).astype(k_ref.dtype)


def _kprep(proj, pos, gn, w, gnope, grope, ind, freq, sign, *, tm=512):
    t = proj.shape[0]
    n = N_HEADS * HEAD_PAD
    nd = N_HEADS * QK_NOPE_DIM
    full = lambda shape: pl.BlockSpec(shape, lambda i: (0, 0))
    return pl.pallas_call(
        _kprep_kernel,
        grid=(t // tm,),
        in_specs=[
            pl.BlockSpec((tm, KV_LORA_RANK), lambda i: (i, 22)),
            pl.BlockSpec((tm, LANES), lambda i: (i, 46)),
            pl.BlockSpec((tm, 1), lambda i: (i, 0)),
            full((1, KV_LORA_RANK)),
            full((KV_LORA_RANK, 2 * nd)),
            full((1, LANES)),
            full((1, LANES)),
            full((nd, LANES)),
            full((1, LANES)),
            full((1, LANES)),
        ],
        out_specs=[pl.BlockSpec((tm, n), lambda i: (i, 0)), pl.BlockSpec((tm, nd), lambda i: (i, 0))],
        out_shape=[jax.ShapeDtypeStruct((t, n), BF16), jax.ShapeDtypeStruct((t, nd), BF16)],
        compiler_params=_cparams(("parallel",)),
        name="kprep",
    )(proj, proj, pos, gn, w, gnope, grope, ind, freq, sign)


def _attn_kernel(q_ref, k_ref, v_ref, wg_ref, wu_ref, wd_ref, o_ref, wgo_ref, wuo_ref, wdo_ref,
                 m_ref, acc_ref, *, tq, hg):
    wgo_ref[...] = wg_ref[...].astype(wgo_ref.dtype)
    wuo_ref[...] = wu_ref[...].astype(wuo_ref.dtype)
    wdo_ref[...] = wd_ref[...].astype(wdo_ref.dtype)

    qi = pl.program_id(2)
    m_ref[...] = jnp.full(m_ref.shape, NEG, F32)
    acc_ref[...] = jnp.zeros(acc_ref.shape, F32)
    ones = jnp.ones((tq, DENOM_ROWS), BF16)

    def block(j, masked):
        r0 = pl.multiple_of(j * tq, tq)
        for h in range(hg):
            q = q_ref[:, h * HEAD_PAD:(h + 1) * HEAD_PAD]
            kb = k_ref[pl.ds(r0, tq), h * HEAD_PAD:(h + 1) * HEAD_PAD]
            vb = v_ref[pl.ds(r0, tq), h * V_HEAD_DIM:(h + 1) * V_HEAD_DIM]
            st = lax.dot_general(kb, q, (((1,), (1,)), ((), ())), preferred_element_type=F32)
            if masked:
                key = lax.broadcasted_iota(jnp.int32, st.shape, 0) // CHUNK
                qry = lax.broadcasted_iota(jnp.int32, st.shape, 1) // CHUNK
                st = jnp.where(key <= qry, st, NEG)
            m_old = m_ref[h]
            m_new = jnp.maximum(m_old, jnp.max(st, axis=0, keepdims=True))
            alpha = jnp.exp2(m_old - m_new)
            p = jnp.exp2((st - m_new).astype(BF16))
            v_ext = jnp.concatenate([vb, ones], axis=1)
            pv = lax.dot_general(v_ext, p, (((0,), (0,)), ((), ())), preferred_element_type=F32)
            acc_ref[h] = alpha * acc_ref[h] + pv
            m_ref[h] = m_new

    def body(j, c):
        block(j, False)
        return c

    lax.fori_loop(0, qi, body, 0)
    block(qi, True)
    for h in range(hg):
        o = acc_ref[h, :V_HEAD_DIM] / acc_ref[h, V_HEAD_DIM:V_HEAD_DIM + 1]
        o_ref[:, h * V_HEAD_DIM:(h + 1) * V_HEAD_DIM] = o.T.astype(o_ref.dtype)


def _attention(q3, k3, v3, wg, wu, wd, *, tq=1024, hg=2):
    b, s, _ = q3.shape
    tq = min(tq, s)
    ng, nq = N_HEADS // hg, s // tq
    steps = b * ng * nq
    w2 = [w.reshape(-1, w.shape[-1]) for w in (wg, wu, wd)]
    slab = [w.shape[0] // steps for w in w2]
    step = lambda bi, h, i: ((bi * ng + h) * nq + i, 0)
    wspec = [pl.BlockSpec((r, w.shape[1]), step) for r, w in zip(slab, w2)]
    outs = pl.pallas_call(
        functools.partial(_attn_kernel, tq=tq, hg=hg),
        grid=(b, ng, nq),
        in_specs=[
            pl.BlockSpec((None, tq, hg * HEAD_PAD), lambda bi, h, i: (bi, i, h)),
            pl.BlockSpec((None, s, hg * HEAD_PAD), lambda bi, h, i: (bi, 0, h)),
            pl.BlockSpec((None, s, hg * V_HEAD_DIM), lambda bi, h, i: (bi, 0, h)),
        ] + wspec,
        out_specs=[pl.BlockSpec((None, tq, hg * V_HEAD_DIM), lambda bi, h, i: (bi, i, h))] + wspec,
        out_shape=[jax.ShapeDtypeStruct((b, s, N_HEADS * V_HEAD_DIM), BF16)]
        + [jax.ShapeDtypeStruct(w.shape, BF16) for w in w2],
        scratch_shapes=[pltpu.VMEM((hg, 1, tq), F32), pltpu.VMEM((hg, V_HEAD_DIM + DENOM_ROWS, tq), F32)],
        compiler_params=_cparams(("parallel", "parallel", "arbitrary")),
        name="attention",
    )(q3, k3, v3, *w2)
    return (outs[0],) + tuple(o.reshape(w.shape) for o, w in zip(outs[1:], (wg, wu, wd)))


def _s5_kernel(u_ref, perm_ref, permt_ref, bb_ref, cc_ref, a_ref, d_ref, wglu_ref, wo_ref, o_ref,
               ut_ref, bu_ref, y_ref, carry_ref, *, tc):
    nb = u_ref.shape[0]
    rows_sub = nb * SUB_T
    nsub = tc // SUB_T

    @pl.when(pl.program_id(0) == 0)
    def _():
        carry_ref[...] = jnp.zeros(carry_ref.shape, F32)

    for j in range(nsub):
        ub = u_ref[:, j * SUB_T:(j + 1) * SUB_T, :].reshape(rows_sub, S5_WIDTH)
        ut_ref[j * rows_sub:(j + 1) * rows_sub, :] = jnp.dot(
            perm_ref[...], ub, preferred_element_type=F32).astype(BF16)

    for c in range(N_CLUSTERS):
        bu_ref[...] = jnp.dot(ut_ref[:, c * LANES:(c + 1) * LANES], bb_ref[c], preferred_element_type=F32)
        ar = a_ref[c, :, :CLUSTER_STATES]
        ai = a_ref[c, :, CLUSTER_STATES:]

        def step(t, carry):
            xr, xi = carry
            r0 = pl.multiple_of(t * nb, nb)
            br = bu_ref[pl.ds(r0, nb), :CLUSTER_STATES]
            bi = bu_ref[pl.ds(r0, nb), CLUSTER_STATES:]
            nxr = ar * xr - ai * xi + br
            nxi = ar * xi + ai * xr + bi
            bu_ref[pl.ds(r0, nb), :CLUSTER_STATES] = nxr
            bu_ref[pl.ds(r0, nb), CLUSTER_STATES:] = nxi
            return nxr, nxi

        xr, xi = lax.fori_loop(0, tc, step, (carry_ref[c, :, :CLUSTER_STATES], carry_ref[c, :, CLUSTER_STATES:]),
                               unroll=8)
        carry_ref[c, :, :CLUSTER_STATES] = xr
        carry_ref[c, :, CLUSTER_STATES:] = xi
        y_ref[:, c * LANES:(c + 1) * LANES] = jnp.dot(bu_ref[...].astype(BF16), cc_ref[c], preferred_element_type=F32)

    y = y_ref[...] + d_ref[...] * ut_ref[...].astype(F32)
    g = jax.nn.gelu(y)
    z = g * jax.nn.sigmoid(jnp.dot(g.astype(BF16), wglu_ref[...], preferred_element_type=F32))
    yb = jnp.dot(z.astype(BF16), wo_ref[...], preferred_element_type=F32).astype(BF16)
    for j in range(nsub):
        blk = jnp.dot(permt_ref[...], yb[j * rows_sub:(j + 1) * rows_sub, :], preferred_element_type=F32)
        o_ref[:, j * SUB_T:(j + 1) * SUB_T, :] = blk.astype(o_ref.dtype).reshape(nb, SUB_T, o_ref.shape[2])


def _s5(proj3, perm, permt, bb, cc, a_bc, d, wglu, wo, *, tc=64):
    b, s, _ = proj3.shape
    rows = b * tc
    dm = wo.shape[1]
    c2 = lambda shape: pl.BlockSpec(shape, lambda i: (0, 0))
    c3 = lambda shape: pl.BlockSpec(shape, lambda i: (0, 0, 0))
    return pl.pallas_call(
        functools.partial(_s5_kernel, tc=tc),
        grid=(s // tc,),
        in_specs=[
            pl.BlockSpec((b, tc, S5_WIDTH), lambda i: (0, i, 4)),
            c2(perm.shape), c2(permt.shape), c3(bb.shape), c3(cc.shape), c3(a_bc.shape), c2(d.shape),
            c2(wglu.shape), c2(wo.shape),
        ],
        out_specs=pl.BlockSpec((b, tc, dm), lambda i: (0, i, 0)),
        out_shape=jax.ShapeDtypeStruct((b, s, dm), BF16),
        scratch_shapes=[
            pltpu.VMEM((rows, S5_WIDTH), BF16),
            pltpu.VMEM((rows, 2 * CLUSTER_STATES), F32),
            pltpu.VMEM((rows, S5_WIDTH), F32),
            pltpu.VMEM((N_CLUSTERS, b, 2 * CLUSTER_STATES), F32),
        ],
        compiler_params=_cparams(("arbitrary",)),
        name="s5",
    )(proj3, perm, permt, bb, cc, a_bc, d, wglu, wo)


def _merge_kernel(o_ref, gl_ref, bg_ref, yb_ref, w_ref, m_ref):
    d = o_ref.shape[1]
    ya = jnp.dot(o_ref[...], w_ref[...], preferred_element_type=F32)
    ga = jax.nn.sigmoid(gl_ref[:, :d].astype(F32) + bg_ref[:, :d])
    gb = jax.nn.sigmoid(gl_ref[:, d:].astype(F32) + bg_ref[:, d:])
    m_ref[...] = (ga * ya + gb * yb_ref[...].astype(F32)).astype(m_ref.dtype)


def _merge(o2, proj, bg, yb2, w, *, tm=512):
    t, d = o2.shape
    return pl.pallas_call(
        _merge_kernel,
        grid=(t // tm,),
        in_specs=[
            pl.BlockSpec((tm, d), lambda i: (i, 0)),
            pl.BlockSpec((tm, 2 * d), lambda i: (i, 0)),
            pl.BlockSpec((1, 2 * d), lambda i: (0, 0)),
            pl.BlockSpec((tm, d), lambda i: (i, 0)),
            pl.BlockSpec((d, d), lambda i: (0, 0)),
        ],
        out_specs=pl.BlockSpec((tm, d), lambda i: (i, 0)),
        out_shape=jax.ShapeDtypeStruct((t, d), BF16),
        compiler_params=_cparams(("parallel",)),
        name="merge",
    )(o2, proj, bg, yb2, w)


def _to_row_tiles(x):
    return x.reshape(x.shape[0], x.shape[1] // LANES, LANES)


def _from_row_tiles(x):
    return x.reshape(x.shape[0], x.shape[1] * LANES)


def _outproj_kernel(m_ref, x_ref, w_ref, g_ref, wrh_ref, wrl_ref, br_ref, x1_ref, h2_ref, idx_ref, wt_ref):
    x1 = x_ref[...] + jnp.dot(m_ref[...], w_ref[...], preferred_element_type=F32)
    x1_ref[...] = x1
    ms = jnp.mean(x1 * x1, axis=-1, keepdims=True)
    h2 = x1 * lax.rsqrt(ms + EPS) * g_ref[...]
    h_hi = h2.astype(BF16)
    h2_ref[...] = _to_row_tiles(h_hi)
    h_lo = (h2 - h_hi.astype(F32)).astype(BF16)
    logits = (jnp.dot(h_hi, wrh_ref[...], preferred_element_type=F32)
              + jnp.dot(h_lo, wrh_ref[...], preferred_element_type=F32)
              + jnp.dot(h_hi, wrl_ref[...], preferred_element_type=F32)) + br_ref[...]
    lane = lax.broadcasted_iota(jnp.int32, logits.shape, 1)
    work = jnp.where(lane < N_EXPERTS, logits, -jnp.inf)
    idx_out = jnp.zeros(logits.shape, jnp.int32)
    val_out = jnp.zeros(logits.shape, F32)
    v0 = None
    denom = None
    for k in range(TOP_K):
        mx = jnp.max(work, axis=-1, keepdims=True)
        sel = jnp.min(jnp.where(work == mx, lane, LANES), axis=-1, keepdims=True)
        if k == 0:
            v0 = mx
        e = jnp.exp(mx - v0)
        denom = e if k == 0 else denom + e
        idx_out = jnp.where(lane == k, sel, idx_out)
        val_out = jnp.where(lane == k, e, val_out)
        work = jnp.where(lane == sel, -jnp.inf, work)
    idx_ref[...] = idx_out
    wt_ref[...] = val_out / denom


def _outproj(m2, x2, w, g, wrh, wrl, br, *, tm=512):
    t, d = x2.shape
    c2 = lambda shape: pl.BlockSpec(shape, lambda i: (0, 0))
    row = lambda width: pl.BlockSpec((tm, width), lambda i: (i, 0))
    return pl.pallas_call(
        _outproj_kernel,
        grid=(t // tm,),
        in_specs=[row(d), row(d), c2((d, d)), c2((1, d)), c2((d, LANES)), c2((d, LANES)), c2((1, LANES))],
        out_specs=[row(d), pl.BlockSpec((tm, d // LANES, LANES), lambda i: (i, 0, 0)), row(LANES), row(LANES)],
        out_shape=[
            jax.ShapeDtypeStruct((t, d), F32),
            jax.ShapeDtypeStruct((t, d // LANES, LANES), BF16),
            jax.ShapeDtypeStruct((t, LANES), jnp.int32),
            jax.ShapeDtypeStruct((t, LANES), F32),
        ],
        compiler_params=_cparams(("parallel",)),
        name="outproj",
    )(m2, x2, w, g, wrh, wrl, br)


BOTH_DMA_PRIORITIES = (0, 1)


def _start_row_gather(idx_ref, n_rows, src_hbm, dst_ref, sem, priorities=BOTH_DMA_PRIORITIES):
    n_pri = len(priorities)

    def issue(r2, c):
        for u, pri in enumerate(priorities):
            r = r2 * n_pri + u
            pltpu.make_async_copy(src_hbm.at[pl.ds(idx_ref[0, 0, r], 1)], dst_ref.at[pl.ds(r, 1)], sem).start(priority=pri)
        return c

    lax.fori_loop(0, n_rows // n_pri, issue, 0, unroll=8 // n_pri)


def _wait_rows(dst_ref, sem):
    pltpu.make_async_copy(dst_ref, dst_ref, sem).wait()


def _expert_kernel(be_ref, nu_ref, tokc_ref, tokn_ref, h_hbm, wg_ref, bg_ref, wu_ref, bu_ref, wd_ref, bd_ref, o_ref,
                   xbuf_ref, xb_ref, acc_ref, sem, *, tm):
    b = pl.program_id(0)
    f = pl.program_id(1)
    nxt = b + 1

    @pl.when(jnp.logical_and(f == 0, jnp.logical_and(b == 0, nu_ref[0] > 0)))
    def _():
        _start_row_gather(tokc_ref, tm, h_hbm, xbuf_ref.at[0], sem.at[0], priorities=(0,))

    @pl.when(jnp.logical_and(f == 0, jnp.logical_and(nxt < pl.num_programs(0), nxt < nu_ref[0])))
    def _():
        _start_row_gather(tokn_ref, tm, h_hbm, xbuf_ref.at[nxt % 2], sem.at[nxt % 2], priorities=(0,))

    @pl.when(b < nu_ref[0])
    def _():
        @pl.when(f == 0)
        def _():
            _wait_rows(xbuf_ref.at[b % 2], sem.at[b % 2])
            xb_ref[...] = _from_row_tiles(xbuf_ref[b % 2])
            acc_ref[...] = jnp.zeros(acc_ref.shape, F32)

        xb = xb_ref[...]
        gate = jnp.dot(xb, wg_ref[0], preferred_element_type=F32) + bg_ref[0]
        up = jnp.dot(xb, wu_ref[0], preferred_element_type=F32) + bu_ref[0]
        gate = jnp.minimum(gate, SWIGLU_LIMIT)
        up = jnp.clip(up, -SWIGLU_LIMIT, SWIGLU_LIMIT)
        glu = gate * jax.nn.sigmoid(SWIGLU_ALPHA * gate)
        act = ((up + 1.0) * glu).astype(BF16)
        acc_ref[...] += jnp.dot(act, wd_ref[0], preferred_element_type=F32)

        @pl.when(f == pl.num_programs(1) - 1)
        def _():
            o_ref[...] = _to_row_tiles((acc_ref[...] + bd_ref[0]).astype(o_ref.dtype))

    @pl.when(jnp.logical_and(b >= nu_ref[0], f == 0))
    def _():
        o_ref[...] = jnp.zeros(o_ref.shape, o_ref.dtype)


def _experts(block_expert, nused, slot_tok, h2, wg, bg, wu, bu, wd, bd, *, tm, tf=1024):
    n_slots = slot_tok.shape[0]
    tile = h2.shape[1:]
    d = tile[0] * tile[1]
    nb = n_slots // tm
    dff = wg.shape[2]
    nf = dff // tf
    tok3 = slot_tok.reshape(nb, 1, tm)

    def blk(b, nu):
        return jnp.minimum(b, nu[0] - 1)

    def fidx(b, f, nu):
        return jnp.where(b < nu[0], f, nf - 1)

    return pl.pallas_call(
        functools.partial(_expert_kernel, tm=tm),
        grid_spec=pltpu.PrefetchScalarGridSpec(
            num_scalar_prefetch=2,
            grid=(nb, nf),
            in_specs=[
                pl.BlockSpec((1, 1, tm), lambda b, f, be, nu: (b, 0, 0), memory_space=pltpu.SMEM),
                pl.BlockSpec((1, 1, tm), lambda b, f, be, nu: (jnp.minimum(b + 1, nb - 1), 0, 0),
                             memory_space=pltpu.SMEM),
                pl.BlockSpec(memory_space=pl.ANY),
                pl.BlockSpec((1, d, tf), lambda b, f, be, nu: (be[blk(b, nu)], 0, fidx(b, f, nu))),
                pl.BlockSpec((1, 1, tf), lambda b, f, be, nu: (be[blk(b, nu)], 0, fidx(b, f, nu))),
                pl.BlockSpec((1, d, tf), lambda b, f, be, nu: (be[blk(b, nu)], 0, fidx(b, f, nu))),
                pl.BlockSpec((1, 1, tf), lambda b, f, be, nu: (be[blk(b, nu)], 0, fidx(b, f, nu))),
                pl.BlockSpec((1, tf, d), lambda b, f, be, nu: (be[blk(b, nu)], fidx(b, f, nu), 0)),
                pl.BlockSpec((1, 1, d), lambda b, f, be, nu: (be[blk(b, nu)], 0, 0)),
            ],
            out_specs=pl.BlockSpec((tm,) + tile, lambda b, f, be, nu: (b, 0, 0)),
            scratch_shapes=[pltpu.VMEM((2, tm) + tile, h2.dtype), pltpu.VMEM((tm, d), BF16), pltpu.VMEM((tm, d), F32),
                            pltpu.SemaphoreType.DMA((2,))],
        ),
        out_shape=jax.ShapeDtypeStruct((n_slots,) + tile, BF16),
        compiler_params=_cparams(("arbitrary", "arbitrary")),
        name="experts",
    )(block_expert, nused, tok3, tok3, h2, wg, bg, wu, bu, wd, bd)


def _combine_kernel(posc_ref, posn_ref, x1_ref, wt_ref, ys_hbm, o_ref, buf_ref, sem, *, tm):
    i = pl.program_id(0)
    nxt = i + 1

    @pl.when(i == 0)
    def _():
        _start_row_gather(posc_ref, TOP_K * tm, ys_hbm, buf_ref.at[0], sem.at[0])

    @pl.when(nxt < pl.num_programs(0))
    def _():
        _start_row_gather(posn_ref, TOP_K * tm, ys_hbm, buf_ref.at[nxt % 2], sem.at[nxt % 2])

    slot = i % 2
    _wait_rows(buf_ref.at[slot], sem.at[slot])
    acc = x1_ref[...]
    for k in range(TOP_K):
        rows = buf_ref[slot, k * tm:(k + 1) * tm]
        acc = acc + wt_ref[:, k:k + 1] * _from_row_tiles(rows).astype(F32)
    o_ref[...] = acc


def _combine(pos_km, x1, wt, ys, *, tm=128):
    t, d = x1.shape
    n = t // tm
    return pl.pallas_call(
        functools.partial(_combine_kernel, tm=tm),
        grid=(n,),
        in_specs=[
            pl.BlockSpec((1, 1, TOP_K * tm), lambda i: (i, 0, 0), memory_space=pltpu.SMEM),
            pl.BlockSpec((1, 1, TOP_K * tm), lambda i: (jnp.minimum(i + 1, n - 1), 0, 0), memory_space=pltpu.SMEM),
            pl.BlockSpec((tm, d), lambda i: (i, 0)),
            pl.BlockSpec((tm, LANES), lambda i: (i, 0)),
            pl.BlockSpec(memory_space=pl.ANY),
        ],
        out_specs=pl.BlockSpec((tm, d), lambda i: (i, 0)),
        out_shape=jax.ShapeDtypeStruct((t, d), F32),
        scratch_shapes=[pltpu.VMEM((2, TOP_K * tm) + ys.shape[1:], ys.dtype), pltpu.SemaphoreType.DMA((2,))],
        compiler_params=_cparams(("arbitrary",)),
        name="combine",
    )(pos_km, pos_km, x1, wt, ys)


def _pad_heads(w, per_head):
    lead = w.shape[:-1]
    w = w.reshape(lead + (N_HEADS, per_head))
    w = jnp.pad(w, [(0, 0)] * len(lead) + [(0, 0), (0, HEAD_PAD - per_head)])
    return w.reshape(lead + (N_HEADS * HEAD_PAD,))


def _s5_discretise(lam_re, lam_im, log_dt, b_re, b_im):
    dt = jnp.exp(log_dt.astype(F32))[:, None]
    lr, li = lam_re.astype(F32), lam_im.astype(F32)
    mag = jnp.exp(lr * dt)
    ar, ai = mag * jnp.cos(li * dt), mag * jnp.sin(li * dt)
    den = lr * lr + li * li
    zr = ((ar - 1.0) * lr + ai * li) / den
    zi = (ai * lr - (ar - 1.0) * li) / den
    br, bi = b_re.astype(F32), b_im.astype(F32)
    bbr = zr[..., None] * br - zi[..., None] * bi
    bbi = zr[..., None] * bi + zi[..., None] * br
    return ar, ai, bbr, bbi


def _s5_pack(ar, ai, bbr, bbi, c_re, c_im, nb):
    eye = jnp.eye(S5_CLUSTER, dtype=F32)

    def pack_b(m):
        m4 = m.reshape(N_CLUSTERS, S5_CLUSTER, S5_STATE, S5_GROUP)
        return jnp.einsum('xgpc,gh->xgchp', m4, eye).reshape(N_CLUSTERS, S5_CLUSTER * S5_GROUP, CLUSTER_STATES)

    def pack_c(m):
        m4 = m.reshape(N_CLUSTERS, S5_CLUSTER, S5_GROUP, S5_STATE)
        return jnp.einsum('xgcp,gh->xgphc', m4, eye).reshape(N_CLUSTERS, CLUSTER_STATES, S5_CLUSTER * S5_GROUP)

    bb = jnp.concatenate([pack_b(bbr), pack_b(bbi)], axis=2).astype(BF16)
    cc = jnp.concatenate([pack_c(c_re.astype(F32)), -pack_c(c_im.astype(F32))], axis=1).astype(BF16)
    a = jnp.concatenate([ar.reshape(N_CLUSTERS, CLUSTER_STATES), ai.reshape(N_CLUSTERS, CLUSTER_STATES)], axis=1)
    a_bc = jnp.broadcast_to(a[:, None, :], (N_CLUSTERS, nb, 2 * CLUSTER_STATES))
    return bb, cc, a_bc


def _perm_matrix(nb):
    n = nb * SUB_T
    p = np.zeros((n, n), np.float32)
    for b in range(nb):
        for t in range(SUB_T):
            p[t * nb + b, b * SUB_T + t] = 1.0
    return p


def kernel(x, positions, norm1_g, w_in, b_gates, q_norm_g, w_uq, kv_norm_g, w_ukv, qk_norm_q_g, qk_norm_k_g, w_o_mla, s5_lambda_re, s5_lambda_im, s5_log_dt, s5_b_re, s5_b_im, s5_c_re, s5_c_im, s5_d, w_glu, w_o_s5, w_out, norm2_g, w_router, b_router, w_gate, b_gate, w_up, b_up, w_down, b_down):
    bsz, seq, d = x.shape
    t = bsz * seq
    depth = norm1_g.shape[0]
    o1 = Q_LORA_RANK
    o2 = o1 + KV_LORA_RANK
    o3 = o2 + QK_ROPE_DIM
    o4 = o3 + S5_WIDTH
    half = ROPE_HALF
    inv_freq = ROPE_THETA ** (-jnp.arange(half, dtype=F32) / half)
    lane = np.arange(LANES)
    freq = jnp.where(lane < QK_ROPE_DIM, jnp.tile(inv_freq, LANES // half), 0.0).reshape(1, LANES).astype(F32)
    sign = jnp.asarray(np.where(lane % QK_ROPE_DIM < half, -1.0, 1.0).reshape(1, LANES), F32)
    pos = positions.reshape(t, 1).astype(F32)
    sm_scale = math.log2(math.e) / math.sqrt(QK_HEAD_DIM)
    ind_q = jnp.asarray(np.equal.outer(np.arange(N_HEADS * HEAD_PAD) // HEAD_PAD, lane), BF16)
    ind_k = jnp.asarray(np.equal.outer(np.arange(N_HEADS * QK_NOPE_DIM) // QK_NOPE_DIM, lane), BF16)
    perm_np = _perm_matrix(bsz)
    perm = jnp.asarray(perm_np, BF16)
    permt = jnp.asarray(perm_np.T, BF16)
    tm_e = 512
    n_assign = t * TOP_K
    nb_e = n_assign // tm_e + N_EXPERTS
    n_slots = nb_e * tm_e
    tm_c = 128

    for l in range(depth):
        wi = w_in[l]
        w_in_p = jnp.concatenate(
            [wi[:, o4:], wi[:, o3:o4], wi[:, :o1], wi[:, o1:o2], wi[:, o2:o3],
             jnp.zeros((d, PROJ_W - wi.shape[1]), wi.dtype)], axis=1).astype(BF16)
        w_uq_p = _pad_heads(w_uq[l], QK_HEAD_DIM).astype(BF16)
        gq_full = _pad_heads(jnp.tile(qk_norm_q_g[l].astype(F32), N_HEADS) * sm_scale, QK_HEAD_DIM).reshape(1, -1)
        wkv = w_ukv[l].reshape(KV_LORA_RANK, N_HEADS, QK_NOPE_DIM + V_HEAD_DIM)
        w_kv_p = jnp.concatenate(
            [wkv[:, :, :QK_NOPE_DIM].reshape(KV_LORA_RANK, -1), wkv[:, :, QK_NOPE_DIM:].reshape(KV_LORA_RANK, -1)],
            axis=1).astype(BF16)
        gk = qk_norm_k_g[l].astype(F32)
        gk_nope = gk[:QK_NOPE_DIM].reshape(1, LANES)
        gk_rope = jnp.pad(gk[QK_NOPE_DIM:], (0, LANES - QK_ROPE_DIM)).reshape(1, LANES)
        ar, ai, bbr, bbi = _s5_discretise(s5_lambda_re[l], s5_lambda_im[l], s5_log_dt[l], s5_b_re[l], s5_b_im[l])
        bb, cc, a_bc = _s5_pack(ar, ai, bbr, bbi, s5_c_re[l], s5_c_im[l], bsz)
        wr = jnp.pad(w_router[l].astype(F32), ((0, 0), (0, LANES - N_EXPERTS)))
        wr_hi = wr.astype(BF16)
        wr_lo = (wr - wr_hi.astype(F32)).astype(BF16)
        br = jnp.pad(b_router[l].astype(F32), (0, LANES - N_EXPERTS)).reshape(1, LANES)

        x2 = x.reshape(t, d)
        proj = _inproj(x2, norm1_g[l].reshape(1, d), w_in_p)
        q = _qprep(proj, pos, q_norm_g[l].reshape(1, -1), w_uq_p, gq_full, ind_q, freq, sign)
        k, v = _kprep(proj, pos, kv_norm_g[l].reshape(1, -1), w_kv_p, gk_nope, gk_rope, ind_k, freq, sign)
        o, wg_b, wu_b, wd_b = _attention(q.reshape(bsz, seq, -1), k.reshape(bsz, seq, -1), v.reshape(bsz, seq, -1),
                                         w_gate[l], w_up[l], w_down[l])
        yb = _s5(proj.reshape(bsz, seq, PROJ_W), perm, permt, bb, cc, a_bc, s5_d[l].reshape(1, -1).astype(F32),
                 w_glu[l].astype(BF16), w_o_s5[l].astype(BF16))
        m = _merge(o.reshape(t, -1), proj, b_gates[l].reshape(1, -1).astype(F32), yb.reshape(t, d),
                   w_o_mla[l].astype(BF16))
        x1, h2, top_idx, top_w = _outproj(m, x2, w_out[l].astype(BF16), norm2_g[l].reshape(1, d), wr_hi, wr_lo, br)

        flat_e = top_idx[:, :TOP_K].reshape(-1)
        onehot = (flat_e[:, None] == jnp.arange(N_EXPERTS, dtype=jnp.int32)[None, :]).astype(jnp.int32)
        csum = jnp.cumsum(onehot, axis=0)
        rank = jnp.take_along_axis(csum, flat_e[:, None], axis=1)[:, 0] - 1
        counts = csum[-1]
        nblk = (counts + tm_e - 1) // tm_e
        blk_end = jnp.cumsum(nblk)
        blk_start = blk_end - nblk
        slot = blk_start[flat_e] * tm_e + rank
        nused = blk_end[-1:].astype(jnp.int32)
        block_expert = jnp.minimum(
            jnp.sum((blk_end[None, :] <= jnp.arange(nb_e, dtype=jnp.int32)[:, None]).astype(jnp.int32), axis=1),
            N_EXPERTS - 1).astype(jnp.int32)
        slot_tok = jnp.zeros((n_slots,), jnp.int32).at[slot].set(jnp.arange(n_assign, dtype=jnp.int32) // TOP_K)
        pos_km = slot.reshape(t // tm_c, tm_c, TOP_K).transpose(0, 2, 1).reshape(t // tm_c, 1, TOP_K * tm_c)

        ys = _experts(block_expert, nused, slot_tok, h2,
                      wg_b, b_gate[l].reshape(N_EXPERTS, 1, -1).astype(F32),
                      wu_b, b_up[l].reshape(N_EXPERTS, 1, -1).astype(F32),
                      wd_b, b_down[l].reshape(N_EXPERTS, 1, -1).astype(F32), tm=tm_e)
        x = _combine(pos_km, x1, top_w, ys, tm=tm_c).reshape(bsz, seq, d)
    return x
```

```python
import functools
import math

import numpy as np
import jax
import jax.numpy as jnp
from jax import lax
from jax.experimental import pallas as pl
from jax.experimental.pallas import tpu as pltpu

F32 = jnp.float32
BF16 = jnp.bfloat16

D_MODEL = 2048
CHUNK = 64
EPS = 1e-6
N_HEADS = 16
QK_NOPE_DIM = 128
QK_ROPE_DIM = 64
QK_HEAD_DIM = QK_NOPE_DIM + QK_ROPE_DIM
V_HEAD_DIM = 128
Q_LORA_RANK = 512
KV_LORA_RANK = 256
ROPE_THETA = 10000.0
S5_WIDTH = 1024
S5_GROUP = 16
S5_GROUPS = S5_WIDTH // S5_GROUP
S5_STATE = 64
N_EXPERTS = 32
TOP_K = 4
D_FF = 2048
SWIGLU_LIMIT = 7.0
SWIGLU_ALPHA = 1.702

LANES = 128
HEAD_PAD = 256
ROPE_HALF = QK_ROPE_DIM // 2
S5_CLUSTER = 8
N_CLUSTERS = S5_GROUPS // S5_CLUSTER
CLUSTER_STATES = S5_CLUSTER * S5_STATE
SUB_T = 32
PROJ_W = 6144
VMEM_LIMIT = 56 * 1024 * 1024
NEG = -1e30
DENOM_ROWS = 8


def _cparams(sem, vmem=VMEM_LIMIT, **kw):
    return pltpu.CompilerParams(dimension_semantics=sem, vmem_limit_bytes=vmem, **kw)


def _inproj_kernel(x_ref, g_ref, w_ref, o_ref, h_ref):
    @pl.when(pl.program_id(1) == 0)
    def _():
        x = x_ref[...]
        ms = jnp.mean(x * x, axis=-1, keepdims=True)
        h_ref[...] = (x * lax.rsqrt(ms + EPS) * g_ref[...]).astype(BF16)

    o_ref[...] = jnp.dot(h_ref[...], w_ref[...], preferred_element_type=F32).astype(o_ref.dtype)


def _inproj(x2, g, w, *, tm=1024, tn=1536):
    t, d = x2.shape
    n = w.shape[1]
    return pl.pallas_call(
        _inproj_kernel,
        grid=(t // tm, n // tn),
        in_specs=[
            pl.BlockSpec((tm, d), lambda i, j: (i, 0)),
            pl.BlockSpec((1, d), lambda i, j: (0, 0)),
            pl.BlockSpec((d, tn), lambda i, j: (0, j)),
        ],
        out_specs=pl.BlockSpec((tm, tn), lambda i, j: (i, j)),
        out_shape=jax.ShapeDtypeStruct((t, n), BF16),
        scratch_shapes=[pltpu.VMEM((tm, d), BF16)],
        compiler_params=_cparams(("parallel", "arbitrary")),
        name="inproj",
    )(x2, g, w)


def _rope_mid(mid, cos, sin_signed):
    lane = lax.broadcasted_iota(jnp.int32, mid.shape, 1)
    rot = jnp.where(lane < ROPE_HALF, pltpu.roll(mid, LANES - ROPE_HALF, 1), pltpu.roll(mid, ROPE_HALF, 1))
    return mid * cos + rot * sin_signed


def _qprep_kernel(ql_ref, pos_ref, gn_ref, w_ref, gfull_ref, ind_ref, freq_ref, sign_ref, o_ref):
    ql = ql_ref[...].astype(F32)
    ms = jnp.mean(ql * ql, axis=-1, keepdims=True)
    qn = (ql * lax.rsqrt(ms + EPS) * gn_ref[...]).astype(BF16)
    q = jnp.dot(qn, w_ref[...], preferred_element_type=F32)
    ssq = jnp.dot((q * q).astype(BF16), ind_ref[...], preferred_element_type=F32)
    sc = lax.rsqrt(ssq * (1.0 / QK_HEAD_DIM) + EPS)
    ang = pos_ref[...] * freq_ref[...]
    cos = jnp.cos(ang)
    sin_signed = jnp.sin(ang) * sign_ref[...]
    for h in range(N_HEADS):
        s_h = sc[:, h:h + 1]
        lo = h * HEAD_PAD
        nope = q[:, lo:lo + LANES] * s_h * gfull_ref[:, lo:lo + LANES]
        mid = q[:, lo + LANES:lo + HEAD_PAD] * s_h * gfull_ref[:, lo + LANES:lo + HEAD_PAD]
        o_ref[:, lo:lo + LANES] = nope.astype(o_ref.dtype)
        o_ref[:, lo + LANES:lo + HEAD_PAD] = _rope_mid(mid, cos, sin_signed).astype(o_ref.dtype)


def _qprep(proj, pos, gn, w, gfull, ind, freq, sign, *, tm=512):
    t = proj.shape[0]
    n = N_HEADS * HEAD_PAD
    full = lambda shape: pl.BlockSpec(shape, lambda i: (0, 0))
    return pl.pallas_call(
        _qprep_kernel,
        grid=(t // tm,),
        in_specs=[
            pl.BlockSpec((tm, Q_LORA_RANK), lambda i: (i, 10)),
            pl.BlockSpec((tm, 1), lambda i: (i, 0)),
            full((1, Q_LORA_RANK)),
            full((Q_LORA_RANK, n)),
            full((1, n)),
            full((n, LANES)),
            full((1, LANES)),
            full((1, LANES)),
        ],
        out_specs=pl.BlockSpec((tm, n), lambda i: (i, 0)),
        out_shape=jax.ShapeDtypeStruct((t, n), BF16),
        compiler_params=_cparams(("parallel",)),
        name="qprep",
    )(proj, pos, gn, w, gfull, ind, freq, sign)


def _kprep_kernel(kvl_ref, kr_ref, pos_ref, gn_ref, w_ref, gnope_ref, grope_ref, ind_ref, freq_ref, sign_ref,
                  k_ref, v_ref):
    kvl = kvl_ref[...].astype(F32)
    ms = jnp.mean(kvl * kvl, axis=-1, keepdims=True)
    kn = (kvl * lax.rsqrt(ms + EPS) * gn_ref[...]).astype(BF16)
    kv = jnp.dot(kn, w_ref[...], preferred_element_type=F32)
    nd = N_HEADS * QK_NOPE_DIM
    knope = kv[:, :nd]
    v_ref[...] = kv[:, nd:].astype(v_ref.dtype)
    kr = kr_ref[...].astype(F32)
    ssq = jnp.dot((knope * knope).astype(BF16), ind_ref[...], preferred_element_type=F32)
    ssq = ssq + jnp.sum(kr * kr, axis=-1, keepdims=True)
    sc = lax.rsqrt(ssq * (1.0 / QK_HEAD_DIM) + EPS)
    ang = pos_ref[...] * freq_ref[...]
    kr_rot = _rope_mid(kr * grope_ref[...], jnp.cos(ang), jnp.sin(ang) * sign_ref[...])
    for h in range(N_HEADS):
        s_h = sc[:, h:h + 1]
        lo = h * HEAD_PAD
        nope = knope[:, h * LANES:(h + 1) * LANES] * s_h * gnope_ref[...]
        k_ref[:, lo:lo + LANES] = nope.astype(k_ref.dtype)
        k_ref[:, lo + LANES:lo + HEAD_PAD] = (kr_rot * s_h).astype(k_ref.dtype)


def _kprep(proj, pos, gn, w, gnope, grope, ind, freq, sign, *, tm=512):
    t = proj.shape[0]
    n = N_HEADS * HEAD_PAD
    nd = N_HEADS * QK_NOPE_DIM
    full = lambda shape: pl.BlockSpec(shape, lambda i: (0, 0))
    return pl.pallas_call(
        _kprep_kernel,
        grid=(t // tm,),
        in_specs=[
            pl.BlockSpec((tm, KV_LORA_RANK), lambda i: (i, 22)),
            pl.BlockSpec((tm, LANES), lambda i: (i, 46)),
            pl.BlockSpec((tm, 1), lambda i: (i, 0)),
            full((1, KV_LORA_RANK)),
            full((KV_LORA_RANK, 2 * nd)),
            full((1, LANES)),
            full((1, LANES)),
            full((nd, LANES)),
            full((1, LANES)),
            full((1, LANES)),
        ],
        out_specs=[pl.BlockSpec((tm, n), lambda i: (i, 0)), pl.BlockSpec((tm, nd), lambda i: (i, 0))],
        out_shape=[jax.ShapeDtypeStruct((t, n), BF16), jax.ShapeDtypeStruct((t, nd), BF16)],
        compiler_params=_cparams(("parallel",)),
        name="kprep",
    )(proj, proj, pos, gn, w, gnope, grope, ind, freq, sign)


def _attn_kernel(q_ref, k_ref, v_ref, wg_ref, wu_ref, wd_ref, o_ref, wgo_ref, wuo_ref, wdo_ref,
                 m_ref, acc_ref, *, tq, hg):
    wgo_ref[...] = wg_ref[...].astype(wgo_ref.dtype)
    wuo_ref[...] = wu_ref[...].astype(wuo_ref.dtype)
    wdo_ref[...] = wd_ref[...].astype(wdo_ref.dtype)

    qi = pl.program_id(2)
    m_ref[...] = jnp.full(m_ref.shape, NEG, F32)
    acc_ref[...] = jnp.zeros(acc_ref.shape, F32)
    ones = jnp.ones((tq, DENOM_ROWS), BF16)

    def block(j, masked):
        r0 = pl.multiple_of(j * tq, tq)
        for h in range(hg):
            q = q_ref[:, h * HEAD_PAD:(h + 1) * HEAD_PAD]
            kb = k_ref[pl.ds(r0, tq), h * HEAD_PAD:(h + 1) * HEAD_PAD]
            vb = v_ref[pl.ds(r0, tq), h * V_HEAD_DIM:(h + 1) * V_HEAD_DIM]
            st = lax.dot_general(kb, q, (((1,), (1,)), ((), ())), preferred_element_type=F32)
            if masked:
                key = lax.broadcasted_iota(jnp.int32, st.shape, 0) // CHUNK
                qry = lax.broadcasted_iota(jnp.int32, st.shape, 1) // CHUNK
                st = jnp.where(key <= qry, st, NEG)
            m_old = m_ref[h]
            m_new = jnp.maximum(m_old, jnp.max(st, axis=0, keepdims=True))
            alpha = jnp.exp2(m_old - m_new)
            p = jnp.exp2((st - m_new).astype(BF16))
            v_ext = jnp.concatenate([vb, ones], axis=1)
            pv = lax.dot_general(v_ext, p, (((0,), (0,)), ((), ())), preferred_element_type=F32)
            acc_ref[h] = alpha * acc_ref[h] + pv
            m_ref[h] = m_new

    def body(j, c):
        block(j, False)
        return c

    lax.fori_loop(0, qi, body, 0)
    block(qi, True)
    for h in range(hg):
        o = acc_ref[h, :V_HEAD_DIM] / acc_ref[h, V_HEAD_DIM:V_HEAD_DIM + 1]
        o_ref[:, h * V_HEAD_DIM:(h + 1) * V_HEAD_DIM] = o.T.astype(o_ref.dtype)


def _attention(q3, k3, v3, wg, wu, wd, *, tq=1024, hg=2):
    b, s, _ = q3.shape
    tq = min(tq, s)
    ng, nq = N_HEADS // hg, s // tq
    steps = b * ng * nq
    w2 = [w.reshape(-1, w.shape[-1]) for w in (wg, wu, wd)]
    slab = [w.shape[0] // steps for w in w2]
    step = lambda bi, h, i: ((bi * ng + h) * nq + i, 0)
    wspec = [pl.BlockSpec((r, w.shape[1]), step) for r, w in zip(slab, w2)]
    outs = pl.pallas_call(
        functools.partial(_attn_kernel, tq=tq, hg=hg),
        grid=(b, ng, nq),
        in_specs=[
            pl.BlockSpec((None, tq, hg * HEAD_PAD), lambda bi, h, i: (bi, i, h)),
            pl.BlockSpec((None, s, hg * HEAD_PAD), lambda bi, h, i: (bi, 0, h)),
            pl.BlockSpec((None, s, hg * V_HEAD_DIM), lambda bi, h, i: (bi, 0, h)),
        ] + wspec,
        out_specs=[pl.BlockSpec((None, tq, hg * V_HEAD_DIM), lambda bi, h, i: (bi, i, h))] + wspec,
        out_shape=[jax.ShapeDtypeStruct((b, s, N_HEADS * V_HEAD_DIM), BF16)]
        + [jax.ShapeDtypeStruct(w.shape, BF16) for w in w2],
        scratch_shapes=[pltpu.VMEM((hg, 1, tq), F32), pltpu.VMEM((hg, V_HEAD_DIM + DENOM_ROWS, tq), F32)],
        compiler_params=_cparams(("parallel", "parallel", "arbitrary")),
        name="attention",
    )(q3, k3, v3, *w2)
    return (outs[0],) + tuple(o.reshape(w.shape) for o, w in zip(outs[1:], (wg, wu, wd)))


def _s5_kernel(u_ref, perm_ref, permt_ref, bb_ref, cc_ref, a_ref, d_ref, wglu_ref, wo_ref, o_ref,
               ut_ref, bu_ref, y_ref, carry_ref, *, tc):
    nb = u_ref.shape[0]
    rows_sub = nb * SUB_T
    nsub = tc // SUB_T

    @pl.when(pl.program_id(0) == 0)
    def _():
        carry_ref[...] = jnp.zeros(carry_ref.shape, F32)

    for j in range(nsub):
        ub = u_ref[:, j * SUB_T:(j + 1) * SUB_T, :].reshape(rows_sub, S5_WIDTH)
        ut_ref[j * rows_sub:(j + 1) * rows_sub, :] = jnp.dot(
            perm_ref[...], ub, preferred_element_type=F32).astype(BF16)

    for c in range(N_CLUSTERS):
        bu_ref[...] = jnp.dot(ut_ref[:, c * LANES:(c + 1) * LANES], bb_ref[c], preferred_element_type=F32)
        ar = a_ref[c, :, :CLUSTER_STATES]
        ai = a_ref[c, :, CLUSTER_STATES:]

        def step(t, carry):
            xr, xi = carry
            r0 = pl.multiple_of(t * nb, nb)
            br = bu_ref[pl.ds(r0, nb), :CLUSTER_STATES]
            bi = bu_ref[pl.ds(r0, nb), CLUSTER_STATES:]
            nxr = ar * xr - ai * xi + br
            nxi = ar * xi + ai * xr + bi
            bu_ref[pl.ds(r0, nb), :CLUSTER_STATES] = nxr
            bu_ref[pl.ds(r0, nb), CLUSTER_STATES:] = nxi
            return nxr, nxi

        xr, xi = lax.fori_loop(0, tc, step, (carry_ref[c, :, :CLUSTER_STATES], carry_ref[c, :, CLUSTER_STATES:]),
                               unroll=8)
        carry_ref[c, :, :CLUSTER_STATES] = xr
        carry_ref[c, :, CLUSTER_STATES:] = xi
        y_ref[:, c * LANES:(c + 1) * LANES] = jnp.dot(bu_ref[...].astype(BF16), cc_ref[c], preferred_element_type=F32)

    y = y_ref[...] + d_ref[...] * ut_ref[...].astype(F32)
    g = jax.nn.gelu(y)
    z = g * jax.nn.sigmoid(jnp.dot(g.astype(BF16), wglu_ref[...], preferred_element_type=F32))
    yb = jnp.dot(z.astype(BF16), wo_ref[...], preferred_element_type=F32).astype(BF16)
    for j in range(nsub):
        blk = jnp.dot(permt_ref[...], yb[j * rows_sub:(j + 1) * rows_sub, :], preferred_element_type=F32)
        o_ref[:, j * SUB_T:(j + 1) * SUB_T, :] = blk.astype(o_ref.dtype).reshape(nb, SUB_T, o_ref.shape[2])


def _s5(proj3, perm, permt, bb, cc, a_bc, d, wglu, wo, *, tc=64):
    b, s, _ = proj3.shape
    rows = b * tc
    dm = wo.shape[1]
    c2 = lambda shape: pl.BlockSpec(shape, lambda i: (0, 0))
    c3 = lambda shape: pl.BlockSpec(shape, lambda i: (0, 0, 0))
    return pl.pallas_call(
        functools.partial(_s5_kernel, tc=tc),
        grid=(s // tc,),
        in_specs=[
            pl.BlockSpec((b, tc, S5_WIDTH), lambda i: (0, i, 4)),
            c2(perm.shape), c2(permt.shape), c3(bb.shape), c3(cc.shape), c3(a_bc.shape), c2(d.shape),
            c2(wglu.shape), c2(wo.shape),
        ],
        out_specs=pl.BlockSpec((b, tc, dm), lambda i: (0, i, 0)),
        out_shape=jax.ShapeDtypeStruct((b, s, dm), BF16),
        scratch_shapes=[
            pltpu.VMEM((rows, S5_WIDTH), BF16),
            pltpu.VMEM((rows, 2 * CLUSTER_STATES), F32),
            pltpu.VMEM((rows, S5_WIDTH), F32),
            pltpu.VMEM((N_CLUSTERS, b, 2 * CLUSTER_STATES), F32),
        ],
        compiler_params=_cparams(("arbitrary",)),
        name="s5",
    )(proj3, perm, permt, bb, cc, a_bc, d, wglu, wo)


def _merge_kernel(o_ref, gl_ref, bg_ref, yb_ref, w_ref, m_ref):
    d = o_ref.shape[1]
    ya = jnp.dot(o_ref[...], w_ref[...], preferred_element_type=F32)
    ga = jax.nn.sigmoid(gl_ref[:, :d].astype(F32) + bg_ref[:, :d])
    gb = jax.nn.sigmoid(gl_ref[:, d:].astype(F32) + bg_ref[:, d:])
    m_ref[...] = (ga * ya + gb * yb_ref[...].astype(F32)).astype(m_ref.dtype)


def _merge(o2, proj, bg, yb2, w, *, tm=512):
    t, d = o2.shape
    return pl.pallas_call(
        _merge_kernel,
        grid=(t // tm,),
        in_specs=[
            pl.BlockSpec((tm, d), lambda i: (i, 0)),
            pl.BlockSpec((tm, 2 * d), lambda i: (i, 0)),
            pl.BlockSpec((1, 2 * d), lambda i: (0, 0)),
            pl.BlockSpec((tm, d), lambda i: (i, 0)),
            pl.BlockSpec((d, d), lambda i: (0, 0)),
        ],
        out_specs=pl.BlockSpec((tm, d), lambda i: (i, 0)),
        out_shape=jax.ShapeDtypeStruct((t, d), BF16),
        compiler_params=_cparams(("parallel",)),
        name="merge",
    )(o2, proj, bg, yb2, w)


def _to_row_tiles(x):
    return x.reshape(x.shape[0], x.shape[1] // LANES, LANES)


def _from_row_tiles(x):
    return x.reshape(x.shape[0], x.shape[1] * LANES)


def _outproj_kernel(m_ref, x_ref, w_ref, g_ref, wrh_ref, wrl_ref, br_ref, x1_ref, h2_ref, idx_ref, wt_ref):
    x1 = x_ref[...] + jnp.dot(m_ref[...], w_ref[...], preferred_element_type=F32)
    x1_ref[...] = x1
    ms = jnp.mean(x1 * x1, axis=-1, keepdims=True)
    h2 = x1 * lax.rsqrt(ms + EPS) * g_ref[...]
    h_hi = h2.astype(BF16)
    h2_ref[...] = _to_row_tiles(h_hi)
    h_lo = (h2 - h_hi.astype(F32)).astype(BF16)
    logits = (jnp.dot(h_hi, wrh_ref[...], preferred_element_type=F32)
              + jnp.dot(h_lo, wrh_ref[...], preferred_element_type=F32)
              + jnp.dot(h_hi, wrl_ref[...], preferred_element_type=F32)) + br_ref[...]
    lane = lax.broadcasted_iota(jnp.int32, logits.shape, 1)
    work = jnp.where(lane < N_EXPERTS, logits, -jnp.inf)
    idx_out = jnp.zeros(logits.shape, jnp.int32)
    val_out = jnp.zeros(logits.shape, F32)
    v0 = None
    denom = None
    for k in range(TOP_K):
        mx = jnp.max(work, axis=-1, keepdims=True)
        sel = jnp.min(jnp.where(work == mx, lane, LANES), axis=-1, keepdims=True)
        if k == 0:
            v0 = mx
        e = jnp.exp(mx - v0)
        denom = e if k == 0 else denom + e
        idx_out = jnp.where(lane == k, sel, idx_out)
        val_out = jnp.where(lane == k, e, val_out)
        work = jnp.where(lane == sel, -jnp.inf, work)
    idx_ref[...] = idx_out
    wt_ref[...] = val_out / denom


def _outproj(m2, x2, w, g, wrh, wrl, br, *, tm=512):
    t, d = x2.shape
    c2 = lambda shape: pl.BlockSpec(shape, lambda i: (0, 0))
    row = lambda width: pl.BlockSpec((tm, width), lambda i: (i, 0))
    return pl.pallas_call(
        _outproj_kernel,
        grid=(t // tm,),
        in_specs=[row(d), row(d), c2((d, d)), c2((1, d)), c2((d, LANES)), c2((d, LANES)), c2((1, LANES))],
        out_specs=[row(d), pl.BlockSpec((tm, d // LANES, LANES), lambda i: (i, 0, 0)), row(LANES), row(LANES)],
        out_shape=[
            jax.ShapeDtypeStruct((t, d), F32),
            jax.ShapeDtypeStruct((t, d // LANES, LANES), BF16),
            jax.ShapeDtypeStruct((t, LANES), jnp.int32),
            jax.ShapeDtypeStruct((t, LANES), F32),
        ],
        compiler_params=_cparams(("parallel",)),
        name="outproj",
    )(m2, x2, w, g, wrh, wrl, br)


BOTH_DMA_PRIORITIES = (0, 1)


def _start_row_gather(idx_ref, n_rows, src_hbm, dst_ref, sem, priorities=BOTH_DMA_PRIORITIES, first=0):
    n_pri = len(priorities)

    def issue(r2, c):
        for u, pri in enumerate(priorities):
            r = first + r2 * n_pri + u
            pltpu.make_async_copy(src_hbm.at[pl.ds(idx_ref[0, 0, r], 1)], dst_ref.at[pl.ds(r, 1)], sem).start(priority=pri)
        return c

    lax.fori_loop(0, n_rows // n_pri, issue, 0, unroll=8 // n_pri)


def _wait_rows(dst_ref, sem):
    pltpu.make_async_copy(dst_ref, dst_ref, sem).wait()


def _expert_kernel(be_ref, nu_ref, nv_ref, tokc_ref, tokn_ref, h_hbm, wg_ref, bg_ref, wu_ref, bu_ref, wd_ref, bd_ref,
                   o_ref, xbuf_ref, xb_ref, acc_ref, sem, *, tm, nf):
    b = pl.program_id(0)
    f = pl.program_id(1)
    nxt = b + 1
    part = tm // nf

    @pl.when(jnp.logical_and(f == 0, jnp.logical_and(b == 0, nu_ref[0] > 0)))
    def _():
        _start_row_gather(tokc_ref, tm, h_hbm, xbuf_ref.at[0], sem.at[0], priorities=(0,))

    @pl.when(jnp.logical_and(nxt < pl.num_programs(0), nxt < nu_ref[0]))
    def _():
        _start_row_gather(tokn_ref, part, h_hbm, xbuf_ref.at[nxt % 2], sem.at[nxt % 2], priorities=(0,),
                          first=f * part)

    @pl.when(b < nu_ref[0])
    def _():
        @pl.when(f == 0)
        def _():
            _wait_rows(xbuf_ref.at[b % 2], sem.at[b % 2])
            xb_ref[...] = _from_row_tiles(xbuf_ref[b % 2])
            acc_ref[...] = jnp.zeros(acc_ref.shape, F32)

        def ffn(rows):
            xb = xb_ref[:rows]
            gate = jnp.dot(xb, wg_ref[0], preferred_element_type=F32) + bg_ref[0]
            up = jnp.dot(xb, wu_ref[0], preferred_element_type=F32) + bu_ref[0]
            gate = jnp.minimum(gate, SWIGLU_LIMIT)
            up = jnp.clip(up, -SWIGLU_LIMIT, SWIGLU_LIMIT)
            glu = gate * jax.nn.sigmoid(SWIGLU_ALPHA * gate)
            act = ((up + 1.0) * glu).astype(BF16)
            acc_ref[:rows] += jnp.dot(act, wd_ref[0], preferred_element_type=F32)

            @pl.when(f == pl.num_programs(1) - 1)
            def _():
                o_ref[:rows] = _to_row_tiles((acc_ref[:rows] + bd_ref[0]).astype(o_ref.dtype))
                if rows < tm:
                    o_ref[rows:] = jnp.zeros((tm - rows,) + o_ref.shape[1:], o_ref.dtype)

        half_full = nv_ref[b] <= tm // 2

        @pl.when(half_full)
        def _():
            ffn(tm // 2)

        @pl.when(jnp.logical_not(half_full))
        def _():
            ffn(tm)

    @pl.when(jnp.logical_and(b >= nu_ref[0], f == 0))
    def _():
        o_ref[...] = jnp.zeros(o_ref.shape, o_ref.dtype)


def _experts(block_expert, nused, nvalid, slot_tok, h2, wg, bg, wu, bu, wd, bd, *, tm, tf=512):
    n_slots = slot_tok.shape[0]
    tile = h2.shape[1:]
    d = tile[0] * tile[1]
    nb = n_slots // tm
    dff = wg.shape[2]
    nf = dff // tf
    tok3 = slot_tok.reshape(nb, 1, tm)

    def blk(b, nu):
        return jnp.minimum(b, nu[0] - 1)

    def fidx(b, f, nu):
        return jnp.where(b < nu[0], f, nf - 1)

    return pl.pallas_call(
        functools.partial(_expert_kernel, tm=tm, nf=nf),
        grid_spec=pltpu.PrefetchScalarGridSpec(
            num_scalar_prefetch=3,
            grid=(nb, nf),
            in_specs=[
                pl.BlockSpec((1, 1, tm), lambda b, f, be, nu, nv: (b, 0, 0), memory_space=pltpu.SMEM),
                pl.BlockSpec((1, 1, tm), lambda b, f, be, nu, nv: (jnp.minimum(b + 1, nb - 1), 0, 0),
                             memory_space=pltpu.SMEM),
                pl.BlockSpec(memory_space=pl.ANY),
                pl.BlockSpec((1, d, tf), lambda b, f, be, nu, nv: (be[blk(b, nu)], 0, fidx(b, f, nu))),
                pl.BlockSpec((1, 1, tf), lambda b, f, be, nu, nv: (be[blk(b, nu)], 0, fidx(b, f, nu))),
                pl.BlockSpec((1, d, tf), lambda b, f, be, nu, nv: (be[blk(b, nu)], 0, fidx(b, f, nu))),
                pl.BlockSpec((1, 1, tf), lambda b, f, be, nu, nv: (be[blk(b, nu)], 0, fidx(b, f, nu))),
                pl.BlockSpec((1, tf, d), lambda b, f, be, nu, nv: (be[blk(b, nu)], fidx(b, f, nu), 0)),
                pl.BlockSpec((1, 1, d), lambda b, f, be, nu, nv: (be[blk(b, nu)], 0, 0)),
            ],
            out_specs=pl.BlockSpec((tm,) + tile, lambda b, f, be, nu, nv: (b, 0, 0)),
            scratch_shapes=[pltpu.VMEM((2, tm) + tile, h2.dtype), pltpu.VMEM((tm, d), BF16), pltpu.VMEM((tm, d), F32),
                            pltpu.SemaphoreType.DMA((2,))],
        ),
        out_shape=jax.ShapeDtypeStruct((n_slots,) + tile, BF16),
        compiler_params=_cparams(("arbitrary", "arbitrary")),
        name="experts",
    )(block_expert, nused, nvalid, tok3, tok3, h2, wg, bg, wu, bu, wd, bd)


def _combine_kernel(posc_ref, posn_ref, x1_ref, wt_ref, ys_hbm, o_ref, buf_ref, sem, *, tm):
    i = pl.program_id(0)
    nxt = i + 1

    @pl.when(i == 0)
    def _():
        _start_row_gather(posc_ref, TOP_K * tm, ys_hbm, buf_ref.at[0], sem.at[0])

    @pl.when(nxt < pl.num_programs(0))
    def _():
        _start_row_gather(posn_ref, TOP_K * tm, ys_hbm, buf_ref.at[nxt % 2], sem.at[nxt % 2])

    slot = i % 2
    _wait_rows(buf_ref.at[slot], sem.at[slot])
    acc = x1_ref[...]
    for k in range(TOP_K):
        rows = buf_ref[slot, k * tm:(k + 1) * tm]
        acc = acc + wt_ref[:, k:k + 1] * _from_row_tiles(rows).astype(F32)
    o_ref[...] = acc


def _combine(pos_km, x1, wt, ys, *, tm=128):
    t, d = x1.shape
    n = t // tm
    return pl.pallas_call(
        functools.partial(_combine_kernel, tm=tm),
        grid=(n,),
        in_specs=[
            pl.BlockSpec((1, 1, TOP_K * tm), lambda i: (i, 0, 0), memory_space=pltpu.SMEM),
            pl.BlockSpec((1, 1, TOP_K * tm), lambda i: (jnp.minimum(i + 1, n - 1), 0, 0), memory_space=pltpu.SMEM),
            pl.BlockSpec((tm, d), lambda i: (i, 0)),
            pl.BlockSpec((tm, LANES), lambda i: (i, 0)),
            pl.BlockSpec(memory_space=pl.ANY),
        ],
        out_specs=pl.BlockSpec((tm, d), lambda i: (i, 0)),
        out_shape=jax.ShapeDtypeStruct((t, d), F32),
        scratch_shapes=[pltpu.VMEM((2, TOP_K * tm) + ys.shape[1:], ys.dtype), pltpu.SemaphoreType.DMA((2,))],
        compiler_params=_cparams(("arbitrary",)),
        name="combine",
    )(pos_km, pos_km, x1, wt, ys)


def _pad_heads(w, per_head):
    lead = w.shape[:-1]
    w = w.reshape(lead + (N_HEADS, per_head))
    w = jnp.pad(w, [(0, 0)] * len(lead) + [(0, 0), (0, HEAD_PAD - per_head)])
    return w.reshape(lead + (N_HEADS * HEAD_PAD,))


def _s5_discretise(lam_re, lam_im, log_dt, b_re, b_im):
    dt = jnp.exp(log_dt.astype(F32))[:, None]
    lr, li = lam_re.astype(F32), lam_im.astype(F32)
    mag = jnp.exp(lr * dt)
    ar, ai = mag * jnp.cos(li * dt), mag * jnp.sin(li * dt)
    den = lr * lr + li * li
    zr = ((ar - 1.0) * lr + ai * li) / den
    zi = (ai * lr - (ar - 1.0) * li) / den
    br, bi = b_re.astype(F32), b_im.astype(F32)
    bbr = zr[..., None] * br - zi[..., None] * bi
    bbi = zr[..., None] * bi + zi[..., None] * br
    return ar, ai, bbr, bbi


def _s5_pack(ar, ai, bbr, bbi, c_re, c_im, nb):
    eye = jnp.eye(S5_CLUSTER, dtype=F32)

    def pack_b(m):
        m4 = m.reshape(N_CLUSTERS, S5_CLUSTER, S5_STATE, S5_GROUP)
        return jnp.einsum('xgpc,gh->xgchp', m4, eye).reshape(N_CLUSTERS, S5_CLUSTER * S5_GROUP, CLUSTER_STATES)

    def pack_c(m):
        m4 = m.reshape(N_CLUSTERS, S5_CLUSTER, S5_GROUP, S5_STATE)
        return jnp.einsum('xgcp,gh->xgphc', m4, eye).reshape(N_CLUSTERS, CLUSTER_STATES, S5_CLUSTER * S5_GROUP)

    bb = jnp.concatenate([pack_b(bbr), pack_b(bbi)], axis=2).astype(BF16)
    cc = jnp.concatenate([pack_c(c_re.astype(F32)), -pack_c(c_im.astype(F32))], axis=1).astype(BF16)
    a = jnp.concatenate([ar.reshape(N_CLUSTERS, CLUSTER_STATES), ai.reshape(N_CLUSTERS, CLUSTER_STATES)], axis=1)
    a_bc = jnp.broadcast_to(a[:, None, :], (N_CLUSTERS, nb, 2 * CLUSTER_STATES))
    return bb, cc, a_bc


def _perm_matrix(nb):
    n = nb * SUB_T
    p = np.zeros((n, n), np.float32)
    for b in range(nb):
        for t in range(SUB_T):
            p[t * nb + b, b * SUB_T + t] = 1.0
    return p


def kernel(x, positions, norm1_g, w_in, b_gates, q_norm_g, w_uq, kv_norm_g, w_ukv, qk_norm_q_g, qk_norm_k_g, w_o_mla, s5_lambda_re, s5_lambda_im, s5_log_dt, s5_b_re, s5_b_im, s5_c_re, s5_c_im, s5_d, w_glu, w_o_s5, w_out, norm2_g, w_router, b_router, w_gate, b_gate, w_up, b_up, w_down, b_down):
    bsz, seq, d = x.shape
    t = bsz * seq
    depth = norm1_g.shape[0]
    o1 = Q_LORA_RANK
    o2 = o1 + KV_LORA_RANK
    o3 = o2 + QK_ROPE_DIM
    o4 = o3 + S5_WIDTH
    half = ROPE_HALF
    inv_freq = ROPE_THETA ** (-jnp.arange(half, dtype=F32) / half)
    lane = np.arange(LANES)
    freq = jnp.where(lane < QK_ROPE_DIM, jnp.tile(inv_freq, LANES // half), 0.0).reshape(1, LANES).astype(F32)
    sign = jnp.asarray(np.where(lane % QK_ROPE_DIM < half, -1.0, 1.0).reshape(1, LANES), F32)
    pos = positions.reshape(t, 1).astype(F32)
    sm_scale = math.log2(math.e) / math.sqrt(QK_HEAD_DIM)
    ind_q = jnp.asarray(np.equal.outer(np.arange(N_HEADS * HEAD_PAD) // HEAD_PAD, lane), BF16)
    ind_k = jnp.asarray(np.equal.outer(np.arange(N_HEADS * QK_NOPE_DIM) // QK_NOPE_DIM, lane), BF16)
    perm_np = _perm_matrix(bsz)
    perm = jnp.asarray(perm_np, BF16)
    permt = jnp.asarray(perm_np.T, BF16)
    tm_e = 512
    n_assign = t * TOP_K
    nb_e = n_assign // tm_e + N_EXPERTS
    n_slots = nb_e * tm_e
    tm_c = 128

    for l in range(depth):
        wi = w_in[l]
        w_in_p = jnp.concatenate(
            [wi[:, o4:], wi[:, o3:o4], wi[:, :o1], wi[:, o1:o2], wi[:, o2:o3],
             jnp.zeros((d, PROJ_W - wi.shape[1]), wi.dtype)], axis=1).astype(BF16)
        w_uq_p = _pad_heads(w_uq[l], QK_HEAD_DIM).astype(BF16)
        gq_full = _pad_heads(jnp.tile(qk_norm_q_g[l].astype(F32), N_HEADS) * sm_scale, QK_HEAD_DIM).reshape(1, -1)
        wkv = w_ukv[l].reshape(KV_LORA_RANK, N_HEADS, QK_NOPE_DIM + V_HEAD_DIM)
        w_kv_p = jnp.concatenate(
            [wkv[:, :, :QK_NOPE_DIM].reshape(KV_LORA_RANK, -1), wkv[:, :, QK_NOPE_DIM:].reshape(KV_LORA_RANK, -1)],
            axis=1).astype(BF16)
        gk = qk_norm_k_g[l].astype(F32)
        gk_nope = gk[:QK_NOPE_DIM].reshape(1, LANES)
        gk_rope = jnp.pad(gk[QK_NOPE_DIM:], (0, LANES - QK_ROPE_DIM)).reshape(1, LANES)
        ar, ai, bbr, bbi = _s5_discretise(s5_lambda_re[l], s5_lambda_im[l], s5_log_dt[l], s5_b_re[l], s5_b_im[l])
        bb, cc, a_bc = _s5_pack(ar, ai, bbr, bbi, s5_c_re[l], s5_c_im[l], bsz)
        wr = jnp.pad(w_router[l].astype(F32), ((0, 0), (0, LANES - N_EXPERTS)))
        wr_hi = wr.astype(BF16)
        wr_lo = (wr - wr_hi.astype(F32)).astype(BF16)
        br = jnp.pad(b_router[l].astype(F32), (0, LANES - N_EXPERTS)).reshape(1, LANES)

        x2 = x.reshape(t, d)
        proj = _inproj(x2, norm1_g[l].reshape(1, d), w_in_p)
        q = _qprep(proj, pos, q_norm_g[l].reshape(1, -1), w_uq_p, gq_full, ind_q, freq, sign)
        k, v = _kprep(proj, pos, kv_norm_g[l].reshape(1, -1), w_kv_p, gk_nope, gk_rope, ind_k, freq, sign)
        o, wg_b, wu_b, wd_b = _attention(q.reshape(bsz, seq, -1), k.reshape(bsz, seq, -1), v.reshape(bsz, seq, -1),
                                         w_gate[l], w_up[l], w_down[l])
        yb = _s5(proj.reshape(bsz, seq, PROJ_W), perm, permt, bb, cc, a_bc, s5_d[l].reshape(1, -1).astype(F32),
                 w_glu[l].astype(BF16), w_o_s5[l].astype(BF16))
        m = _merge(o.reshape(t, -1), proj, b_gates[l].reshape(1, -1).astype(F32), yb.reshape(t, d),
                   w_o_mla[l].astype(BF16))
        x1, h2, top_idx, top_w = _outproj(m, x2, w_out[l].astype(BF16), norm2_g[l].reshape(1, d), wr_hi, wr_lo, br)

        flat_e = top_idx[:, :TOP_K].reshape(-1)
        onehot = (flat_e[:, None] == jnp.arange(N_EXPERTS, dtype=jnp.int32)[None, :]).astype(jnp.int32)
        csum = jnp.cumsum(onehot, axis=0)
        rank = jnp.take_along_axis(csum, flat_e[:, None], axis=1)[:, 0] - 1
        counts = csum[-1]
        nblk = (counts + tm_e - 1) // tm_e
        blk_end = jnp.cumsum(nblk)
        blk_start = blk_end - nblk
        slot = blk_start[flat_e] * tm_e + rank
        nused = blk_end[-1:].astype(jnp.int32)
        block_expert = jnp.minimum(
            jnp.sum((blk_end[None, :] <= jnp.arange(nb_e, dtype=jnp.int32)[:, None]).astype(jnp.int32), axis=1),
            N_EXPERTS - 1).astype(jnp.int32)
        slot_tok = jnp.zeros((n_slots,), jnp.int32).at[slot].set(
            jnp.arange(n_assign, dtype=jnp.int32) // TOP_K, unique_indices=True, mode='promise_in_bounds')
        pos_km = slot.reshape(t // tm_c, tm_c, TOP_K).transpose(0, 2, 1).reshape(t // tm_c, 1, TOP_K * tm_c)

        blk_ids = jnp.arange(nb_e, dtype=jnp.int32)
        nvalid = jnp.clip(counts[block_expert] - (blk_ids - blk_start[block_expert]) * tm_e, 0, tm_e).astype(jnp.int32)
        ys = _experts(block_expert, nused, nvalid, slot_tok, h2,
                      wg_b, b_gate[l].reshape(N_EXPERTS, 1, -1).astype(F32),
                      wu_b, b_up[l].reshape(N_EXPERTS, 1, -1).astype(F32),
                      wd_b, b_down[l].reshape(N_EXPERTS, 1, -1).astype(F32), tm=tm_e)
        x = _combine(pos_km, x1, top_w, ys, tm=tm_c).reshape(bsz, seq, d)
    return x
```

```python
import functools
import math

import numpy as np
import jax
import jax.numpy as jnp
from jax import lax
from jax.experimental import pallas as pl
from jax.experimental.pallas import tpu as pltpu

F32 = jnp.float32
BF16 = jnp.bfloat16

D_MODEL = 2048
CHUNK = 64
EPS = 1e-6
N_HEADS = 16
QK_NOPE_DIM = 128
QK_ROPE_DIM = 64
QK_HEAD_DIM = QK_NOPE_DIM + QK_ROPE_DIM
V_HEAD_DIM = 128
Q_LORA_RANK = 512
KV_LORA_RANK = 256
ROPE_THETA = 10000.0
S5_WIDTH = 1024
S5_GROUP = 16
S5_GROUPS = S5_WIDTH // S5_GROUP
S5_STATE = 64
N_EXPERTS = 32
TOP_K = 4
D_FF = 2048
SWIGLU_LIMIT = 7.0
SWIGLU_ALPHA = 1.702

LANES = 128
HEAD_PAD = 256
ROPE_HALF = QK_ROPE_DIM // 2
S5_CLUSTER = 8
N_CLUSTERS = S5_GROUPS // S5_CLUSTER
CLUSTER_STATES = S5_CLUSTER * S5_STATE
SUB_T = 32
PROJ_W = 6144
VMEM_LIMIT = 56 * 1024 * 1024
NEG = -1e30
DENOM_ROWS = 8


def _cparams(sem, vmem=VMEM_LIMIT, **kw):
    return pltpu.CompilerParams(dimension_semantics=sem, vmem_limit_bytes=vmem, **kw)


def _inproj_kernel(x_ref, g_ref, w_ref, o_ref, h_ref):
    @pl.when(pl.program_id(1) == 0)
    def _():
        x = x_ref[...]
        ms = jnp.mean(x * x, axis=-1, keepdims=True)
        h_ref[...] = (x * lax.rsqrt(ms + EPS) * g_ref[...]).astype(BF16)

    o_ref[...] = jnp.dot(h_ref[...], w_ref[...], preferred_element_type=F32).astype(o_ref.dtype)


def _inproj(x2, g, w, *, tm=1024, tn=1536):
    t, d = x2.shape
    n = w.shape[1]
    return pl.pallas_call(
        _inproj_kernel,
        grid=(t // tm, n // tn),
        in_specs=[
            pl.BlockSpec((tm, d), lambda i, j: (i, 0)),
            pl.BlockSpec((1, d), lambda i, j: (0, 0)),
            pl.BlockSpec((d, tn), lambda i, j: (0, j)),
        ],
        out_specs=pl.BlockSpec((tm, tn), lambda i, j: (i, j)),
        out_shape=jax.ShapeDtypeStruct((t, n), BF16),
        scratch_shapes=[pltpu.VMEM((tm, d), BF16)],
        compiler_params=_cparams(("parallel", "arbitrary")),
        name="inproj",
    )(x2, g, w)


def _rope_mid(mid, cos, sin_signed):
    lane = lax.broadcasted_iota(jnp.int32, mid.shape, 1)
    rot = jnp.where(lane < ROPE_HALF, pltpu.roll(mid, LANES - ROPE_HALF, 1), pltpu.roll(mid, ROPE_HALF, 1))
    return mid * cos + rot * sin_signed


def _qprep_kernel(ql_ref, pos_ref, gn_ref, w_ref, gfull_ref, ind_ref, freq_ref, sign_ref, o_ref):
    ql = ql_ref[...].astype(F32)
    ms = jnp.mean(ql * ql, axis=-1, keepdims=True)
    qn = (ql * lax.rsqrt(ms + EPS) * gn_ref[...]).astype(BF16)
    q = jnp.dot(qn, w_ref[...], preferred_element_type=F32)
    ssq = jnp.dot((q * q).astype(BF16), ind_ref[...], preferred_element_type=F32)
    sc = lax.rsqrt(ssq * (1.0 / QK_HEAD_DIM) + EPS)
    ang = pos_ref[...] * freq_ref[...]
    cos = jnp.cos(ang)
    sin_signed = jnp.sin(ang) * sign_ref[...]
    for h in range(N_HEADS):
        s_h = sc[:, h:h + 1]
        lo = h * HEAD_PAD
        nope = q[:, lo:lo + LANES] * s_h * gfull_ref[:, lo:lo + LANES]
        mid = q[:, lo + LANES:lo + HEAD_PAD] * s_h * gfull_ref[:, lo + LANES:lo + HEAD_PAD]
        o_ref[:, lo:lo + LANES] = nope.astype(o_ref.dtype)
        o_ref[:, lo + LANES:lo + HEAD_PAD] = _rope_mid(mid, cos, sin_signed).astype(o_ref.dtype)


def _qprep(proj, pos, gn, w, gfull, ind, freq, sign, *, tm=512):
    t = proj.shape[0]
    n = N_HEADS * HEAD_PAD
    full = lambda shape: pl.BlockSpec(shape, lambda i: (0, 0))
    return pl.pallas_call(
        _qprep_kernel,
        grid=(t // tm,),
        in_specs=[
            pl.BlockSpec((tm, Q_LORA_RANK), lambda i: (i, 10)),
            pl.BlockSpec((tm, 1), lambda i: (i, 0)),
            full((1, Q_LORA_RANK)),
            full((Q_LORA_RANK, n)),
            full((1, n)),
            full((n, LANES)),
            full((1, LANES)),
            full((1, LANES)),
        ],
        out_specs=pl.BlockSpec((tm, n), lambda i: (i, 0)),
        out_shape=jax.ShapeDtypeStruct((t, n), BF16),
        compiler_params=_cparams(("parallel",)),
        name="qprep",
    )(proj, pos, gn, w, gfull, ind, freq, sign)


def _kprep_kernel(kvl_ref, kr_ref, pos_ref, gn_ref, w_ref, gnope_ref, grope_ref, ind_ref, freq_ref, sign_ref,
                  k_ref, v_ref):
    kvl = kvl_ref[...].astype(F32)
    ms = jnp.mean(kvl * kvl, axis=-1, keepdims=True)
    kn = (kvl * lax.rsqrt(ms + EPS) * gn_ref[...]).astype(BF16)
    kv = jnp.dot(kn, w_ref[...], preferred_element_type=F32)
    nd = N_HEADS * QK_NOPE_DIM
    knope = kv[:, :nd]
    v_ref[...] = kv[:, nd:].astype(v_ref.dtype)
    kr = kr_ref[...].astype(F32)
    ssq = jnp.dot((knope * knope).astype(BF16), ind_ref[...], preferred_element_type=F32)
    ssq = ssq + jnp.sum(kr * kr, axis=-1, keepdims=True)
    sc = lax.rsqrt(ssq * (1.0 / QK_HEAD_DIM) + EPS)
    ang = pos_ref[...] * freq_ref[...]
    kr_rot = _rope_mid(kr * grope_ref[...], jnp.cos(ang), jnp.sin(ang) * sign_ref[...])
    for h in range(N_HEADS):
        s_h = sc[:, h:h + 1]
        lo = h * HEAD_PAD
        nope = knope[:, h * LANES:(h + 1) * LANES] * s_h * gnope_ref[...]
        k_ref[:, lo:lo + LANES] = nope.astype(k_ref.dtype)
        k_ref[:, lo + LANES:lo + HEAD_PAD] = (kr_rot * s_h).astype(k_ref.dtype)


def _kprep(proj, pos, gn, w, gnope, grope, ind, freq, sign, *, tm=512):
    t = proj.shape[0]
    n = N_HEADS * HEAD_PAD
    nd = N_HEADS * QK_NOPE_DIM
    full = lambda shape: pl.BlockSpec(shape, lambda i: (0, 0))
    return pl.pallas_call(
        _kprep_kernel,
        grid=(t // tm,),
        in_specs=[
            pl.BlockSpec((tm, KV_LORA_RANK), lambda i: (i, 22)),
            pl.BlockSpec((tm, LANES), lambda i: (i, 46)),
            pl.BlockSpec((tm, 1), lambda i: (i, 0)),
            full((1, KV_LORA_RANK)),
            full((KV_LORA_RANK, 2 * nd)),
            full((1, LANES)),
            full((1, LANES)),
            full((nd, LANES)),
            full((1, LANES)),
            full((1, LANES)),
        ],
        out_specs=[pl.BlockSpec((tm, n), lambda i: (i, 0)), pl.BlockSpec((tm, nd), lambda i: (i, 0))],
        out_shape=[jax.ShapeDtypeStruct((t, n), BF16), jax.ShapeDtypeStruct((t, nd), BF16)],
        compiler_params=_cparams(("parallel",)),
        name="kprep",
    )(proj, proj, pos, gn, w, gnope, grope, ind, freq, sign)


def _attn_kernel(q_ref, k_ref, v_ref, wg_ref, wu_ref, wd_ref, o_ref, wgo_ref, wuo_ref, wdo_ref,
                 m_ref, acc_ref, *, tq, hg):
    wgo_ref[...] = wg_ref[...].astype(wgo_ref.dtype)
    wuo_ref[...] = wu_ref[...].astype(wuo_ref.dtype)
    wdo_ref[...] = wd_ref[...].astype(wdo_ref.dtype)

    qi = pl.program_id(2)
    m_ref[...] = jnp.full(m_ref.shape, NEG, F32)
    acc_ref[...] = jnp.zeros(acc_ref.shape, F32)
    ones = jnp.ones((tq, DENOM_ROWS), BF16)

    def block(j, masked):
        r0 = pl.multiple_of(j * tq, tq)
        for h in range(hg):
            q = q_ref[:, h * HEAD_PAD:(h + 1) * HEAD_PAD]
            kb = k_ref[pl.ds(r0, tq), h * HEAD_PAD:(h + 1) * HEAD_PAD]
            vb = v_ref[pl.ds(r0, tq), h * V_HEAD_DIM:(h + 1) * V_HEAD_DIM]
            st = lax.dot_general(kb, q, (((1,), (1,)), ((), ())), preferred_element_type=F32)
            if masked:
                key = lax.broadcasted_iota(jnp.int32, st.shape, 0) // CHUNK
                qry = lax.broadcasted_iota(jnp.int32, st.shape, 1) // CHUNK
                st = jnp.where(key <= qry, st, NEG)
            m_old = m_ref[h]
            m_new = jnp.maximum(m_old, jnp.max(st, axis=0, keepdims=True))
            alpha = jnp.exp2(m_old - m_new)
            p = jnp.exp2((st - m_new).astype(BF16))
            v_ext = jnp.concatenate([vb, ones], axis=1)
            pv = lax.dot_general(v_ext, p, (((0,), (0,)), ((), ())), preferred_element_type=F32)
            acc_ref[h] = alpha * acc_ref[h] + pv
            m_ref[h] = m_new

    def body(j, c):
        block(j, False)
        return c

    lax.fori_loop(0, qi, body, 0)
    block(qi, True)
    for h in range(hg):
        o = acc_ref[h, :V_HEAD_DIM] / acc_ref[h, V_HEAD_DIM:V_HEAD_DIM + 1]
        o_ref[:, h * V_HEAD_DIM:(h + 1) * V_HEAD_DIM] = o.T.astype(o_ref.dtype)


def _attention(q3, k3, v3, wg, wu, wd, *, tq=1024, hg=2):
    b, s, _ = q3.shape
    tq = min(tq, s)
    ng, nq = N_HEADS // hg, s // tq
    steps = b * ng * nq
    w2 = [w.reshape(-1, w.shape[-1]) for w in (wg, wu, wd)]
    slab = [w.shape[0] // steps for w in w2]
    step = lambda bi, h, i: ((bi * ng + h) * nq + i, 0)
    wspec = [pl.BlockSpec((r, w.shape[1]), step) for r, w in zip(slab, w2)]
    outs = pl.pallas_call(
        functools.partial(_attn_kernel, tq=tq, hg=hg),
        grid=(b, ng, nq),
        in_specs=[
            pl.BlockSpec((None, tq, hg * HEAD_PAD), lambda bi, h, i: (bi, i, h)),
            pl.BlockSpec((None, s, hg * HEAD_PAD), lambda bi, h, i: (bi, 0, h)),
            pl.BlockSpec((None, s, hg * V_HEAD_DIM), lambda bi, h, i: (bi, 0, h)),
        ] + wspec,
        out_specs=[pl.BlockSpec((None, tq, hg * V_HEAD_DIM), lambda bi, h, i: (bi, i, h))] + wspec,
        out_shape=[jax.ShapeDtypeStruct((b, s, N_HEADS * V_HEAD_DIM), BF16)]
        + [jax.ShapeDtypeStruct(w.shape, BF16) for w in w2],
        scratch_shapes=[pltpu.VMEM((hg, 1, tq), F32), pltpu.VMEM((hg, V_HEAD_DIM + DENOM_ROWS, tq), F32)],
        compiler_params=_cparams(("parallel", "parallel", "arbitrary")),
        name="attention",
    )(q3, k3, v3, *w2)
    return (outs[0],) + tuple(o.reshape(w.shape) for o, w in zip(outs[1:], (wg, wu, wd)))


def _s5_kernel(u_ref, perm_ref, permt_ref, bb_ref, cc_ref, a_ref, d_ref, wglu_ref, wo_ref, o_ref,
               ut_ref, bu_ref, y_ref, carry_ref, *, tc):
    nb = u_ref.shape[0]
    rows_sub = nb * SUB_T
    nsub = tc // SUB_T

    @pl.when(pl.program_id(0) == 0)
    def _():
        carry_ref[...] = jnp.zeros(carry_ref.shape, F32)

    for j in range(nsub):
        ub = u_ref[:, j * SUB_T:(j + 1) * SUB_T, :].reshape(rows_sub, S5_WIDTH)
        ut_ref[j * rows_sub:(j + 1) * rows_sub, :] = jnp.dot(
            perm_ref[...], ub, preferred_element_type=F32).astype(BF16)

    for c in range(N_CLUSTERS):
        bu_ref[...] = jnp.dot(ut_ref[:, c * LANES:(c + 1) * LANES], bb_ref[c], preferred_element_type=F32)
        ar = a_ref[c, :, :CLUSTER_STATES]
        ai = a_ref[c, :, CLUSTER_STATES:]

        def step(t, carry):
            xr, xi = carry
            r0 = pl.multiple_of(t * nb, nb)
            br = bu_ref[pl.ds(r0, nb), :CLUSTER_STATES]
            bi = bu_ref[pl.ds(r0, nb), CLUSTER_STATES:]
            nxr = ar * xr - ai * xi + br
            nxi = ar * xi + ai * xr + bi
            bu_ref[pl.ds(r0, nb), :CLUSTER_STATES] = nxr
            bu_ref[pl.ds(r0, nb), CLUSTER_STATES:] = nxi
            return nxr, nxi

        xr, xi = lax.fori_loop(0, tc, step, (carry_ref[c, :, :CLUSTER_STATES], carry_ref[c, :, CLUSTER_STATES:]),
                               unroll=8)
        carry_ref[c, :, :CLUSTER_STATES] = xr
        carry_ref[c, :, CLUSTER_STATES:] = xi
        y_ref[:, c * LANES:(c + 1) * LANES] = jnp.dot(bu_ref[...].astype(BF16), cc_ref[c], preferred_element_type=F32)

    y = y_ref[...] + d_ref[...] * ut_ref[...].astype(F32)
    g = jax.nn.gelu(y)
    z = g * jax.nn.sigmoid(jnp.dot(g.astype(BF16), wglu_ref[...], preferred_element_type=F32))
    yb = jnp.dot(z.astype(BF16), wo_ref[...], preferred_element_type=F32).astype(BF16)
    for j in range(nsub):
        blk = jnp.dot(permt_ref[...], yb[j * rows_sub:(j + 1) * rows_sub, :], preferred_element_type=F32)
        o_ref[:, j * SUB_T:(j + 1) * SUB_T, :] = blk.astype(o_ref.dtype).reshape(nb, SUB_T, o_ref.shape[2])


def _s5(proj3, perm, permt, bb, cc, a_bc, d, wglu, wo, *, tc=128):
    b, s, _ = proj3.shape
    rows = b * tc
    dm = wo.shape[1]
    c2 = lambda shape: pl.BlockSpec(shape, lambda i: (0, 0))
    c3 = lambda shape: pl.BlockSpec(shape, lambda i: (0, 0, 0))
    return pl.pallas_call(
        functools.partial(_s5_kernel, tc=tc),
        grid=(s // tc,),
        in_specs=[
            pl.BlockSpec((b, tc, S5_WIDTH), lambda i: (0, i, 4)),
            c2(perm.shape), c2(permt.shape), c3(bb.shape), c3(cc.shape), c3(a_bc.shape), c2(d.shape),
            c2(wglu.shape), c2(wo.shape),
        ],
        out_specs=pl.BlockSpec((b, tc, dm), lambda i: (0, i, 0)),
        out_shape=jax.ShapeDtypeStruct((b, s, dm), BF16),
        scratch_shapes=[
            pltpu.VMEM((rows, S5_WIDTH), BF16),
            pltpu.VMEM((rows, 2 * CLUSTER_STATES), F32),
            pltpu.VMEM((rows, S5_WIDTH), F32),
            pltpu.VMEM((N_CLUSTERS, b, 2 * CLUSTER_STATES), F32),
        ],
        compiler_params=_cparams(("arbitrary",)),
        name="s5",
    )(proj3, perm, permt, bb, cc, a_bc, d, wglu, wo)


def _merge_kernel(o_ref, gl_ref, bg_ref, yb_ref, w_ref, m_ref):
    d = o_ref.shape[1]
    ya = jnp.dot(o_ref[...], w_ref[...], preferred_element_type=F32)
    ga = jax.nn.sigmoid(gl_ref[:, :d].astype(F32) + bg_ref[:, :d])
    gb = jax.nn.sigmoid(gl_ref[:, d:].astype(F32) + bg_ref[:, d:])
    m_ref[...] = (ga * ya + gb * yb_ref[...].astype(F32)).astype(m_ref.dtype)


def _merge(o2, proj, bg, yb2, w, *, tm=512):
    t, d = o2.shape
    return pl.pallas_call(
        _merge_kernel,
        grid=(t // tm,),
        in_specs=[
            pl.BlockSpec((tm, d), lambda i: (i, 0)),
            pl.BlockSpec((tm, 2 * d), lambda i: (i, 0)),
            pl.BlockSpec((1, 2 * d), lambda i: (0, 0)),
            pl.BlockSpec((tm, d), lambda i: (i, 0)),
            pl.BlockSpec((d, d), lambda i: (0, 0)),
        ],
        out_specs=pl.BlockSpec((tm, d), lambda i: (i, 0)),
        out_shape=jax.ShapeDtypeStruct((t, d), BF16),
        compiler_params=_cparams(("parallel",)),
        name="merge",
    )(o2, proj, bg, yb2, w)


def _to_row_tiles(x):
    return x.reshape(x.shape[0], x.shape[1] // LANES, LANES)


def _from_row_tiles(x):
    return x.reshape(x.shape[0], x.shape[1] * LANES)


def _outproj_kernel(m_ref, x_ref, w_ref, g_ref, wrh_ref, wrl_ref, br_ref, x1_ref, h2_ref, idx_ref, wt_ref):
    x1 = x_ref[...] + jnp.dot(m_ref[...], w_ref[...], preferred_element_type=F32)
    x1_ref[...] = x1
    ms = jnp.mean(x1 * x1, axis=-1, keepdims=True)
    h2 = x1 * lax.rsqrt(ms + EPS) * g_ref[...]
    h_hi = h2.astype(BF16)
    h2_ref[...] = _to_row_tiles(h_hi)
    h_lo = (h2 - h_hi.astype(F32)).astype(BF16)
    logits = (jnp.dot(h_hi, wrh_ref[...], preferred_element_type=F32)
              + jnp.dot(h_lo, wrh_ref[...], preferred_element_type=F32)
              + jnp.dot(h_hi, wrl_ref[...], preferred_element_type=F32)) + br_ref[...]
    lane = lax.broadcasted_iota(jnp.int32, logits.shape, 1)
    work = jnp.where(lane < N_EXPERTS, logits, -jnp.inf)
    idx_out = jnp.zeros(logits.shape, jnp.int32)
    val_out = jnp.zeros(logits.shape, F32)
    v0 = None
    denom = None
    for k in range(TOP_K):
        mx = jnp.max(work, axis=-1, keepdims=True)
        sel = jnp.min(jnp.where(work == mx, lane, LANES), axis=-1, keepdims=True)
        if k == 0:
            v0 = mx
        e = jnp.exp(mx - v0)
        denom = e if k == 0 else denom + e
        idx_out = jnp.where(lane == k, sel, idx_out)
        val_out = jnp.where(lane == k, e, val_out)
        work = jnp.where(lane == sel, -jnp.inf, work)
    idx_ref[...] = idx_out
    wt_ref[...] = val_out / denom


def _outproj(m2, x2, w, g, wrh, wrl, br, *, tm=512):
    t, d = x2.shape
    c2 = lambda shape: pl.BlockSpec(shape, lambda i: (0, 0))
    row = lambda width: pl.BlockSpec((tm, width), lambda i: (i, 0))
    return pl.pallas_call(
        _outproj_kernel,
        grid=(t // tm,),
        in_specs=[row(d), row(d), c2((d, d)), c2((1, d)), c2((d, LANES)), c2((d, LANES)), c2((1, LANES))],
        out_specs=[row(d), pl.BlockSpec((tm, d // LANES, LANES), lambda i: (i, 0, 0)), row(LANES), row(LANES)],
        out_shape=[
            jax.ShapeDtypeStruct((t, d), F32),
            jax.ShapeDtypeStruct((t, d // LANES, LANES), BF16),
            jax.ShapeDtypeStruct((t, LANES), jnp.int32),
            jax.ShapeDtypeStruct((t, LANES), F32),
        ],
        compiler_params=_cparams(("parallel",)),
        name="outproj",
    )(m2, x2, w, g, wrh, wrl, br)


BOTH_DMA_PRIORITIES = (0, 1)


def _start_row_gather(idx_ref, n_rows, src_hbm, dst_ref, sem, priorities=BOTH_DMA_PRIORITIES):
    n_pri = len(priorities)

    def issue(r2, c):
        for u, pri in enumerate(priorities):
            r = r2 * n_pri + u
            pltpu.make_async_copy(src_hbm.at[pl.ds(idx_ref[0, 0, r], 1)], dst_ref.at[pl.ds(r, 1)], sem).start(priority=pri)
        return c

    lax.fori_loop(0, n_rows // n_pri, issue, 0, unroll=8 // n_pri)


def _wait_rows(dst_ref, sem):
    pltpu.make_async_copy(dst_ref, dst_ref, sem).wait()


def _expert_kernel(be_ref, nu_ref, nv_ref, tokc_ref, tokn_ref, h_hbm, wg_ref, bg_ref, wu_ref, bu_ref, wd_ref, bd_ref,
                   o_ref, xbuf_ref, xb_ref, acc_ref, sem, *, tm):
    b = pl.program_id(0)
    f = pl.program_id(1)
    nxt = b + 1

    @pl.when(jnp.logical_and(f == 0, jnp.logical_and(b == 0, nu_ref[0] > 0)))
    def _():
        _start_row_gather(tokc_ref, tm, h_hbm, xbuf_ref.at[0], sem.at[0], priorities=(0,))

    @pl.when(jnp.logical_and(f == 0, jnp.logical_and(nxt < pl.num_programs(0), nxt < nu_ref[0])))
    def _():
        _start_row_gather(tokn_ref, tm, h_hbm, xbuf_ref.at[nxt % 2], sem.at[nxt % 2], priorities=(0,))

    @pl.when(b < nu_ref[0])
    def _():
        @pl.when(f == 0)
        def _():
            _wait_rows(xbuf_ref.at[b % 2], sem.at[b % 2])
            xb_ref[...] = _from_row_tiles(xbuf_ref[b % 2])
            acc_ref[...] = jnp.zeros(acc_ref.shape, F32)

        def ffn(rows):
            xb = xb_ref[:rows]
            gate = jnp.dot(xb, wg_ref[0], preferred_element_type=F32) + bg_ref[0]
            up = jnp.dot(xb, wu_ref[0], preferred_element_type=F32) + bu_ref[0]
            gate = jnp.minimum(gate, SWIGLU_LIMIT)
            up = jnp.clip(up, -SWIGLU_LIMIT, SWIGLU_LIMIT)
            glu = gate * jax.nn.sigmoid(SWIGLU_ALPHA * gate)
            act = ((up + 1.0) * glu).astype(BF16)
            acc_ref[:rows] += jnp.dot(act, wd_ref[0], preferred_element_type=F32)

            @pl.when(f == pl.num_programs(1) - 1)
            def _():
                o_ref[:rows] = _to_row_tiles((acc_ref[:rows] + bd_ref[0]).astype(o_ref.dtype))
                if rows < tm:
                    o_ref[rows:] = jnp.zeros((tm - rows,) + o_ref.shape[1:], o_ref.dtype)

        half_full = nv_ref[b] <= tm // 2

        @pl.when(half_full)
        def _():
            ffn(tm // 2)

        @pl.when(jnp.logical_not(half_full))
        def _():
            ffn(tm)

    @pl.when(jnp.logical_and(b >= nu_ref[0], f == 0))
    def _():
        o_ref[...] = jnp.zeros(o_ref.shape, o_ref.dtype)


def _experts(block_expert, nused, nvalid, slot_tok, h2, wg, bg, wu, bu, wd, bd, *, tm, tf=1024):
    n_slots = slot_tok.shape[0]
    tile = h2.shape[1:]
    d = tile[0] * tile[1]
    nb = n_slots // tm
    dff = wg.shape[2]
    nf = dff // tf
    tok3 = slot_tok.reshape(nb, 1, tm)

    def blk(b, nu):
        return jnp.minimum(b, nu[0] - 1)

    def fidx(b, f, nu):
        return jnp.where(b < nu[0], f, nf - 1)

    return pl.pallas_call(
        functools.partial(_expert_kernel, tm=tm),
        grid_spec=pltpu.PrefetchScalarGridSpec(
            num_scalar_prefetch=3,
            grid=(nb, nf),
            in_specs=[
                pl.BlockSpec((1, 1, tm), lambda b, f, be, nu, nv: (b, 0, 0), memory_space=pltpu.SMEM),
                pl.BlockSpec((1, 1, tm), lambda b, f, be, nu, nv: (jnp.minimum(b + 1, nb - 1), 0, 0),
                             memory_space=pltpu.SMEM),
                pl.BlockSpec(memory_space=pl.ANY),
                pl.BlockSpec((1, d, tf), lambda b, f, be, nu, nv: (be[blk(b, nu)], 0, fidx(b, f, nu))),
                pl.BlockSpec((1, 1, tf), lambda b, f, be, nu, nv: (be[blk(b, nu)], 0, fidx(b, f, nu))),
                pl.BlockSpec((1, d, tf), lambda b, f, be, nu, nv: (be[blk(b, nu)], 0, fidx(b, f, nu))),
                pl.BlockSpec((1, 1, tf), lambda b, f, be, nu, nv: (be[blk(b, nu)], 0, fidx(b, f, nu))),
                pl.BlockSpec((1, tf, d), lambda b, f, be, nu, nv: (be[blk(b, nu)], fidx(b, f, nu), 0)),
                pl.BlockSpec((1, 1, d), lambda b, f, be, nu, nv: (be[blk(b, nu)], 0, 0)),
            ],
            out_specs=pl.BlockSpec((tm,) + tile, lambda b, f, be, nu, nv: (b, 0, 0)),
            scratch_shapes=[pltpu.VMEM((2, tm) + tile, h2.dtype), pltpu.VMEM((tm, d), BF16), pltpu.VMEM((tm, d), F32),
                            pltpu.SemaphoreType.DMA((2,))],
        ),
        out_shape=jax.ShapeDtypeStruct((n_slots,) + tile, BF16),
        compiler_params=_cparams(("arbitrary", "arbitrary")),
        name="experts",
    )(block_expert, nused, nvalid, tok3, tok3, h2, wg, bg, wu, bu, wd, bd)


def _combine_kernel(posc_ref, posn_ref, x1_ref, wt_ref, ys_hbm, o_ref, buf_ref, sem, *, tm):
    i = pl.program_id(0)
    nxt = i + 1

    @pl.when(i == 0)
    def _():
        _start_row_gather(posc_ref, TOP_K * tm, ys_hbm, buf_ref.at[0], sem.at[0])

    @pl.when(nxt < pl.num_programs(0))
    def _():
        _start_row_gather(posn_ref, TOP_K * tm, ys_hbm, buf_ref.at[nxt % 2], sem.at[nxt % 2])

    slot = i % 2
    _wait_rows(buf_ref.at[slot], sem.at[slot])
    acc = x1_ref[...]
    for k in range(TOP_K):
        rows = buf_ref[slot, k * tm:(k + 1) * tm]
        acc = acc + wt_ref[:, k:k + 1] * _from_row_tiles(rows).astype(F32)
    o_ref[...] = acc


def _combine(pos_km, x1, wt, ys, *, tm=128):
    t, d = x1.shape
    n = t // tm
    return pl.pallas_call(
        functools.partial(_combine_kernel, tm=tm),
        grid=(n,),
        in_specs=[
            pl.BlockSpec((1, 1, TOP_K * tm), lambda i: (i, 0, 0), memory_space=pltpu.SMEM),
            pl.BlockSpec((1, 1, TOP_K * tm), lambda i: (jnp.minimum(i + 1, n - 1), 0, 0), memory_space=pltpu.SMEM),
            pl.BlockSpec((tm, d), lambda i: (i, 0)),
            pl.BlockSpec((tm, LANES), lambda i: (i, 0)),
            pl.BlockSpec(memory_space=pl.ANY),
        ],
        out_specs=pl.BlockSpec((tm, d), lambda i: (i, 0)),
        out_shape=jax.ShapeDtypeStruct((t, d), F32),
        scratch_shapes=[pltpu.VMEM((2, TOP_K * tm) + ys.shape[1:], ys.dtype), pltpu.SemaphoreType.DMA((2,))],
        compiler_params=_cparams(("arbitrary",)),
        name="combine",
    )(pos_km, pos_km, x1, wt, ys)


def _pad_heads(w, per_head):
    lead = w.shape[:-1]
    w = w.reshape(lead + (N_HEADS, per_head))
    w = jnp.pad(w, [(0, 0)] * len(lead) + [(0, 0), (0, HEAD_PAD - per_head)])
    return w.reshape(lead + (N_HEADS * HEAD_PAD,))


def _s5_discretise(lam_re, lam_im, log_dt, b_re, b_im):
    dt = jnp.exp(log_dt.astype(F32))[:, None]
    lr, li = lam_re.astype(F32), lam_im.astype(F32)
    mag = jnp.exp(lr * dt)
    ar, ai = mag * jnp.cos(li * dt), mag * jnp.sin(li * dt)
    den = lr * lr + li * li
    zr = ((ar - 1.0) * lr + ai * li) / den
    zi = (ai * lr - (ar - 1.0) * li) / den
    br, bi = b_re.astype(F32), b_im.astype(F32)
    bbr = zr[..., None] * br - zi[..., None] * bi
    bbi = zr[..., None] * bi + zi[..., None] * br
    return ar, ai, bbr, bbi


def _s5_pack(ar, ai, bbr, bbi, c_re, c_im, nb):
    eye = jnp.eye(S5_CLUSTER, dtype=F32)

    def pack_b(m):
        m4 = m.reshape(N_CLUSTERS, S5_CLUSTER, S5_STATE, S5_GROUP)
        return jnp.einsum('xgpc,gh->xgchp', m4, eye).reshape(N_CLUSTERS, S5_CLUSTER * S5_GROUP, CLUSTER_STATES)

    def pack_c(m):
        m4 = m.reshape(N_CLUSTERS, S5_CLUSTER, S5_GROUP, S5_STATE)
        return jnp.einsum('xgcp,gh->xgphc', m4, eye).reshape(N_CLUSTERS, CLUSTER_STATES, S5_CLUSTER * S5_GROUP)

    bb = jnp.concatenate([pack_b(bbr), pack_b(bbi)], axis=2).astype(BF16)
    cc = jnp.concatenate([pack_c(c_re.astype(F32)), -pack_c(c_im.astype(F32))], axis=1).astype(BF16)
    a = jnp.concatenate([ar.reshape(N_CLUSTERS, CLUSTER_STATES), ai.reshape(N_CLUSTERS, CLUSTER_STATES)], axis=1)
    a_bc = jnp.broadcast_to(a[:, None, :], (N_CLUSTERS, nb, 2 * CLUSTER_STATES))
    return bb, cc, a_bc


def _perm_matrix(nb):
    n = nb * SUB_T
    p = np.zeros((n, n), np.float32)
    for b in range(nb):
        for t in range(SUB_T):
            p[t * nb + b, b * SUB_T + t] = 1.0
    return p


def kernel(x, positions, norm1_g, w_in, b_gates, q_norm_g, w_uq, kv_norm_g, w_ukv, qk_norm_q_g, qk_norm_k_g, w_o_mla, s5_lambda_re, s5_lambda_im, s5_log_dt, s5_b_re, s5_b_im, s5_c_re, s5_c_im, s5_d, w_glu, w_o_s5, w_out, norm2_g, w_router, b_router, w_gate, b_gate, w_up, b_up, w_down, b_down):
    bsz, seq, d = x.shape
    t = bsz * seq
    depth = norm1_g.shape[0]
    o1 = Q_LORA_RANK
    o2 = o1 + KV_LORA_RANK
    o3 = o2 + QK_ROPE_DIM
    o4 = o3 + S5_WIDTH
    half = ROPE_HALF
    inv_freq = ROPE_THETA ** (-jnp.arange(half, dtype=F32) / half)
    lane = np.arange(LANES)
    freq = jnp.where(lane < QK_ROPE_DIM, jnp.tile(inv_freq, LANES // half), 0.0).reshape(1, LANES).astype(F32)
    sign = jnp.asarray(np.where(lane % QK_ROPE_DIM < half, -1.0, 1.0).reshape(1, LANES), F32)
    pos = positions.reshape(t, 1).astype(F32)
    sm_scale = math.log2(math.e) / math.sqrt(QK_HEAD_DIM)
    ind_q = jnp.asarray(np.equal.outer(np.arange(N_HEADS * HEAD_PAD) // HEAD_PAD, lane), BF16)
    ind_k = jnp.asarray(np.equal.outer(np.arange(N_HEADS * QK_NOPE_DIM) // QK_NOPE_DIM, lane), BF16)
    perm_np = _perm_matrix(bsz)
    perm = jnp.asarray(perm_np, BF16)
    permt = jnp.asarray(perm_np.T, BF16)
    tm_e = 512
    n_assign = t * TOP_K
    nb_e = n_assign // tm_e + N_EXPERTS
    n_slots = nb_e * tm_e
    tm_c = 128

    for l in range(depth):
        wi = w_in[l]
        w_in_p = jnp.concatenate(
            [wi[:, o4:], wi[:, o3:o4], wi[:, :o1], wi[:, o1:o2], wi[:, o2:o3],
             jnp.zeros((d, PROJ_W - wi.shape[1]), wi.dtype)], axis=1).astype(BF16)
        w_uq_p = _pad_heads(w_uq[l], QK_HEAD_DIM).astype(BF16)
        gq_full = _pad_heads(jnp.tile(qk_norm_q_g[l].astype(F32), N_HEADS) * sm_scale, QK_HEAD_DIM).reshape(1, -1)
        wkv = w_ukv[l].reshape(KV_LORA_RANK, N_HEADS, QK_NOPE_DIM + V_HEAD_DIM)
        w_kv_p = jnp.concatenate(
            [wkv[:, :, :QK_NOPE_DIM].reshape(KV_LORA_RANK, -1), wkv[:, :, QK_NOPE_DIM:].reshape(KV_LORA_RANK, -1)],
            axis=1).astype(BF16)
        gk = qk_norm_k_g[l].astype(F32)
        gk_nope = gk[:QK_NOPE_DIM].reshape(1, LANES)
        gk_rope = jnp.pad(gk[QK_NOPE_DIM:], (0, LANES - QK_ROPE_DIM)).reshape(1, LANES)
        ar, ai, bbr, bbi = _s5_discretise(s5_lambda_re[l], s5_lambda_im[l], s5_log_dt[l], s5_b_re[l], s5_b_im[l])
        bb, cc, a_bc = _s5_pack(ar, ai, bbr, bbi, s5_c_re[l], s5_c_im[l], bsz)
        wr = jnp.pad(w_router[l].astype(F32), ((0, 0), (0, LANES - N_EXPERTS)))
        wr_hi = wr.astype(BF16)
        wr_lo = (wr - wr_hi.astype(F32)).astype(BF16)
        br = jnp.pad(b_router[l].astype(F32), (0, LANES - N_EXPERTS)).reshape(1, LANES)

        x2 = x.reshape(t, d)
        proj = _inproj(x2, norm1_g[l].reshape(1, d), w_in_p)
        q = _qprep(proj, pos, q_norm_g[l].reshape(1, -1), w_uq_p, gq_full, ind_q, freq, sign)
        k, v = _kprep(proj, pos, kv_norm_g[l].reshape(1, -1), w_kv_p, gk_nope, gk_rope, ind_k, freq, sign)
        o, wg_b, wu_b, wd_b = _attention(q.reshape(bsz, seq, -1), k.reshape(bsz, seq, -1), v.reshape(bsz, seq, -1),
                                         w_gate[l], w_up[l], w_down[l])
        yb = _s5(proj.reshape(bsz, seq, PROJ_W), perm, permt, bb, cc, a_bc, s5_d[l].reshape(1, -1).astype(F32),
                 w_glu[l].astype(BF16), w_o_s5[l].astype(BF16))
        m = _merge(o.reshape(t, -1), proj, b_gates[l].reshape(1, -1).astype(F32), yb.reshape(t, d),
                   w_o_mla[l].astype(BF16))
        x1, h2, top_idx, top_w = _outproj(m, x2, w_out[l].astype(BF16), norm2_g[l].reshape(1, d), wr_hi, wr_lo, br)

        flat_e = top_idx[:, :TOP_K].reshape(-1)
        onehot = (flat_e[:, None] == jnp.arange(N_EXPERTS, dtype=jnp.int32)[None, :]).astype(jnp.int32)
        csum = jnp.cumsum(onehot, axis=0)
        rank = jnp.take_along_axis(csum, flat_e[:, None], axis=1)[:, 0] - 1
        counts = csum[-1]
        nblk = (counts + tm_e - 1) // tm_e
        blk_end = jnp.cumsum(nblk)
        blk_start = blk_end - nblk
        slot = blk_start[flat_e] * tm_e + rank
        nused = blk_end[-1:].astype(jnp.int32)
        block_expert = jnp.minimum(
            jnp.sum((blk_end[None, :] <= jnp.arange(nb_e, dtype=jnp.int32)[:, None]).astype(jnp.int32), axis=1),
            N_EXPERTS - 1).astype(jnp.int32)
        slot_tok = jnp.zeros((n_slots,), jnp.int32).at[slot].set(jnp.arange(n_assign, dtype=jnp.int32) // TOP_K)
        pos_km = slot.reshape(t // tm_c, tm_c, TOP_K).transpose(0, 2, 1).reshape(t // tm_c, 1, TOP_K * tm_c)

        blk_ids = jnp.arange(nb_e, dtype=jnp.int32)
        nvalid = jnp.clip(counts[block_expert] - (blk_ids - blk_start[block_expert]) * tm_e, 0, tm_e).astype(jnp.int32)
        ys = _experts(block_expert, nused, nvalid, slot_tok, h2,
                      wg_b, b_gate[l].reshape(N_EXPERTS, 1, -1).astype(F32),
                      wu_b, b_up[l].reshape(N_EXPERTS, 1, -1).astype(F32),
                      wd_b, b_down[l].reshape(N_EXPERTS, 1, -1).astype(F32), tm=tm_e)
        x = _combine(pos_km, x1, top_w, ys, tm=tm_c).reshape(bsz, seq, d)
    return x
```

```python
import functools
import math

import numpy as np
import jax
import jax.numpy as jnp
from jax import lax
from jax.experimental import pallas as pl
from jax.experimental.pallas import tpu as pltpu

F32 = jnp.float32
BF16 = jnp.bfloat16

D_MODEL = 2048
CHUNK = 64
EPS = 1e-6
N_HEADS = 16
QK_NOPE_DIM = 128
QK_ROPE_DIM = 64
QK_HEAD_DIM = QK_NOPE_DIM + QK_ROPE_DIM
V_HEAD_DIM = 128
Q_LORA_RANK = 512
KV_LORA_RANK = 256
ROPE_THETA = 10000.0
S5_WIDTH = 1024
S5_GROUP = 16
S5_GROUPS = S5_WIDTH // S5_GROUP
S5_STATE = 64
N_EXPERTS = 32
TOP_K = 4
D_FF = 2048
SWIGLU_LIMIT = 7.0
SWIGLU_ALPHA = 1.702

LANES = 128
HEAD_PAD = 256
ROPE_HALF = QK_ROPE_DIM // 2
S5_CLUSTER = 8
N_CLUSTERS = S5_GROUPS // S5_CLUSTER
CLUSTER_STATES = S5_CLUSTER * S5_STATE
SUB_T = 32
PROJ_W = 6144
VMEM_LIMIT = 56 * 1024 * 1024
NEG = -1e30
DENOM_ROWS = 8


def _cparams(sem, vmem=VMEM_LIMIT, **kw):
    return pltpu.CompilerParams(dimension_semantics=sem, vmem_limit_bytes=vmem, **kw)


def _inproj_kernel(x_ref, g_ref, w_ref, o_ref, h_ref):
    @pl.when(pl.program_id(1) == 0)
    def _():
        x = x_ref[...]
        ms = jnp.mean(x * x, axis=-1, keepdims=True)
        h_ref[...] = (x * lax.rsqrt(ms + EPS) * g_ref[...]).astype(BF16)

    o_ref[...] = jnp.dot(h_ref[...], w_ref[...], preferred_element_type=F32).astype(o_ref.dtype)


def _inproj(x2, g, w, *, tm=1024, tn=1536):
    t, d = x2.shape
    n = w.shape[1]
    return pl.pallas_call(
        _inproj_kernel,
        grid=(t // tm, n // tn),
        in_specs=[
            pl.BlockSpec((tm, d), lambda i, j: (i, 0)),
            pl.BlockSpec((1, d), lambda i, j: (0, 0)),
            pl.BlockSpec((d, tn), lambda i, j: (0, j)),
        ],
        out_specs=pl.BlockSpec((tm, tn), lambda i, j: (i, j)),
        out_shape=jax.ShapeDtypeStruct((t, n), BF16),
        scratch_shapes=[pltpu.VMEM((tm, d), BF16)],
        compiler_params=_cparams(("parallel", "arbitrary")),
        name="inproj",
    )(x2, g, w)


def _rope_mid(mid, cos, sin_signed):
    lane = lax.broadcasted_iota(jnp.int32, mid.shape, 1)
    rot = jnp.where(lane < ROPE_HALF, pltpu.roll(mid, LANES - ROPE_HALF, 1), pltpu.roll(mid, ROPE_HALF, 1))
    return mid * cos + rot * sin_signed


def _qprep_kernel(ql_ref, pos_ref, gn_ref, w_ref, gfull_ref, ind_ref, freq_ref, sign_ref, o_ref):
    ql = ql_ref[...].astype(F32)
    ms = jnp.mean(ql * ql, axis=-1, keepdims=True)
    qn = (ql * lax.rsqrt(ms + EPS) * gn_ref[...]).astype(BF16)
    q = jnp.dot(qn, w_ref[...], preferred_element_type=F32)
    ssq = jnp.dot((q * q).astype(BF16), ind_ref[...], preferred_element_type=F32)
    sc = lax.rsqrt(ssq * (1.0 / QK_HEAD_DIM) + EPS)
    ang = pos_ref[...] * freq_ref[...]
    cos = jnp.cos(ang)
    sin_signed = jnp.sin(ang) * sign_ref[...]
    for h in range(N_HEADS):
        s_h = sc[:, h:h + 1]
        lo = h * HEAD_PAD
        nope = q[:, lo:lo + LANES] * s_h * gfull_ref[:, lo:lo + LANES]
        mid = q[:, lo + LANES:lo + HEAD_PAD] * s_h * gfull_ref[:, lo + LANES:lo + HEAD_PAD]
        o_ref[:, lo:lo + LANES] = nope.astype(o_ref.dtype)
        o_ref[:, lo + LANES:lo + HEAD_PAD] = _rope_mid(mid, cos, sin_signed).astype(o_ref.dtype)


def _qprep(proj, pos, gn, w, gfull, ind, freq, sign, *, tm=512):
    t = proj.shape[0]
    n = N_HEADS * HEAD_PAD
    full = lambda shape: pl.BlockSpec(shape, lambda i: (0, 0))
    return pl.pallas_call(
        _qprep_kernel,
        grid=(t // tm,),
        in_specs=[
            pl.BlockSpec((tm, Q_LORA_RANK), lambda i: (i, 10)),
            pl.BlockSpec((tm, 1), lambda i: (i, 0)),
            full((1, Q_LORA_RANK)),
            full((Q_LORA_RANK, n)),
            full((1, n)),
            full((n, LANES)),
            full((1, LANES)),
            full((1, LANES)),
        ],
        out_specs=pl.BlockSpec((tm, n), lambda i: (i, 0)),
        out_shape=jax.ShapeDtypeStruct((t, n), BF16),
        compiler_params=_cparams(("parallel",)),
        name="qprep",
    )(proj, pos, gn, w, gfull, ind, freq, sign)


def _kprep_kernel(kvl_ref, kr_ref, pos_ref, gn_ref, w_ref, gnope_ref, grope_ref, ind_ref, freq_ref, sign_ref,
                  k_ref, v_ref):
    kvl = kvl_ref[...].astype(F32)
    ms = jnp.mean(kvl * kvl, axis=-1, keepdims=True)
    kn = (kvl * lax.rsqrt(ms + EPS) * gn_ref[...]).astype(BF16)
    kv = jnp.dot(kn, w_ref[...], preferred_element_type=F32)
    nd = N_HEADS * QK_NOPE_DIM
    knope = kv[:, :nd]
    v_ref[...] = kv[:, nd:].astype(v_ref.dtype)
    kr = kr_ref[...].astype(F32)
    ssq = jnp.dot((knope * knope).astype(BF16), ind_ref[...], preferred_element_type=F32)
    ssq = ssq + jnp.sum(kr * kr, axis=-1, keepdims=True)
    sc = lax.rsqrt(ssq * (1.0 / QK_HEAD_DIM) + EPS)
    ang = pos_ref[...] * freq_ref[...]
    kr_rot = _rope_mid(kr * grope_ref[...], jnp.cos(ang), jnp.sin(ang) * sign_ref[...])
    for h in range(N_HEADS):
        s_h = sc[:, h:h + 1]
        lo = h * HEAD_PAD
        nope = knope[:, h * LANES:(h + 1) * LANES] * s_h * gnope_ref[...]
        k_ref[:, lo:lo + LANES] = nope.astype(k_ref.dtype)
        k_ref[:, lo + LANES:lo + HEAD_PAD] = (kr_rot * s_h).astype(k_ref.dtype)


def _kprep(proj, pos, gn, w, gnope, grope, ind, freq, sign, *, tm=512):
    t = proj.shape[0]
    n = N_HEADS * HEAD_PAD
    nd = N_HEADS * QK_NOPE_DIM
    full = lambda shape: pl.BlockSpec(shape, lambda i: (0, 0))
    return pl.pallas_call(
        _kprep_kernel,
        grid=(t // tm,),
        in_specs=[
            pl.BlockSpec((tm, KV_LORA_RANK), lambda i: (i, 22)),
            pl.BlockSpec((tm, LANES), lambda i: (i, 46)),
            pl.BlockSpec((tm, 1), lambda i: (i, 0)),
            full((1, KV_LORA_RANK)),
            full((KV_LORA_RANK, 2 * nd)),
            full((1, LANES)),
            full((1, LANES)),
            full((nd, LANES)),
            full((1, LANES)),
            full((1, LANES)),
        ],
        out_specs=[pl.BlockSpec((tm, n), lambda i: (i, 0)), pl.BlockSpec((tm, nd), lambda i: (i, 0))],
        out_shape=[jax.ShapeDtypeStruct((t, n), BF16), jax.ShapeDtypeStruct((t, nd), BF16)],
        compiler_params=_cparams(("parallel",)),
        name="kprep",
    )(proj, proj, pos, gn, w, gnope, grope, ind, freq, sign)


def _attn_kernel(q_ref, k_ref, v_ref, wg_ref, wu_ref, wd_ref, o_ref, wgo_ref, wuo_ref, wdo_ref,
                 m_ref, acc_ref, *, tq, hg):
    wgo_ref[...] = wg_ref[...].astype(wgo_ref.dtype)
    wuo_ref[...] = wu_ref[...].astype(wuo_ref.dtype)
    wdo_ref[...] = wd_ref[...].astype(wdo_ref.dtype)

    qi = pl.program_id(2)
    m_ref[...] = jnp.full(m_ref.shape, NEG, F32)
    acc_ref[...] = jnp.zeros(acc_ref.shape, F32)
    ones = jnp.ones((tq, DENOM_ROWS), BF16)

    def block(j, masked):
        r0 = pl.multiple_of(j * tq, tq)
        for h in range(hg):
            q = q_ref[:, h * HEAD_PAD:(h + 1) * HEAD_PAD]
            kb = k_ref[pl.ds(r0, tq), h * HEAD_PAD:(h + 1) * HEAD_PAD]
            vb = v_ref[pl.ds(r0, tq), h * V_HEAD_DIM:(h + 1) * V_HEAD_DIM]
            st = lax.dot_general(kb, q, (((1,), (1,)), ((), ())), preferred_element_type=F32)
            if masked:
                key = lax.broadcasted_iota(jnp.int32, st.shape, 0) // CHUNK
                qry = lax.broadcasted_iota(jnp.int32, st.shape, 1) // CHUNK
                st = jnp.where(key <= qry, st, NEG)
            m_old = m_ref[h]
            m_new = jnp.maximum(m_old, jnp.max(st, axis=0, keepdims=True))
            alpha = jnp.exp2(m_old - m_new)
            p = jnp.exp2((st - m_new).astype(BF16))
            v_ext = jnp.concatenate([vb, ones], axis=1)
            pv = lax.dot_general(v_ext, p, (((0,), (0,)), ((), ())), preferred_element_type=F32)
            acc_ref[h] = alpha * acc_ref[h] + pv
            m_ref[h] = m_new

    def body(j, c):
        block(j, False)
        return c

    lax.fori_loop(0, qi, body, 0)
    block(qi, True)
    for h in range(hg):
        o = acc_ref[h, :V_HEAD_DIM] / acc_ref[h, V_HEAD_DIM:V_HEAD_DIM + 1]
        o_ref[:, h * V_HEAD_DIM:(h + 1) * V_HEAD_DIM] = o.T.astype(o_ref.dtype)


def _attention(q3, k3, v3, wg, wu, wd, *, tq=1024, hg=2):
    b, s, _ = q3.shape
    tq = min(tq, s)
    ng, nq = N_HEADS // hg, s // tq
    steps = b * ng * nq
    w2 = [w.reshape(-1, w.shape[-1]) for w in (wg, wu, wd)]
    slab = [w.shape[0] // steps for w in w2]
    step = lambda bi, h, i: ((bi * ng + h) * nq + i, 0)
    wspec = [pl.BlockSpec((r, w.shape[1]), step) for r, w in zip(slab, w2)]
    outs = pl.pallas_call(
        functools.partial(_attn_kernel, tq=tq, hg=hg),
        grid=(b, ng, nq),
        in_specs=[
            pl.BlockSpec((None, tq, hg * HEAD_PAD), lambda bi, h, i: (bi, i, h)),
            pl.BlockSpec((None, s, hg * HEAD_PAD), lambda bi, h, i: (bi, 0, h)),
            pl.BlockSpec((None, s, hg * V_HEAD_DIM), lambda bi, h, i: (bi, 0, h)),
        ] + wspec,
        out_specs=[pl.BlockSpec((None, tq, hg * V_HEAD_DIM), lambda bi, h, i: (bi, i, h))] + wspec,
        out_shape=[jax.ShapeDtypeStruct((b, s, N_HEADS * V_HEAD_DIM), BF16)]
        + [jax.ShapeDtypeStruct(w.shape, BF16) for w in w2],
        scratch_shapes=[pltpu.VMEM((hg, 1, tq), F32), pltpu.VMEM((hg, V_HEAD_DIM + DENOM_ROWS, tq), F32)],
        compiler_params=_cparams(("parallel", "parallel", "arbitrary")),
        name="attention",
    )(q3, k3, v3, *w2)
    return (outs[0],) + tuple(o.reshape(w.shape) for o, w in zip(outs[1:], (wg, wu, wd)))


def _s5_kernel(u_ref, perm_ref, permt_ref, bb_ref, cc_ref, a_ref, d_ref, wglu_ref, wo_ref, o_ref,
               ut_ref, bu_ref, y_ref, carry_ref, *, tc):
    nb = u_ref.shape[0]
    rows_sub = nb * SUB_T
    nsub = tc // SUB_T

    @pl.when(pl.program_id(0) == 0)
    def _():
        carry_ref[...] = jnp.zeros(carry_ref.shape, F32)

    for j in range(nsub):
        ub = u_ref[:, j * SUB_T:(j + 1) * SUB_T, :].reshape(rows_sub, S5_WIDTH)
        ut_ref[j * rows_sub:(j + 1) * rows_sub, :] = jnp.dot(
            perm_ref[...], ub, preferred_element_type=F32).astype(BF16)

    for c in range(N_CLUSTERS):
        bu_ref[...] = jnp.dot(ut_ref[:, c * LANES:(c + 1) * LANES], bb_ref[c], preferred_element_type=F32)
        ar = a_ref[c, :, :CLUSTER_STATES]
        ai = a_ref[c, :, CLUSTER_STATES:]

        def step(t, carry):
            xr, xi = carry
            r0 = pl.multiple_of(t * nb, nb)
            br = bu_ref[pl.ds(r0, nb), :CLUSTER_STATES]
            bi = bu_ref[pl.ds(r0, nb), CLUSTER_STATES:]
            nxr = ar * xr - ai * xi + br
            nxi = ar * xi + ai * xr + bi
            bu_ref[pl.ds(r0, nb), :CLUSTER_STATES] = nxr
            bu_ref[pl.ds(r0, nb), CLUSTER_STATES:] = nxi
            return nxr, nxi

        xr, xi = lax.fori_loop(0, tc, step, (carry_ref[c, :, :CLUSTER_STATES], carry_ref[c, :, CLUSTER_STATES:]),
                               unroll=8)
        carry_ref[c, :, :CLUSTER_STATES] = xr
        carry_ref[c, :, CLUSTER_STATES:] = xi
        y_ref[:, c * LANES:(c + 1) * LANES] = jnp.dot(bu_ref[...].astype(BF16), cc_ref[c], preferred_element_type=F32)

    y = y_ref[...] + d_ref[...] * ut_ref[...].astype(F32)
    g = jax.nn.gelu(y)
    z = g * jax.nn.sigmoid(jnp.dot(g.astype(BF16), wglu_ref[...], preferred_element_type=F32))
    yb = jnp.dot(z.astype(BF16), wo_ref[...], preferred_element_type=F32).astype(BF16)
    for j in range(nsub):
        blk = jnp.dot(permt_ref[...], yb[j * rows_sub:(j + 1) * rows_sub, :], preferred_element_type=F32)
        o_ref[:, j * SUB_T:(j + 1) * SUB_T, :] = blk.astype(o_ref.dtype).reshape(nb, SUB_T, o_ref.shape[2])


def _s5(proj3, perm, permt, bb, cc, a_bc, d, wglu, wo, *, tc=128):
    b, s, _ = proj3.shape
    rows = b * tc
    dm = wo.shape[1]
    c2 = lambda shape: pl.BlockSpec(shape, lambda i: (0, 0))
    c3 = lambda shape: pl.BlockSpec(shape, lambda i: (0, 0, 0))
    return pl.pallas_call(
        functools.partial(_s5_kernel, tc=tc),
        grid=(s // tc,),
        in_specs=[
            pl.BlockSpec((b, tc, S5_WIDTH), lambda i: (0, i, 4)),
            c2(perm.shape), c2(permt.shape), c3(bb.shape), c3(cc.shape), c3(a_bc.shape), c2(d.shape),
            c2(wglu.shape), c2(wo.shape),
        ],
        out_specs=pl.BlockSpec((b, tc, dm), lambda i: (0, i, 0)),
        out_shape=jax.ShapeDtypeStruct((b, s, dm), BF16),
        scratch_shapes=[
            pltpu.VMEM((rows, S5_WIDTH), BF16),
            pltpu.VMEM((rows, 2 * CLUSTER_STATES), F32),
            pltpu.VMEM((rows, S5_WIDTH), F32),
            pltpu.VMEM((N_CLUSTERS, b, 2 * CLUSTER_STATES), F32),
        ],
        compiler_params=_cparams(("arbitrary",)),
        name="s5",
    )(proj3, perm, permt, bb, cc, a_bc, d, wglu, wo)


def _merge_kernel(o_ref, gl_ref, bg_ref, yb_ref, w_ref, m_ref):
    d = o_ref.shape[1]
    ya = jnp.dot(o_ref[...], w_ref[...], preferred_element_type=F32)
    ga = jax.nn.sigmoid(gl_ref[:, :d].astype(F32) + bg_ref[:, :d])
    gb = jax.nn.sigmoid(gl_ref[:, d:].astype(F32) + bg_ref[:, d:])
    m_ref[...] = (ga * ya + gb * yb_ref[...].astype(F32)).astype(m_ref.dtype)


def _merge(o2, proj, bg, yb2, w, *, tm=512):
    t, d = o2.shape
    return pl.pallas_call(
        _merge_kernel,
        grid=(t // tm,),
        in_specs=[
            pl.BlockSpec((tm, d), lambda i: (i, 0)),
            pl.BlockSpec((tm, 2 * d), lambda i: (i, 0)),
            pl.BlockSpec((1, 2 * d), lambda i: (0, 0)),
            pl.BlockSpec((tm, d), lambda i: (i, 0)),
            pl.BlockSpec((d, d), lambda i: (0, 0)),
        ],
        out_specs=pl.BlockSpec((tm, d), lambda i: (i, 0)),
        out_shape=jax.ShapeDtypeStruct((t, d), BF16),
        compiler_params=_cparams(("parallel",)),
        name="merge",
    )(o2, proj, bg, yb2, w)


def _to_row_tiles(x):
    return x.reshape(x.shape[0], x.shape[1] // LANES, LANES)


def _from_row_tiles(x):
    return x.reshape(x.shape[0], x.shape[1] * LANES)


def _outproj_kernel(m_ref, x_ref, w_ref, g_ref, wrh_ref, wrl_ref, br_ref, x1_ref, h2_ref, idx_ref, wt_ref):
    x1 = x_ref[...] + jnp.dot(m_ref[...], w_ref[...], preferred_element_type=F32)
    x1_ref[...] = x1
    ms = jnp.mean(x1 * x1, axis=-1, keepdims=True)
    h2 = x1 * lax.rsqrt(ms + EPS) * g_ref[...]
    h_hi = h2.astype(BF16)
    h2_ref[...] = _to_row_tiles(h_hi)
    h_lo = (h2 - h_hi.astype(F32)).astype(BF16)
    logits = (jnp.dot(h_hi, wrh_ref[...], preferred_element_type=F32)
              + jnp.dot(h_lo, wrh_ref[...], preferred_element_type=F32)
              + jnp.dot(h_hi, wrl_ref[...], preferred_element_type=F32)) + br_ref[...]
    lane = lax.broadcasted_iota(jnp.int32, logits.shape, 1)
    work = jnp.where(lane < N_EXPERTS, logits, -jnp.inf)
    idx_out = jnp.zeros(logits.shape, jnp.int32)
    val_out = jnp.zeros(logits.shape, F32)
    v0 = None
    denom = None
    for k in range(TOP_K):
        mx = jnp.max(work, axis=-1, keepdims=True)
        sel = jnp.min(jnp.where(work == mx, lane, LANES), axis=-1, keepdims=True)
        if k == 0:
            v0 = mx
        e = jnp.exp(mx - v0)
        denom = e if k == 0 else denom + e
        idx_out = jnp.where(lane == k, sel, idx_out)
        val_out = jnp.where(lane == k, e, val_out)
        work = jnp.where(lane == sel, -jnp.inf, work)
    idx_ref[...] = idx_out
    wt_ref[...] = val_out / denom


def _outproj(m2, x2, w, g, wrh, wrl, br, *, tm=512):
    t, d = x2.shape
    c2 = lambda shape: pl.BlockSpec(shape, lambda i: (0, 0))
    row = lambda width: pl.BlockSpec((tm, width), lambda i: (i, 0))
    return pl.pallas_call(
        _outproj_kernel,
        grid=(t // tm,),
        in_specs=[row(d), row(d), c2((d, d)), c2((1, d)), c2((d, LANES)), c2((d, LANES)), c2((1, LANES))],
        out_specs=[row(d), pl.BlockSpec((tm, d // LANES, LANES), lambda i: (i, 0, 0)), row(LANES), row(LANES)],
        out_shape=[
            jax.ShapeDtypeStruct((t, d), F32),
            jax.ShapeDtypeStruct((t, d // LANES, LANES), BF16),
            jax.ShapeDtypeStruct((t, LANES), jnp.int32),
            jax.ShapeDtypeStruct((t, LANES), F32),
        ],
        compiler_params=_cparams(("parallel",)),
        name="outproj",
    )(m2, x2, w, g, wrh, wrl, br)


def _slot_tokens_kernel(slot_ref, o_ref, *, chunk):
    i = pl.program_id(0)

    @pl.when(i == 0)
    def _():
        def clear(j, c):
            o_ref[j] = 0
            return c

        lax.fori_loop(0, o_ref.shape[0], clear, 0, unroll=32)

    tokens = chunk // TOP_K
    first_tok = i * tokens

    def place(t, c):
        for k in range(TOP_K):
            o_ref[slot_ref[t * TOP_K + k]] = first_tok + t
        return c

    lax.fori_loop(0, tokens, place, 0, unroll=4)


def _slot_tokens(slot, n_slots, *, chunk=8192):
    n = slot.shape[0]
    chunk = min(chunk, n)
    return pl.pallas_call(
        functools.partial(_slot_tokens_kernel, chunk=chunk),
        grid=(n // chunk,),
        in_specs=[pl.BlockSpec((chunk,), lambda i: (i,), memory_space=pltpu.SMEM)],
        out_specs=pl.BlockSpec(memory_space=pltpu.SMEM),
        out_shape=jax.ShapeDtypeStruct((n_slots,), jnp.int32),
        compiler_params=_cparams(("arbitrary",)),
        name="slot_tokens",
    )(slot)


BOTH_DMA_PRIORITIES = (0, 1)
ROW_QUARTERS = 4


def _start_row_gather(idx_ref, n_rows, src_hbm, dst_ref, sem, priorities=BOTH_DMA_PRIORITIES):
    n_pri = len(priorities)

    def issue(r2, c):
        for u, pri in enumerate(priorities):
            r = r2 * n_pri + u
            pltpu.make_async_copy(src_hbm.at[pl.ds(idx_ref[0, 0, r], 1)], dst_ref.at[pl.ds(r, 1)], sem).start(priority=pri)
        return c

    lax.fori_loop(0, n_rows // n_pri, issue, 0, unroll=8 // n_pri)


def _wait_rows(dst_ref, sem):
    pltpu.make_async_copy(dst_ref, dst_ref, sem).wait()


def _expert_kernel(be_ref, nu_ref, nv_ref, tokc_ref, tokn_ref, h_hbm, wg_ref, bg_ref, wu_ref, bu_ref, wd_ref, bd_ref,
                   o_ref, xbuf_ref, xb_ref, acc_ref, sem, *, tm):
    b = pl.program_id(0)
    f = pl.program_id(1)
    nxt = b + 1

    @pl.when(jnp.logical_and(f == 0, jnp.logical_and(b == 0, nu_ref[0] > 0)))
    def _():
        _start_row_gather(tokc_ref, tm, h_hbm, xbuf_ref.at[0], sem.at[0], priorities=(0,))

    @pl.when(jnp.logical_and(f == 0, jnp.logical_and(nxt < pl.num_programs(0), nxt < nu_ref[0])))
    def _():
        _start_row_gather(tokn_ref, tm, h_hbm, xbuf_ref.at[nxt % 2], sem.at[nxt % 2], priorities=(0,))

    @pl.when(b < nu_ref[0])
    def _():
        @pl.when(f == 0)
        def _():
            _wait_rows(xbuf_ref.at[b % 2], sem.at[b % 2])
            xb_ref[...] = _from_row_tiles(xbuf_ref[b % 2])
            acc_ref[...] = jnp.zeros(acc_ref.shape, F32)

        def ffn(rows):
            xb = xb_ref[:rows]
            gate = jnp.dot(xb, wg_ref[0], preferred_element_type=F32) + bg_ref[0]
            up = jnp.dot(xb, wu_ref[0], preferred_element_type=F32) + bu_ref[0]
            gate = jnp.minimum(gate, SWIGLU_LIMIT)
            up = jnp.clip(up, -SWIGLU_LIMIT, SWIGLU_LIMIT)
            glu = gate * jax.nn.sigmoid(SWIGLU_ALPHA * gate)
            act = ((up + 1.0) * glu).astype(BF16)
            acc_ref[:rows] += jnp.dot(act, wd_ref[0], preferred_element_type=F32)

            @pl.when(f == pl.num_programs(1) - 1)
            def _():
                o_ref[:rows] = _to_row_tiles((acc_ref[:rows] + bd_ref[0]).astype(o_ref.dtype))
                if rows < tm:
                    o_ref[rows:] = jnp.zeros((tm - rows,) + o_ref.shape[1:], o_ref.dtype)

        nv = nv_ref[b]
        quarter = tm // ROW_QUARTERS
        for i in range(1, ROW_QUARTERS + 1):
            lo_ok = nv > (i - 1) * quarter if i > 1 else True
            hi_ok = nv <= i * quarter if i < ROW_QUARTERS else True
            pl.when(jnp.logical_and(lo_ok, hi_ok))(functools.partial(ffn, i * quarter))

    @pl.when(jnp.logical_and(b >= nu_ref[0], f == 0))
    def _():
        o_ref[...] = jnp.zeros(o_ref.shape, o_ref.dtype)


def _experts(block_expert, nused, nvalid, slot_tok, h2, wg, bg, wu, bu, wd, bd, *, tm, tf=1024):
    n_slots = slot_tok.shape[0]
    tile = h2.shape[1:]
    d = tile[0] * tile[1]
    nb = n_slots // tm
    dff = wg.shape[2]
    nf = dff // tf
    tok3 = slot_tok.reshape(nb, 1, tm)

    def blk(b, nu):
        return jnp.minimum(b, nu[0] - 1)

    def fidx(b, f, nu):
        return jnp.where(b < nu[0], f, nf - 1)

    return pl.pallas_call(
        functools.partial(_expert_kernel, tm=tm),
        grid_spec=pltpu.PrefetchScalarGridSpec(
            num_scalar_prefetch=3,
            grid=(nb, nf),
            in_specs=[
                pl.BlockSpec((1, 1, tm), lambda b, f, be, nu, nv: (b, 0, 0), memory_space=pltpu.SMEM),
                pl.BlockSpec((1, 1, tm), lambda b, f, be, nu, nv: (jnp.minimum(b + 1, nb - 1), 0, 0),
                             memory_space=pltpu.SMEM),
                pl.BlockSpec(memory_space=pl.ANY),
                pl.BlockSpec((1, d, tf), lambda b, f, be, nu, nv: (be[blk(b, nu)], 0, fidx(b, f, nu))),
                pl.BlockSpec((1, 1, tf), lambda b, f, be, nu, nv: (be[blk(b, nu)], 0, fidx(b, f, nu))),
                pl.BlockSpec((1, d, tf), lambda b, f, be, nu, nv: (be[blk(b, nu)], 0, fidx(b, f, nu))),
                pl.BlockSpec((1, 1, tf), lambda b, f, be, nu, nv: (be[blk(b, nu)], 0, fidx(b, f, nu))),
                pl.BlockSpec((1, tf, d), lambda b, f, be, nu, nv: (be[blk(b, nu)], fidx(b, f, nu), 0)),
                pl.BlockSpec((1, 1, d), lambda b, f, be, nu, nv: (be[blk(b, nu)], 0, 0)),
            ],
            out_specs=pl.BlockSpec((tm,) + tile, lambda b, f, be, nu, nv: (b, 0, 0)),
            scratch_shapes=[pltpu.VMEM((2, tm) + tile, h2.dtype), pltpu.VMEM((tm, d), BF16), pltpu.VMEM((tm, d), F32),
                            pltpu.SemaphoreType.DMA((2,))],
        ),
        out_shape=jax.ShapeDtypeStruct((n_slots,) + tile, BF16),
        compiler_params=_cparams(("arbitrary", "arbitrary")),
        name="experts",
    )(block_expert, nused, nvalid, tok3, tok3, h2, wg, bg, wu, bu, wd, bd)


def _combine_kernel(posc_ref, posn_ref, x1_ref, wt_ref, ys_hbm, o_ref, buf_ref, sem, *, tm):
    i = pl.program_id(0)
    nxt = i + 1

    @pl.when(i == 0)
    def _():
        _start_row_gather(posc_ref, TOP_K * tm, ys_hbm, buf_ref.at[0], sem.at[0])

    @pl.when(nxt < pl.num_programs(0))
    def _():
        _start_row_gather(posn_ref, TOP_K * tm, ys_hbm, buf_ref.at[nxt % 2], sem.at[nxt % 2])

    slot = i % 2
    _wait_rows(buf_ref.at[slot], sem.at[slot])
    acc = x1_ref[...]
    for k in range(TOP_K):
        rows = buf_ref[slot, k * tm:(k + 1) * tm]
        acc = acc + wt_ref[:, k:k + 1] * _from_row_tiles(rows).astype(F32)
    o_ref[...] = acc


def _combine(pos_km, x1, wt, ys, *, tm=128):
    t, d = x1.shape
    n = t // tm
    return pl.pallas_call(
        functools.partial(_combine_kernel, tm=tm),
        grid=(n,),
        in_specs=[
            pl.BlockSpec((1, 1, TOP_K * tm), lambda i: (i, 0, 0), memory_space=pltpu.SMEM),
            pl.BlockSpec((1, 1, TOP_K * tm), lambda i: (jnp.minimum(i + 1, n - 1), 0, 0), memory_space=pltpu.SMEM),
            pl.BlockSpec((tm, d), lambda i: (i, 0)),
            pl.BlockSpec((tm, LANES), lambda i: (i, 0)),
            pl.BlockSpec(memory_space=pl.ANY),
        ],
        out_specs=pl.BlockSpec((tm, d), lambda i: (i, 0)),
        out_shape=jax.ShapeDtypeStruct((t, d), F32),
        scratch_shapes=[pltpu.VMEM((2, TOP_K * tm) + ys.shape[1:], ys.dtype), pltpu.SemaphoreType.DMA((2,))],
        compiler_params=_cparams(("arbitrary",)),
        name="combine",
    )(pos_km, pos_km, x1, wt, ys)


def _pad_heads(w, per_head):
    lead = w.shape[:-1]
    w = w.reshape(lead + (N_HEADS, per_head))
    w = jnp.pad(w, [(0, 0)] * len(lead) + [(0, 0), (0, HEAD_PAD - per_head)])
    return w.reshape(lead + (N_HEADS * HEAD_PAD,))


def _s5_discretise(lam_re, lam_im, log_dt, b_re, b_im):
    dt = jnp.exp(log_dt.astype(F32))[:, None]
    lr, li = lam_re.astype(F32), lam_im.astype(F32)
    mag = jnp.exp(lr * dt)
    ar, ai = mag * jnp.cos(li * dt), mag * jnp.sin(li * dt)
    den = lr * lr + li * li
    zr = ((ar - 1.0) * lr + ai * li) / den
    zi = (ai * lr - (ar - 1.0) * li) / den
    br, bi = b_re.astype(F32), b_im.astype(F32)
    bbr = zr[..., None] * br - zi[..., None] * bi
    bbi = zr[..., None] * bi + zi[..., None] * br
    return ar, ai, bbr, bbi


def _s5_pack(ar, ai, bbr, bbi, c_re, c_im, nb):
    eye = jnp.eye(S5_CLUSTER, dtype=F32)

    def pack_b(m):
        m4 = m.reshape(N_CLUSTERS, S5_CLUSTER, S5_STATE, S5_GROUP)
        return jnp.einsum('xgpc,gh->xgchp', m4, eye).reshape(N_CLUSTERS, S5_CLUSTER * S5_GROUP, CLUSTER_STATES)

    def pack_c(m):
        m4 = m.reshape(N_CLUSTERS, S5_CLUSTER, S5_GROUP, S5_STATE)
        return jnp.einsum('xgcp,gh->xgphc', m4, eye).reshape(N_CLUSTERS, CLUSTER_STATES, S5_CLUSTER * S5_GROUP)

    bb = jnp.concatenate([pack_b(bbr), pack_b(bbi)], axis=2).astype(BF16)
    cc = jnp.concatenate([pack_c(c_re.astype(F32)), -pack_c(c_im.astype(F32))], axis=1).astype(BF16)
    a = jnp.concatenate([ar.reshape(N_CLUSTERS, CLUSTER_STATES), ai.reshape(N_CLUSTERS, CLUSTER_STATES)], axis=1)
    a_bc = jnp.broadcast_to(a[:, None, :], (N_CLUSTERS, nb, 2 * CLUSTER_STATES))
    return bb, cc, a_bc


def _perm_matrix(nb):
    n = nb * SUB_T
    p = np.zeros((n, n), np.float32)
    for b in range(nb):
        for t in range(SUB_T):
            p[t * nb + b, b * SUB_T + t] = 1.0
    return p


def kernel(x, positions, norm1_g, w_in, b_gates, q_norm_g, w_uq, kv_norm_g, w_ukv, qk_norm_q_g, qk_norm_k_g, w_o_mla, s5_lambda_re, s5_lambda_im, s5_log_dt, s5_b_re, s5_b_im, s5_c_re, s5_c_im, s5_d, w_glu, w_o_s5, w_out, norm2_g, w_router, b_router, w_gate, b_gate, w_up, b_up, w_down, b_down):
    bsz, seq, d = x.shape
    t = bsz * seq
    depth = norm1_g.shape[0]
    o1 = Q_LORA_RANK
    o2 = o1 + KV_LORA_RANK
    o3 = o2 + QK_ROPE_DIM
    o4 = o3 + S5_WIDTH
    half = ROPE_HALF
    inv_freq = ROPE_THETA ** (-jnp.arange(half, dtype=F32) / half)
    lane = np.arange(LANES)
    freq = jnp.where(lane < QK_ROPE_DIM, jnp.tile(inv_freq, LANES // half), 0.0).reshape(1, LANES).astype(F32)
    sign = jnp.asarray(np.where(lane % QK_ROPE_DIM < half, -1.0, 1.0).reshape(1, LANES), F32)
    pos = positions.reshape(t, 1).astype(F32)
    sm_scale = math.log2(math.e) / math.sqrt(QK_HEAD_DIM)
    ind_q = jnp.asarray(np.equal.outer(np.arange(N_HEADS * HEAD_PAD) // HEAD_PAD, lane), BF16)
    ind_k = jnp.asarray(np.equal.outer(np.arange(N_HEADS * QK_NOPE_DIM) // QK_NOPE_DIM, lane), BF16)
    perm_np = _perm_matrix(bsz)
    perm = jnp.asarray(perm_np, BF16)
    permt = jnp.asarray(perm_np.T, BF16)
    tm_e = 512
    n_assign = t * TOP_K
    nb_e = n_assign // tm_e + N_EXPERTS
    n_slots = nb_e * tm_e
    tm_c = 128

    for l in range(depth):
        wi = w_in[l]
        w_in_p = jnp.concatenate(
            [wi[:, o4:], wi[:, o3:o4], wi[:, :o1], wi[:, o1:o2], wi[:, o2:o3],
             jnp.zeros((d, PROJ_W - wi.shape[1]), wi.dtype)], axis=1).astype(BF16)
        w_uq_p = _pad_heads(w_uq[l], QK_HEAD_DIM).astype(BF16)
        gq_full = _pad_heads(jnp.tile(qk_norm_q_g[l].astype(F32), N_HEADS) * sm_scale, QK_HEAD_DIM).reshape(1, -1)
        wkv = w_ukv[l].reshape(KV_LORA_RANK, N_HEADS, QK_NOPE_DIM + V_HEAD_DIM)
        w_kv_p = jnp.concatenate(
            [wkv[:, :, :QK_NOPE_DIM].reshape(KV_LORA_RANK, -1), wkv[:, :, QK_NOPE_DIM:].reshape(KV_LORA_RANK, -1)],
            axis=1).astype(BF16)
        gk = qk_norm_k_g[l].astype(F32)
        gk_nope = gk[:QK_NOPE_DIM].reshape(1, LANES)
        gk_rope = jnp.pad(gk[QK_NOPE_DIM:], (0, LANES - QK_ROPE_DIM)).reshape(1, LANES)
        ar, ai, bbr, bbi = _s5_discretise(s5_lambda_re[l], s5_lambda_im[l], s5_log_dt[l], s5_b_re[l], s5_b_im[l])
        bb, cc, a_bc = _s5_pack(ar, ai, bbr, bbi, s5_c_re[l], s5_c_im[l], bsz)
        wr = jnp.pad(w_router[l].astype(F32), ((0, 0), (0, LANES - N_EXPERTS)))
        wr_hi = wr.astype(BF16)
        wr_lo = (wr - wr_hi.astype(F32)).astype(BF16)
        br = jnp.pad(b_router[l].astype(F32), (0, LANES - N_EXPERTS)).reshape(1, LANES)

        x2 = x.reshape(t, d)
        proj = _inproj(x2, norm1_g[l].reshape(1, d), w_in_p)
        q = _qprep(proj, pos, q_norm_g[l].reshape(1, -1), w_uq_p, gq_full, ind_q, freq, sign)
        k, v = _kprep(proj, pos, kv_norm_g[l].reshape(1, -1), w_kv_p, gk_nope, gk_rope, ind_k, freq, sign)
        o, wg_b, wu_b, wd_b = _attention(q.reshape(bsz, seq, -1), k.reshape(bsz, seq, -1), v.reshape(bsz, seq, -1),
                                         w_gate[l], w_up[l], w_down[l])
        yb = _s5(proj.reshape(bsz, seq, PROJ_W), perm, permt, bb, cc, a_bc, s5_d[l].reshape(1, -1).astype(F32),
                 w_glu[l].astype(BF16), w_o_s5[l].astype(BF16))
        m = _merge(o.reshape(t, -1), proj, b_gates[l].reshape(1, -1).astype(F32), yb.reshape(t, d),
                   w_o_mla[l].astype(BF16))
        x1, h2, top_idx, top_w = _outproj(m, x2, w_out[l].astype(BF16), norm2_g[l].reshape(1, d), wr_hi, wr_lo, br)

        flat_e = top_idx[:, :TOP_K].reshape(-1)
        onehot = (flat_e[:, None] == jnp.arange(N_EXPERTS, dtype=jnp.int32)[None, :]).astype(jnp.int32)
        csum = jnp.cumsum(onehot, axis=0)
        rank = jnp.take_along_axis(csum, flat_e[:, None], axis=1)[:, 0] - 1
        counts = csum[-1]
        nblk = (counts + tm_e - 1) // tm_e
        blk_end = jnp.cumsum(nblk)
        blk_start = blk_end - nblk
        slot = blk_start[flat_e] * tm_e + rank
        nused = blk_end[-1:].astype(jnp.int32)
        block_expert = jnp.minimum(
            jnp.sum((blk_end[None, :] <= jnp.arange(nb_e, dtype=jnp.int32)[:, None]).astype(jnp.int32), axis=1),
            N_EXPERTS - 1).astype(jnp.int32)
        slot_tok = _slot_tokens(slot.astype(jnp.int32), n_slots)
        pos_km = slot.reshape(t // tm_c, tm_c, TOP_K).transpose(0, 2, 1).reshape(t // tm_c, 1, TOP_K * tm_c)

        blk_ids = jnp.arange(nb_e, dtype=jnp.int32)
        nvalid = jnp.clip(counts[block_expert] - (blk_ids - blk_start[block_expert]) * tm_e, 0, tm_e).astype(jnp.int32)
        ys = _experts(block_expert, nused, nvalid, slot_tok, h2,
                      wg_b, b_gate[l].reshape(N_EXPERTS, 1, -1).astype(F32),
                      wu_b, b_up[l].reshape(N_EXPERTS, 1, -1).astype(F32),
                      wd_b, b_down[l].reshape(N_EXPERTS, 1, -1).astype(F32), tm=tm_e)
        x = _combine(pos_km, x1, top_w, ys, tm=tm_c).reshape(bsz, seq, d)
    return x
```

```python
import functools
import math

import numpy as np
import jax
import jax.numpy as jnp
from jax import lax
from jax.experimental import pallas as pl
from jax.experimental.pallas import tpu as pltpu

F32 = jnp.float32
BF16 = jnp.bfloat16

D_MODEL = 2048
CHUNK = 64
EPS = 1e-6
N_HEADS = 16
QK_NOPE_DIM = 128
QK_ROPE_DIM = 64
QK_HEAD_DIM = QK_NOPE_DIM + QK_ROPE_DIM
V_HEAD_DIM = 128
Q_LORA_RANK = 512
KV_LORA_RANK = 256
ROPE_THETA = 10000.0
S5_WIDTH = 1024
S5_GROUP = 16
S5_GROUPS = S5_WIDTH // S5_GROUP
S5_STATE = 64
N_EXPERTS = 32
TOP_K = 4
D_FF = 2048
SWIGLU_LIMIT = 7.0
SWIGLU_ALPHA = 1.702

LANES = 128
HEAD_PAD = 256
ROPE_HALF = QK_ROPE_DIM // 2
S5_CLUSTER = 8
N_CLUSTERS = S5_GROUPS // S5_CLUSTER
CLUSTER_STATES = S5_CLUSTER * S5_STATE
SUB_T = 32
PROJ_W = 6144
VMEM_LIMIT = 56 * 1024 * 1024
NEG = -1e30
DENOM_ROWS = 8


def _cparams(sem, vmem=VMEM_LIMIT, **kw):
    return pltpu.CompilerParams(dimension_semantics=sem, vmem_limit_bytes=vmem, **kw)


def _inproj_kernel(x_ref, g_ref, w_ref, o_ref, h_ref):
    @pl.when(pl.program_id(1) == 0)
    def _():
        x = x_ref[...]
        ms = jnp.mean(x * x, axis=-1, keepdims=True)
        h_ref[...] = (x * lax.rsqrt(ms + EPS) * g_ref[...]).astype(BF16)

    o_ref[...] = jnp.dot(h_ref[...], w_ref[...], preferred_element_type=F32).astype(o_ref.dtype)


def _inproj(x2, g, w, *, tm=1024, tn=1536):
    t, d = x2.shape
    n = w.shape[1]
    return pl.pallas_call(
        _inproj_kernel,
        grid=(t // tm, n // tn),
        in_specs=[
            pl.BlockSpec((tm, d), lambda i, j: (i, 0)),
            pl.BlockSpec((1, d), lambda i, j: (0, 0)),
            pl.BlockSpec((d, tn), lambda i, j: (0, j)),
        ],
        out_specs=pl.BlockSpec((tm, tn), lambda i, j: (i, j)),
        out_shape=jax.ShapeDtypeStruct((t, n), BF16),
        scratch_shapes=[pltpu.VMEM((tm, d), BF16)],
        compiler_params=_cparams(("parallel", "arbitrary")),
        name="inproj",
    )(x2, g, w)


def _rope_mid(mid, cos, sin_signed):
    lane = lax.broadcasted_iota(jnp.int32, mid.shape, 1)
    rot = jnp.where(lane < ROPE_HALF, pltpu.roll(mid, LANES - ROPE_HALF, 1), pltpu.roll(mid, ROPE_HALF, 1))
    return mid * cos + rot * sin_signed


def _qprep_kernel(ql_ref, pos_ref, gn_ref, w_ref, gfull_ref, ind_ref, freq_ref, sign_ref, o_ref):
    ql = ql_ref[...].astype(F32)
    ms = jnp.mean(ql * ql, axis=-1, keepdims=True)
    qn = (ql * lax.rsqrt(ms + EPS) * gn_ref[...]).astype(BF16)
    q = jnp.dot(qn, w_ref[...], preferred_element_type=F32)
    ssq = jnp.dot((q * q).astype(BF16), ind_ref[...], preferred_element_type=F32)
    sc = lax.rsqrt(ssq * (1.0 / QK_HEAD_DIM) + EPS)
    ang = pos_ref[...] * freq_ref[...]
    cos = jnp.cos(ang)
    sin_signed = jnp.sin(ang) * sign_ref[...]
    for h in range(N_HEADS):
        s_h = sc[:, h:h + 1]
        lo = h * HEAD_PAD
        nope = q[:, lo:lo + LANES] * s_h * gfull_ref[:, lo:lo + LANES]
        mid = q[:, lo + LANES:lo + HEAD_PAD] * s_h * gfull_ref[:, lo + LANES:lo + HEAD_PAD]
        o_ref[:, lo:lo + LANES] = nope.astype(o_ref.dtype)
        o_ref[:, lo + LANES:lo + HEAD_PAD] = _rope_mid(mid, cos, sin_signed).astype(o_ref.dtype)


def _qprep(proj, pos, gn, w, gfull, ind, freq, sign, *, tm=512):
    t = proj.shape[0]
    n = N_HEADS * HEAD_PAD
    full = lambda shape: pl.BlockSpec(shape, lambda i: (0, 0))
    return pl.pallas_call(
        _qprep_kernel,
        grid=(t // tm,),
        in_specs=[
            pl.BlockSpec((tm, Q_LORA_RANK), lambda i: (i, 10)),
            pl.BlockSpec((tm, 1), lambda i: (i, 0)),
            full((1, Q_LORA_RANK)),
            full((Q_LORA_RANK, n)),
            full((1, n)),
            full((n, LANES)),
            full((1, LANES)),
            full((1, LANES)),
        ],
        out_specs=pl.BlockSpec((tm, n), lambda i: (i, 0)),
        out_shape=jax.ShapeDtypeStruct((t, n), BF16),
        compiler_params=_cparams(("parallel",)),
        name="qprep",
    )(proj, pos, gn, w, gfull, ind, freq, sign)


def _kprep_kernel(kvl_ref, kr_ref, pos_ref, gn_ref, w_ref, gnope_ref, grope_ref, ind_ref, freq_ref, sign_ref,
                  k_ref, v_ref):
    kvl = kvl_ref[...].astype(F32)
    ms = jnp.mean(kvl * kvl, axis=-1, keepdims=True)
    kn = (kvl * lax.rsqrt(ms + EPS) * gn_ref[...]).astype(BF16)
    kv = jnp.dot(kn, w_ref[...], preferred_element_type=F32)
    nd = N_HEADS * QK_NOPE_DIM
    knope = kv[:, :nd]
    v_ref[...] = kv[:, nd:].astype(v_ref.dtype)
    kr = kr_ref[...].astype(F32)
    ssq = jnp.dot((knope * knope).astype(BF16), ind_ref[...], preferred_element_type=F32)
    ssq = ssq + jnp.sum(kr * kr, axis=-1, keepdims=True)
    sc = lax.rsqrt(ssq * (1.0 / QK_HEAD_DIM) + EPS)
    ang = pos_ref[...] * freq_ref[...]
    kr_rot = _rope_mid(kr * grope_ref[...], jnp.cos(ang), jnp.sin(ang) * sign_ref[...])
    for h in range(N_HEADS):
        s_h = sc[:, h:h + 1]
        lo = h * HEAD_PAD
        nope = knope[:, h * LANES:(h + 1) * LANES] * s_h * gnope_ref[...]
        k_ref[:, lo:lo + LANES] = nope.astype(k_ref.dtype)
        k_ref[:, lo + LANES:lo + HEAD_PAD] = (kr_rot * s_h).astype(k_ref.dtype)


def _kprep(proj, pos, gn, w, gnope, grope, ind, freq, sign, *, tm=512):
    t = proj.shape[0]
    n = N_HEADS * HEAD_PAD
    nd = N_HEADS * QK_NOPE_DIM
    full = lambda shape: pl.BlockSpec(shape, lambda i: (0, 0))
    return pl.pallas_call(
        _kprep_kernel,
        grid=(t // tm,),
        in_specs=[
            pl.BlockSpec((tm, KV_LORA_RANK), lambda i: (i, 22)),
            pl.BlockSpec((tm, LANES), lambda i: (i, 46)),
            pl.BlockSpec((tm, 1), lambda i: (i, 0)),
            full((1, KV_LORA_RANK)),
            full((KV_LORA_RANK, 2 * nd)),
            full((1, LANES)),
            full((1, LANES)),
            full((nd, LANES)),
            full((1, LANES)),
            full((1, LANES)),
        ],
        out_specs=[pl.BlockSpec((tm, n), lambda i: (i, 0)), pl.BlockSpec((tm, nd), lambda i: (i, 0))],
        out_shape=[jax.ShapeDtypeStruct((t, n), BF16), jax.ShapeDtypeStruct((t, nd), BF16)],
        compiler_params=_cparams(("parallel",)),
        name="kprep",
    )(proj, proj, pos, gn, w, gnope, grope, ind, freq, sign)


def _attn_kernel(q_ref, k_ref, v_ref, *refs, tq, hg, n_side):
    side_in, o_ref, side_out = refs[:n_side], refs[n_side], refs[n_side + 1:2 * n_side + 1]
    m_ref, acc_ref = refs[2 * n_side + 1:]
    for w_ref, wo_ref in zip(side_in, side_out):
        wo_ref[...] = w_ref[...].astype(wo_ref.dtype)

    qi = pl.program_id(2)
    m_ref[...] = jnp.full(m_ref.shape, NEG, F32)
    acc_ref[...] = jnp.zeros(acc_ref.shape, F32)
    ones = jnp.ones((tq, DENOM_ROWS), BF16)

    def block(j, masked):
        r0 = pl.multiple_of(j * tq, tq)
        for h in range(hg):
            q = q_ref[:, h * HEAD_PAD:(h + 1) * HEAD_PAD]
            kb = k_ref[pl.ds(r0, tq), h * HEAD_PAD:(h + 1) * HEAD_PAD]
            vb = v_ref[pl.ds(r0, tq), h * V_HEAD_DIM:(h + 1) * V_HEAD_DIM]
            st = lax.dot_general(kb, q, (((1,), (1,)), ((), ())), preferred_element_type=F32)
            if masked:
                key = lax.broadcasted_iota(jnp.int32, st.shape, 0) // CHUNK
                qry = lax.broadcasted_iota(jnp.int32, st.shape, 1) // CHUNK
                st = jnp.where(key <= qry, st, NEG)
            m_old = m_ref[h]
            m_new = jnp.maximum(m_old, jnp.max(st, axis=0, keepdims=True))
            alpha = jnp.exp2(m_old - m_new)
            p = jnp.exp2((st - m_new).astype(BF16))
            v_ext = jnp.concatenate([vb, ones], axis=1)
            pv = lax.dot_general(v_ext, p, (((0,), (0,)), ((), ())), preferred_element_type=F32)
            acc_ref[h] = alpha * acc_ref[h] + pv
            m_ref[h] = m_new

    def body(j, c):
        block(j, False)
        return c

    lax.fori_loop(0, qi, body, 0)
    block(qi, True)
    for h in range(hg):
        o = acc_ref[h, :V_HEAD_DIM] / acc_ref[h, V_HEAD_DIM:V_HEAD_DIM + 1]
        o_ref[:, h * V_HEAD_DIM:(h + 1) * V_HEAD_DIM] = o.T.astype(o_ref.dtype)


def _attention(q3, k3, v3, side_weights, *, tq=1024, hg=2):
    b, s, _ = q3.shape
    tq = min(tq, s)
    ng, nq = N_HEADS // hg, s // tq
    steps = b * ng * nq
    w2 = [w.reshape(-1, w.shape[-1]) for w in side_weights]
    slab = [w.shape[0] // steps for w in w2]
    step = lambda bi, h, i: ((bi * ng + h) * nq + i, 0)
    wspec = [pl.BlockSpec((r, w.shape[1]), step) for r, w in zip(slab, w2)]
    outs = pl.pallas_call(
        functools.partial(_attn_kernel, tq=tq, hg=hg, n_side=len(w2)),
        grid=(b, ng, nq),
        in_specs=[
            pl.BlockSpec((None, tq, hg * HEAD_PAD), lambda bi, h, i: (bi, i, h)),
            pl.BlockSpec((None, s, hg * HEAD_PAD), lambda bi, h, i: (bi, 0, h)),
            pl.BlockSpec((None, s, hg * V_HEAD_DIM), lambda bi, h, i: (bi, 0, h)),
        ] + wspec,
        out_specs=[pl.BlockSpec((None, tq, hg * V_HEAD_DIM), lambda bi, h, i: (bi, i, h))] + wspec,
        out_shape=[jax.ShapeDtypeStruct((b, s, N_HEADS * V_HEAD_DIM), BF16)]
        + [jax.ShapeDtypeStruct(w.shape, BF16) for w in w2],
        scratch_shapes=[pltpu.VMEM((hg, 1, tq), F32), pltpu.VMEM((hg, V_HEAD_DIM + DENOM_ROWS, tq), F32)],
        compiler_params=_cparams(("parallel", "parallel", "arbitrary")),
        name="attention",
    )(q3, k3, v3, *w2)
    return (outs[0],) + tuple(o.reshape(w.shape) for o, w in zip(outs[1:], side_weights))


def _s5_kernel(u_ref, perm_ref, permt_ref, bb_ref, cc_ref, a_ref, d_ref, wglu_ref, wo_ref, o_ref,
               ut_ref, bu_ref, y_ref, carry_ref, *, tc):
    nb = u_ref.shape[0]
    rows_sub = nb * SUB_T
    nsub = tc // SUB_T

    @pl.when(pl.program_id(0) == 0)
    def _():
        carry_ref[...] = jnp.zeros(carry_ref.shape, F32)

    for j in range(nsub):
        ub = u_ref[:, j * SUB_T:(j + 1) * SUB_T, :].reshape(rows_sub, S5_WIDTH)
        ut_ref[j * rows_sub:(j + 1) * rows_sub, :] = jnp.dot(
            perm_ref[...], ub, preferred_element_type=F32).astype(BF16)

    for c in range(N_CLUSTERS):
        bu_ref[...] = jnp.dot(ut_ref[:, c * LANES:(c + 1) * LANES], bb_ref[c], preferred_element_type=F32)
        ar = a_ref[c, :, :CLUSTER_STATES]
        ai = a_ref[c, :, CLUSTER_STATES:]

        def step(t, carry):
            xr, xi = carry
            r0 = pl.multiple_of(t * nb, nb)
            br = bu_ref[pl.ds(r0, nb), :CLUSTER_STATES]
            bi = bu_ref[pl.ds(r0, nb), CLUSTER_STATES:]
            nxr = ar * xr - ai * xi + br
            nxi = ar * xi + ai * xr + bi
            bu_ref[pl.ds(r0, nb), :CLUSTER_STATES] = nxr
            bu_ref[pl.ds(r0, nb), CLUSTER_STATES:] = nxi
            return nxr, nxi

        xr, xi = lax.fori_loop(0, tc, step, (carry_ref[c, :, :CLUSTER_STATES], carry_ref[c, :, CLUSTER_STATES:]),
                               unroll=8)
        carry_ref[c, :, :CLUSTER_STATES] = xr
        carry_ref[c, :, CLUSTER_STATES:] = xi
        y_ref[:, c * LANES:(c + 1) * LANES] = jnp.dot(bu_ref[...].astype(BF16), cc_ref[c], preferred_element_type=F32)

    y = y_ref[...] + d_ref[...] * ut_ref[...].astype(F32)
    g = jax.nn.gelu(y)
    z = g * jax.nn.sigmoid(jnp.dot(g.astype(BF16), wglu_ref[...], preferred_element_type=F32))
    yb = jnp.dot(z.astype(BF16), wo_ref[...], preferred_element_type=F32).astype(BF16)
    for j in range(nsub):
        blk = jnp.dot(permt_ref[...], yb[j * rows_sub:(j + 1) * rows_sub, :], preferred_element_type=F32)
        o_ref[:, j * SUB_T:(j + 1) * SUB_T, :] = blk.astype(o_ref.dtype).reshape(nb, SUB_T, o_ref.shape[2])


def _s5(proj3, perm, permt, bb, cc, a_bc, d, wglu, wo, *, tc=128):
    b, s, _ = proj3.shape
    rows = b * tc
    dm = wo.shape[1]
    c2 = lambda shape: pl.BlockSpec(shape, lambda i: (0, 0))
    c3 = lambda shape: pl.BlockSpec(shape, lambda i: (0, 0, 0))
    return pl.pallas_call(
        functools.partial(_s5_kernel, tc=tc),
        grid=(s // tc,),
        in_specs=[
            pl.BlockSpec((b, tc, S5_WIDTH), lambda i: (0, i, 4)),
            c2(perm.shape), c2(permt.shape), c3(bb.shape), c3(cc.shape), c3(a_bc.shape), c2(d.shape),
            c2(wglu.shape), c2(wo.shape),
        ],
        out_specs=pl.BlockSpec((b, tc, dm), lambda i: (0, i, 0)),
        out_shape=jax.ShapeDtypeStruct((b, s, dm), BF16),
        scratch_shapes=[
            pltpu.VMEM((rows, S5_WIDTH), BF16),
            pltpu.VMEM((rows, 2 * CLUSTER_STATES), F32),
            pltpu.VMEM((rows, S5_WIDTH), F32),
            pltpu.VMEM((N_CLUSTERS, b, 2 * CLUSTER_STATES), F32),
        ],
        compiler_params=_cparams(("arbitrary",)),
        name="s5",
    )(proj3, perm, permt, bb, cc, a_bc, d, wglu, wo)


def _merge_kernel(o_ref, gl_ref, bg_ref, yb_ref, w_ref, m_ref):
    d = o_ref.shape[1]
    ya = jnp.dot(o_ref[...], w_ref[...], preferred_element_type=F32)
    ga = jax.nn.sigmoid(gl_ref[:, :d].astype(F32) + bg_ref[:, :d])
    gb = jax.nn.sigmoid(gl_ref[:, d:].astype(F32) + bg_ref[:, d:])
    m_ref[...] = (ga * ya + gb * yb_ref[...].astype(F32)).astype(m_ref.dtype)


def _merge(o2, proj, bg, yb2, w, *, tm=512):
    t, d = o2.shape
    return pl.pallas_call(
        _merge_kernel,
        grid=(t // tm,),
        in_specs=[
            pl.BlockSpec((tm, d), lambda i: (i, 0)),
            pl.BlockSpec((tm, 2 * d), lambda i: (i, 0)),
            pl.BlockSpec((1, 2 * d), lambda i: (0, 0)),
            pl.BlockSpec((tm, d), lambda i: (i, 0)),
            pl.BlockSpec((d, d), lambda i: (0, 0)),
        ],
        out_specs=pl.BlockSpec((tm, d), lambda i: (i, 0)),
        out_shape=jax.ShapeDtypeStruct((t, d), BF16),
        compiler_params=_cparams(("parallel",)),
        name="merge",
    )(o2, proj, bg, yb2, w)


def _to_row_tiles(x):
    return x.reshape(x.shape[0], x.shape[1] // LANES, LANES)


def _from_row_tiles(x):
    return x.reshape(x.shape[0], x.shape[1] * LANES)


def _outproj_kernel(m_ref, x_ref, w_ref, g_ref, wrh_ref, wrl_ref, br_ref, x1_ref, h2_ref, idx_ref, wt_ref):
    x1 = x_ref[...] + jnp.dot(m_ref[...], w_ref[...], preferred_element_type=F32)
    x1_ref[...] = x1
    ms = jnp.mean(x1 * x1, axis=-1, keepdims=True)
    h2 = x1 * lax.rsqrt(ms + EPS) * g_ref[...]
    h_hi = h2.astype(BF16)
    h2_ref[...] = _to_row_tiles(h_hi)
    h_lo = (h2 - h_hi.astype(F32)).astype(BF16)
    logits = (jnp.dot(h_hi, wrh_ref[...], preferred_element_type=F32)
              + jnp.dot(h_lo, wrh_ref[...], preferred_element_type=F32)
              + jnp.dot(h_hi, wrl_ref[...], preferred_element_type=F32)) + br_ref[...]
    lane = lax.broadcasted_iota(jnp.int32, logits.shape, 1)
    work = jnp.where(lane < N_EXPERTS, logits, -jnp.inf)
    idx_out = jnp.zeros(logits.shape, jnp.int32)
    val_out = jnp.zeros(logits.shape, F32)
    v0 = None
    denom = None
    for k in range(TOP_K):
        mx = jnp.max(work, axis=-1, keepdims=True)
        sel = jnp.min(jnp.where(work == mx, lane, LANES), axis=-1, keepdims=True)
        if k == 0:
            v0 = mx
        e = jnp.exp(mx - v0)
        denom = e if k == 0 else denom + e
        idx_out = jnp.where(lane == k, sel, idx_out)
        val_out = jnp.where(lane == k, e, val_out)
        work = jnp.where(lane == sel, -jnp.inf, work)
    idx_ref[...] = idx_out
    wt_ref[...] = val_out / denom


def _outproj(m2, x2, w, g, wrh, wrl, br, *, tm=512):
    t, d = x2.shape
    c2 = lambda shape: pl.BlockSpec(shape, lambda i: (0, 0))
    row = lambda width: pl.BlockSpec((tm, width), lambda i: (i, 0))
    return pl.pallas_call(
        _outproj_kernel,
        grid=(t // tm,),
        in_specs=[row(d), row(d), c2((d, d)), c2((1, d)), c2((d, LANES)), c2((d, LANES)), c2((1, LANES))],
        out_specs=[row(d), pl.BlockSpec((tm, d // LANES, LANES), lambda i: (i, 0, 0)), row(LANES), row(LANES)],
        out_shape=[
            jax.ShapeDtypeStruct((t, d), F32),
            jax.ShapeDtypeStruct((t, d // LANES, LANES), BF16),
            jax.ShapeDtypeStruct((t, LANES), jnp.int32),
            jax.ShapeDtypeStruct((t, LANES), F32),
        ],
        compiler_params=_cparams(("parallel",)),
        name="outproj",
    )(m2, x2, w, g, wrh, wrl, br)


def _slot_tokens_kernel(slot_ref, o_ref, *, chunk):
    i = pl.program_id(0)

    @pl.when(i == 0)
    def _():
        def clear(j, c):
            o_ref[j] = 0
            return c

        lax.fori_loop(0, o_ref.shape[0], clear, 0, unroll=32)

    tokens = chunk // TOP_K
    first_tok = i * tokens

    def place(t, c):
        for k in range(TOP_K):
            o_ref[slot_ref[t * TOP_K + k]] = first_tok + t
        return c

    lax.fori_loop(0, tokens, place, 0, unroll=4)


def _slot_tokens(slot, n_slots, *, chunk=8192):
    n = slot.shape[0]
    chunk = min(chunk, n)
    return pl.pallas_call(
        functools.partial(_slot_tokens_kernel, chunk=chunk),
        grid=(n // chunk,),
        in_specs=[pl.BlockSpec((chunk,), lambda i: (i,), memory_space=pltpu.SMEM)],
        out_specs=pl.BlockSpec(memory_space=pltpu.SMEM),
        out_shape=jax.ShapeDtypeStruct((n_slots,), jnp.int32),
        compiler_params=_cparams(("arbitrary",)),
        name="slot_tokens",
    )(slot)


BOTH_DMA_PRIORITIES = (0, 1)
ROW_QUARTERS = 4


def _start_row_gather(idx_ref, n_rows, src_hbm, dst_ref, sem, priorities=BOTH_DMA_PRIORITIES):
    n_pri = len(priorities)

    def issue(r2, c):
        for u, pri in enumerate(priorities):
            r = r2 * n_pri + u
            pltpu.make_async_copy(src_hbm.at[pl.ds(idx_ref[0, 0, r], 1)], dst_ref.at[pl.ds(r, 1)], sem).start(priority=pri)
        return c

    lax.fori_loop(0, n_rows // n_pri, issue, 0, unroll=8 // n_pri)


def _wait_rows(dst_ref, sem):
    pltpu.make_async_copy(dst_ref, dst_ref, sem).wait()


def _expert_kernel(be_ref, nu_ref, nv_ref, tokc_ref, tokn_ref, h_hbm, wg_ref, bg_ref, wu_ref, bu_ref, wd_ref, bd_ref,
                   o_ref, xbuf_ref, xb_ref, acc_ref, sem, *, tm):
    b = pl.program_id(0)
    f = pl.program_id(1)
    nxt = b + 1

    @pl.when(jnp.logical_and(f == 0, jnp.logical_and(b == 0, nu_ref[0] > 0)))
    def _():
        _start_row_gather(tokc_ref, tm, h_hbm, xbuf_ref.at[0], sem.at[0], priorities=(0,))

    @pl.when(jnp.logical_and(f == 0, jnp.logical_and(nxt < pl.num_programs(0), nxt < nu_ref[0])))
    def _():
        _start_row_gather(tokn_ref, tm, h_hbm, xbuf_ref.at[nxt % 2], sem.at[nxt % 2], priorities=(0,))

    @pl.when(b < nu_ref[0])
    def _():
        @pl.when(f == 0)
        def _():
            _wait_rows(xbuf_ref.at[b % 2], sem.at[b % 2])
            xb_ref[...] = _from_row_tiles(xbuf_ref[b % 2])
            acc_ref[...] = jnp.zeros(acc_ref.shape, F32)

        def ffn(rows):
            xb = xb_ref[:rows]
            gate = jnp.dot(xb, wg_ref[0], preferred_element_type=F32) + bg_ref[0]
            up = jnp.dot(xb, wu_ref[0], preferred_element_type=F32) + bu_ref[0]
            gate = jnp.minimum(gate, SWIGLU_LIMIT)
            up = jnp.clip(up, -SWIGLU_LIMIT, SWIGLU_LIMIT)
            glu = gate * jax.nn.sigmoid(SWIGLU_ALPHA * gate)
            act = ((up + 1.0) * glu).astype(BF16)
            acc_ref[:rows] += jnp.dot(act, wd_ref[0].astype(BF16), preferred_element_type=F32)

            @pl.when(f == pl.num_programs(1) - 1)
            def _():
                o_ref[:rows] = _to_row_tiles((acc_ref[:rows] + bd_ref[0]).astype(o_ref.dtype))
                if rows < tm:
                    o_ref[rows:] = jnp.zeros((tm - rows,) + o_ref.shape[1:], o_ref.dtype)

        nv = nv_ref[b]
        quarter = tm // ROW_QUARTERS
        for i in range(1, ROW_QUARTERS + 1):
            lo_ok = nv > (i - 1) * quarter if i > 1 else True
            hi_ok = nv <= i * quarter if i < ROW_QUARTERS else True
            pl.when(jnp.logical_and(lo_ok, hi_ok))(functools.partial(ffn, i * quarter))

    @pl.when(jnp.logical_and(b >= nu_ref[0], f == 0))
    def _():
        o_ref[...] = jnp.zeros(o_ref.shape, o_ref.dtype)


def _experts(block_expert, nused, nvalid, slot_tok, h2, wg, bg, wu, bu, wd, bd, *, tm, tf=1024):
    n_slots = slot_tok.shape[0]
    tile = h2.shape[1:]
    d = tile[0] * tile[1]
    nb = n_slots // tm
    dff = wg.shape[2]
    nf = dff // tf
    tok3 = slot_tok.reshape(nb, 1, tm)

    def blk(b, nu):
        return jnp.minimum(b, nu[0] - 1)

    def fidx(b, f, nu):
        return jnp.where(b < nu[0], f, nf - 1)

    return pl.pallas_call(
        functools.partial(_expert_kernel, tm=tm),
        grid_spec=pltpu.PrefetchScalarGridSpec(
            num_scalar_prefetch=3,
            grid=(nb, nf),
            in_specs=[
                pl.BlockSpec((1, 1, tm), lambda b, f, be, nu, nv: (b, 0, 0), memory_space=pltpu.SMEM),
                pl.BlockSpec((1, 1, tm), lambda b, f, be, nu, nv: (jnp.minimum(b + 1, nb - 1), 0, 0),
                             memory_space=pltpu.SMEM),
                pl.BlockSpec(memory_space=pl.ANY),
                pl.BlockSpec((1, d, tf), lambda b, f, be, nu, nv: (be[blk(b, nu)], 0, fidx(b, f, nu))),
                pl.BlockSpec((1, 1, tf), lambda b, f, be, nu, nv: (be[blk(b, nu)], 0, fidx(b, f, nu))),
                pl.BlockSpec((1, d, tf), lambda b, f, be, nu, nv: (be[blk(b, nu)], 0, fidx(b, f, nu))),
                pl.BlockSpec((1, 1, tf), lambda b, f, be, nu, nv: (be[blk(b, nu)], 0, fidx(b, f, nu))),
                pl.BlockSpec((1, tf, d), lambda b, f, be, nu, nv: (be[blk(b, nu)], fidx(b, f, nu), 0)),
                pl.BlockSpec((1, 1, d), lambda b, f, be, nu, nv: (be[blk(b, nu)], 0, 0)),
            ],
            out_specs=pl.BlockSpec((tm,) + tile, lambda b, f, be, nu, nv: (b, 0, 0)),
            scratch_shapes=[pltpu.VMEM((2, tm) + tile, h2.dtype), pltpu.VMEM((tm, d), BF16), pltpu.VMEM((tm, d), F32),
                            pltpu.SemaphoreType.DMA((2,))],
        ),
        out_shape=jax.ShapeDtypeStruct((n_slots,) + tile, BF16),
        compiler_params=_cparams(("arbitrary", "arbitrary")),
        name="experts",
    )(block_expert, nused, nvalid, tok3, tok3, h2, wg, bg, wu, bu, wd, bd)


def _combine_kernel(posc_ref, posn_ref, x1_ref, wt_ref, ys_hbm, o_ref, buf_ref, sem, *, tm):
    i = pl.program_id(0)
    nxt = i + 1

    @pl.when(i == 0)
    def _():
        _start_row_gather(posc_ref, TOP_K * tm, ys_hbm, buf_ref.at[0], sem.at[0])

    @pl.when(nxt < pl.num_programs(0))
    def _():
        _start_row_gather(posn_ref, TOP_K * tm, ys_hbm, buf_ref.at[nxt % 2], sem.at[nxt % 2])

    slot = i % 2
    _wait_rows(buf_ref.at[slot], sem.at[slot])
    acc = x1_ref[...]
    for k in range(TOP_K):
        rows = buf_ref[slot, k * tm:(k + 1) * tm]
        acc = acc + wt_ref[:, k:k + 1] * _from_row_tiles(rows).astype(F32)
    o_ref[...] = acc


def _combine(pos_km, x1, wt, ys, *, tm=128):
    t, d = x1.shape
    n = t // tm
    return pl.pallas_call(
        functools.partial(_combine_kernel, tm=tm),
        grid=(n,),
        in_specs=[
            pl.BlockSpec((1, 1, TOP_K * tm), lambda i: (i, 0, 0), memory_space=pltpu.SMEM),
            pl.BlockSpec((1, 1, TOP_K * tm), lambda i: (jnp.minimum(i + 1, n - 1), 0, 0), memory_space=pltpu.SMEM),
            pl.BlockSpec((tm, d), lambda i: (i, 0)),
            pl.BlockSpec((tm, LANES), lambda i: (i, 0)),
            pl.BlockSpec(memory_space=pl.ANY),
        ],
        out_specs=pl.BlockSpec((tm, d), lambda i: (i, 0)),
        out_shape=jax.ShapeDtypeStruct((t, d), F32),
        scratch_shapes=[pltpu.VMEM((2, TOP_K * tm) + ys.shape[1:], ys.dtype), pltpu.SemaphoreType.DMA((2,))],
        compiler_params=_cparams(("arbitrary",)),
        name="combine",
    )(pos_km, pos_km, x1, wt, ys)


def _pad_heads(w, per_head):
    lead = w.shape[:-1]
    w = w.reshape(lead + (N_HEADS, per_head))
    w = jnp.pad(w, [(0, 0)] * len(lead) + [(0, 0), (0, HEAD_PAD - per_head)])
    return w.reshape(lead + (N_HEADS * HEAD_PAD,))


def _s5_discretise(lam_re, lam_im, log_dt, b_re, b_im):
    dt = jnp.exp(log_dt.astype(F32))[:, None]
    lr, li = lam_re.astype(F32), lam_im.astype(F32)
    mag = jnp.exp(lr * dt)
    ar, ai = mag * jnp.cos(li * dt), mag * jnp.sin(li * dt)
    den = lr * lr + li * li
    zr = ((ar - 1.0) * lr + ai * li) / den
    zi = (ai * lr - (ar - 1.0) * li) / den
    br, bi = b_re.astype(F32), b_im.astype(F32)
    bbr = zr[..., None] * br - zi[..., None] * bi
    bbi = zr[..., None] * bi + zi[..., None] * br
    return ar, ai, bbr, bbi


def _s5_pack(ar, ai, bbr, bbi, c_re, c_im, nb):
    eye = jnp.eye(S5_CLUSTER, dtype=F32)

    def pack_b(m):
        m4 = m.reshape(N_CLUSTERS, S5_CLUSTER, S5_STATE, S5_GROUP)
        return jnp.einsum('xgpc,gh->xgchp', m4, eye).reshape(N_CLUSTERS, S5_CLUSTER * S5_GROUP, CLUSTER_STATES)

    def pack_c(m):
        m4 = m.reshape(N_CLUSTERS, S5_CLUSTER, S5_GROUP, S5_STATE)
        return jnp.einsum('xgcp,gh->xgphc', m4, eye).reshape(N_CLUSTERS, CLUSTER_STATES, S5_CLUSTER * S5_GROUP)

    bb = jnp.concatenate([pack_b(bbr), pack_b(bbi)], axis=2).astype(BF16)
    cc = jnp.concatenate([pack_c(c_re.astype(F32)), -pack_c(c_im.astype(F32))], axis=1).astype(BF16)
    a = jnp.concatenate([ar.reshape(N_CLUSTERS, CLUSTER_STATES), ai.reshape(N_CLUSTERS, CLUSTER_STATES)], axis=1)
    a_bc = jnp.broadcast_to(a[:, None, :], (N_CLUSTERS, nb, 2 * CLUSTER_STATES))
    return bb, cc, a_bc


def _perm_matrix(nb):
    n = nb * SUB_T
    p = np.zeros((n, n), np.float32)
    for b in range(nb):
        for t in range(SUB_T):
            p[t * nb + b, b * SUB_T + t] = 1.0
    return p


def kernel(x, positions, norm1_g, w_in, b_gates, q_norm_g, w_uq, kv_norm_g, w_ukv, qk_norm_q_g, qk_norm_k_g, w_o_mla, s5_lambda_re, s5_lambda_im, s5_log_dt, s5_b_re, s5_b_im, s5_c_re, s5_c_im, s5_d, w_glu, w_o_s5, w_out, norm2_g, w_router, b_router, w_gate, b_gate, w_up, b_up, w_down, b_down):
    bsz, seq, d = x.shape
    t = bsz * seq
    depth = norm1_g.shape[0]
    o1 = Q_LORA_RANK
    o2 = o1 + KV_LORA_RANK
    o3 = o2 + QK_ROPE_DIM
    o4 = o3 + S5_WIDTH
    half = ROPE_HALF
    inv_freq = ROPE_THETA ** (-jnp.arange(half, dtype=F32) / half)
    lane = np.arange(LANES)
    freq = jnp.where(lane < QK_ROPE_DIM, jnp.tile(inv_freq, LANES // half), 0.0).reshape(1, LANES).astype(F32)
    sign = jnp.asarray(np.where(lane % QK_ROPE_DIM < half, -1.0, 1.0).reshape(1, LANES), F32)
    pos = positions.reshape(t, 1).astype(F32)
    sm_scale = math.log2(math.e) / math.sqrt(QK_HEAD_DIM)
    ind_q = jnp.asarray(np.equal.outer(np.arange(N_HEADS * HEAD_PAD) // HEAD_PAD, lane), BF16)
    ind_k = jnp.asarray(np.equal.outer(np.arange(N_HEADS * QK_NOPE_DIM) // QK_NOPE_DIM, lane), BF16)
    perm_np = _perm_matrix(bsz)
    perm = jnp.asarray(perm_np, BF16)
    permt = jnp.asarray(perm_np.T, BF16)
    tm_e = 512
    n_assign = t * TOP_K
    nb_e = n_assign // tm_e + N_EXPERTS
    n_slots = nb_e * tm_e
    tm_c = 128

    for l in range(depth):
        wi = w_in[l]
        w_in_p = jnp.concatenate(
            [wi[:, o4:], wi[:, o3:o4], wi[:, :o1], wi[:, o1:o2], wi[:, o2:o3],
             jnp.zeros((d, PROJ_W - wi.shape[1]), wi.dtype)], axis=1).astype(BF16)
        w_uq_p = _pad_heads(w_uq[l], QK_HEAD_DIM).astype(BF16)
        gq_full = _pad_heads(jnp.tile(qk_norm_q_g[l].astype(F32), N_HEADS) * sm_scale, QK_HEAD_DIM).reshape(1, -1)
        wkv = w_ukv[l].reshape(KV_LORA_RANK, N_HEADS, QK_NOPE_DIM + V_HEAD_DIM)
        w_kv_p = jnp.concatenate(
            [wkv[:, :, :QK_NOPE_DIM].reshape(KV_LORA_RANK, -1), wkv[:, :, QK_NOPE_DIM:].reshape(KV_LORA_RANK, -1)],
            axis=1).astype(BF16)
        gk = qk_norm_k_g[l].astype(F32)
        gk_nope = gk[:QK_NOPE_DIM].reshape(1, LANES)
        gk_rope = jnp.pad(gk[QK_NOPE_DIM:], (0, LANES - QK_ROPE_DIM)).reshape(1, LANES)
        ar, ai, bbr, bbi = _s5_discretise(s5_lambda_re[l], s5_lambda_im[l], s5_log_dt[l], s5_b_re[l], s5_b_im[l])
        bb, cc, a_bc = _s5_pack(ar, ai, bbr, bbi, s5_c_re[l], s5_c_im[l], bsz)
        wr = jnp.pad(w_router[l].astype(F32), ((0, 0), (0, LANES - N_EXPERTS)))
        wr_hi = wr.astype(BF16)
        wr_lo = (wr - wr_hi.astype(F32)).astype(BF16)
        br = jnp.pad(b_router[l].astype(F32), (0, LANES - N_EXPERTS)).reshape(1, LANES)

        x2 = x.reshape(t, d)
        proj = _inproj(x2, norm1_g[l].reshape(1, d), w_in_p)
        q = _qprep(proj, pos, q_norm_g[l].reshape(1, -1), w_uq_p, gq_full, ind_q, freq, sign)
        k, v = _kprep(proj, pos, kv_norm_g[l].reshape(1, -1), w_kv_p, gk_nope, gk_rope, ind_k, freq, sign)
        o, wg_b, wu_b = _attention(q.reshape(bsz, seq, -1), k.reshape(bsz, seq, -1), v.reshape(bsz, seq, -1),
                                   (w_gate[l], w_up[l]))
        yb = _s5(proj.reshape(bsz, seq, PROJ_W), perm, permt, bb, cc, a_bc, s5_d[l].reshape(1, -1).astype(F32),
                 w_glu[l].astype(BF16), w_o_s5[l].astype(BF16))
        m = _merge(o.reshape(t, -1), proj, b_gates[l].reshape(1, -1).astype(F32), yb.reshape(t, d),
                   w_o_mla[l].astype(BF16))
        x1, h2, top_idx, top_w = _outproj(m, x2, w_out[l].astype(BF16), norm2_g[l].reshape(1, d), wr_hi, wr_lo, br)

        flat_e = top_idx[:, :TOP_K].reshape(-1)
        onehot = (flat_e[:, None] == jnp.arange(N_EXPERTS, dtype=jnp.int32)[None, :]).astype(jnp.int32)
        csum = jnp.cumsum(onehot, axis=0)
        rank = jnp.take_along_axis(csum, flat_e[:, None], axis=1)[:, 0] - 1
        counts = csum[-1]
        nblk = (counts + tm_e - 1) // tm_e
        blk_end = jnp.cumsum(nblk)
        blk_start = blk_end - nblk
        slot = blk_start[flat_e] * tm_e + rank
        nused = blk_end[-1:].astype(jnp.int32)
        block_expert = jnp.minimum(
            jnp.sum((blk_end[None, :] <= jnp.arange(nb_e, dtype=jnp.int32)[:, None]).astype(jnp.int32), axis=1),
            N_EXPERTS - 1).astype(jnp.int32)
        slot_tok = _slot_tokens(slot.astype(jnp.int32), n_slots)
        pos_km = slot.reshape(t // tm_c, tm_c, TOP_K).transpose(0, 2, 1).reshape(t // tm_c, 1, TOP_K * tm_c)

        blk_ids = jnp.arange(nb_e, dtype=jnp.int32)
        nvalid = jnp.clip(counts[block_expert] - (blk_ids - blk_start[block_expert]) * tm_e, 0, tm_e).astype(jnp.int32)
        ys = _experts(block_expert, nused, nvalid, slot_tok, h2,
                      wg_b, b_gate[l].reshape(N_EXPERTS, 1, -1).astype(F32),
                      wu_b, b_up[l].reshape(N_EXPERTS, 1, -1).astype(F32),
                      w_down[l], b_down[l].reshape(N_EXPERTS, 1, -1).astype(F32), tm=tm_e)
        x = _combine(pos_km, x1, top_w, ys, tm=tm_c).reshape(bsz, seq, d)
    return x
```

```python
import functools
import math

import numpy as np
import jax
import jax.numpy as jnp
from jax import lax
from jax.experimental import pallas as pl
from jax.experimental.pallas import tpu as pltpu

F32 = jnp.float32
BF16 = jnp.bfloat16

D_MODEL = 2048
CHUNK = 64
EPS = 1e-6
N_HEADS = 16
QK_NOPE_DIM = 128
QK_ROPE_DIM = 64
QK_HEAD_DIM = QK_NOPE_DIM + QK_ROPE_DIM
V_HEAD_DIM = 128
Q_LORA_RANK = 512
KV_LORA_RANK = 256
ROPE_THETA = 10000.0
S5_WIDTH = 1024
S5_GROUP = 16
S5_GROUPS = S5_WIDTH // S5_GROUP
S5_STATE = 64
N_EXPERTS = 32
TOP_K = 4
D_FF = 2048
SWIGLU_LIMIT = 7.0
SWIGLU_ALPHA = 1.702

LANES = 128
HEAD_PAD = 256
ROPE_HALF = QK_ROPE_DIM // 2
S5_CLUSTER = 8
N_CLUSTERS = S5_GROUPS // S5_CLUSTER
CLUSTER_STATES = S5_CLUSTER * S5_STATE
SUB_T = 32
PROJ_W = 6144
VMEM_LIMIT = 56 * 1024 * 1024
NEG = -1e30
DENOM_ROWS = 8


def _cparams(sem, vmem=VMEM_LIMIT, **kw):
    return pltpu.CompilerParams(dimension_semantics=sem, vmem_limit_bytes=vmem, **kw)


def _inproj_kernel(x_ref, g_ref, w_ref, o_ref, h_ref):
    @pl.when(pl.program_id(1) == 0)
    def _():
        x = x_ref[...]
        ms = jnp.mean(x * x, axis=-1, keepdims=True)
        h_ref[...] = (x * lax.rsqrt(ms + EPS) * g_ref[...]).astype(BF16)

    o_ref[...] = jnp.dot(h_ref[...], w_ref[...], preferred_element_type=F32).astype(o_ref.dtype)


def _inproj(x2, g, w, *, tm=1024, tn=1536):
    t, d = x2.shape
    n = w.shape[1]
    return pl.pallas_call(
        _inproj_kernel,
        grid=(t // tm, n // tn),
        in_specs=[
            pl.BlockSpec((tm, d), lambda i, j: (i, 0)),
            pl.BlockSpec((1, d), lambda i, j: (0, 0)),
            pl.BlockSpec((d, tn), lambda i, j: (0, j)),
        ],
        out_specs=pl.BlockSpec((tm, tn), lambda i, j: (i, j)),
        out_shape=jax.ShapeDtypeStruct((t, n), BF16),
        scratch_shapes=[pltpu.VMEM((tm, d), BF16)],
        compiler_params=_cparams(("parallel", "arbitrary")),
        name="inproj",
    )(x2, g, w)


def _rope_mid(mid, cos, sin_signed):
    lane = lax.broadcasted_iota(jnp.int32, mid.shape, 1)
    rot = jnp.where(lane < ROPE_HALF, pltpu.roll(mid, LANES - ROPE_HALF, 1), pltpu.roll(mid, ROPE_HALF, 1))
    return mid * cos + rot * sin_signed


def _qprep_kernel(ql_ref, pos_ref, gn_ref, w_ref, gfull_ref, ind_ref, freq_ref, sign_ref, o_ref):
    ql = ql_ref[...].astype(F32)
    ms = jnp.mean(ql * ql, axis=-1, keepdims=True)
    qn = (ql * lax.rsqrt(ms + EPS) * gn_ref[...]).astype(BF16)
    q = jnp.dot(qn, w_ref[...], preferred_element_type=F32)
    ssq = jnp.dot((q * q).astype(BF16), ind_ref[...], preferred_element_type=F32)
    sc = lax.rsqrt(ssq * (1.0 / QK_HEAD_DIM) + EPS)
    ang = pos_ref[...] * freq_ref[...]
    cos = jnp.cos(ang)
    sin_signed = jnp.sin(ang) * sign_ref[...]
    for h in range(N_HEADS):
        s_h = sc[:, h:h + 1]
        lo = h * HEAD_PAD
        nope = q[:, lo:lo + LANES] * s_h * gfull_ref[:, lo:lo + LANES]
        mid = q[:, lo + LANES:lo + HEAD_PAD] * s_h * gfull_ref[:, lo + LANES:lo + HEAD_PAD]
        o_ref[:, lo:lo + LANES] = nope.astype(o_ref.dtype)
        o_ref[:, lo + LANES:lo + HEAD_PAD] = _rope_mid(mid, cos, sin_signed).astype(o_ref.dtype)


def _qprep(proj, pos, gn, w, gfull, ind, freq, sign, *, tm=512):
    t = proj.shape[0]
    n = N_HEADS * HEAD_PAD
    full = lambda shape: pl.BlockSpec(shape, lambda i: (0, 0))
    return pl.pallas_call(
        _qprep_kernel,
        grid=(t // tm,),
        in_specs=[
            pl.BlockSpec((tm, Q_LORA_RANK), lambda i: (i, 10)),
            pl.BlockSpec((tm, 1), lambda i: (i, 0)),
            full((1, Q_LORA_RANK)),
            full((Q_LORA_RANK, n)),
            full((1, n)),
            full((n, LANES)),
            full((1, LANES)),
            full((1, LANES)),
        ],
        out_specs=pl.BlockSpec((tm, n), lambda i: (i, 0)),
        out_shape=jax.ShapeDtypeStruct((t, n), BF16),
        compiler_params=_cparams(("parallel",)),
        name="qprep",
    )(proj, pos, gn, w, gfull, ind, freq, sign)


def _kprep_kernel(kvl_ref, kr_ref, pos_ref, gn_ref, w_ref, gnope_ref, grope_ref, ind_ref, freq_ref, sign_ref,
                  k_ref, v_ref):
    kvl = kvl_ref[...].astype(F32)
    ms = jnp.mean(kvl * kvl, axis=-1, keepdims=True)
    kn = (kvl * lax.rsqrt(ms + EPS) * gn_ref[...]).astype(BF16)
    kv = jnp.dot(kn, w_ref[...], preferred_element_type=F32)
    nd = N_HEADS * QK_NOPE_DIM
    knope = kv[:, :nd]
    v_ref[...] = kv[:, nd:].astype(v_ref.dtype)
    kr = kr_ref[...].astype(F32)
    ssq = jnp.dot((knope * knope).astype(BF16), ind_ref[...], preferred_element_type=F32)
    ssq = ssq + jnp.sum(kr * kr, axis=-1, keepdims=True)
    sc = lax.rsqrt(ssq * (1.0 / QK_HEAD_DIM) + EPS)
    ang = pos_ref[...] * freq_ref[...]
    kr_rot = _rope_mid(kr * grope_ref[...], jnp.cos(ang), jnp.sin(ang) * sign_ref[...])
    for h in range(N_HEADS):
        s_h = sc[:, h:h + 1]
        lo = h * HEAD_PAD
        nope = knope[:, h * LANES:(h + 1) * LANES] * s_h * gnope_ref[...]
        k_ref[:, lo:lo + LANES] = nope.astype(k_ref.dtype)
        k_ref[:, lo + LANES:lo + HEAD_PAD] = (kr_rot * s_h).astype(k_ref.dtype)


def _kprep(proj, pos, gn, w, gnope, grope, ind, freq, sign, *, tm=512):
    t = proj.shape[0]
    n = N_HEADS * HEAD_PAD
    nd = N_HEADS * QK_NOPE_DIM
    full = lambda shape: pl.BlockSpec(shape, lambda i: (0, 0))
    return pl.pallas_call(
        _kprep_kernel,
        grid=(t // tm,),
        in_specs=[
            pl.BlockSpec((tm, KV_LORA_RANK), lambda i: (i, 22)),
            pl.BlockSpec((tm, LANES), lambda i: (i, 46)),
            pl.BlockSpec((tm, 1), lambda i: (i, 0)),
            full((1, KV_LORA_RANK)),
            full((KV_LORA_RANK, 2 * nd)),
            full((1, LANES)),
            full((1, LANES)),
            full((nd, LANES)),
            full((1, LANES)),
            full((1, LANES)),
        ],
        out_specs=[pl.BlockSpec((tm, n), lambda i: (i, 0)), pl.BlockSpec((tm, nd), lambda i: (i, 0))],
        out_shape=[jax.ShapeDtypeStruct((t, n), BF16), jax.ShapeDtypeStruct((t, nd), BF16)],
        compiler_params=_cparams(("parallel",)),
        name="kprep",
    )(proj, proj, pos, gn, w, gnope, grope, ind, freq, sign)


def _attn_kernel(q_ref, k_ref, v_ref, *refs, tq, hg, n_side):
    side_in, o_ref, side_out = refs[:n_side], refs[n_side], refs[n_side + 1:2 * n_side + 1]
    m_ref, acc_ref = refs[2 * n_side + 1:]
    for w_ref, wo_ref in zip(side_in, side_out):
        wo_ref[...] = w_ref[...].astype(wo_ref.dtype)

    qi = pl.program_id(2)
    m_ref[...] = jnp.full(m_ref.shape, NEG, F32)
    acc_ref[...] = jnp.zeros(acc_ref.shape, F32)
    ones = jnp.ones((tq, DENOM_ROWS), BF16)

    def block(j, masked):
        r0 = pl.multiple_of(j * tq, tq)
        for h in range(hg):
            q = q_ref[:, h * HEAD_PAD:(h + 1) * HEAD_PAD]
            kb = k_ref[pl.ds(r0, tq), h * HEAD_PAD:(h + 1) * HEAD_PAD]
            vb = v_ref[pl.ds(r0, tq), h * V_HEAD_DIM:(h + 1) * V_HEAD_DIM]
            st = lax.dot_general(kb, q, (((1,), (1,)), ((), ())), preferred_element_type=F32)
            if masked:
                key = lax.broadcasted_iota(jnp.int32, st.shape, 0) // CHUNK
                qry = lax.broadcasted_iota(jnp.int32, st.shape, 1) // CHUNK
                st = jnp.where(key <= qry, st, NEG)
            m_old = m_ref[h]
            m_new = jnp.maximum(m_old, jnp.max(st, axis=0, keepdims=True))
            alpha = jnp.exp2(m_old - m_new)
            p = jnp.exp2((st - m_new).astype(BF16))
            v_ext = jnp.concatenate([vb, ones], axis=1)
            pv = lax.dot_general(v_ext, p, (((0,), (0,)), ((), ())), preferred_element_type=F32)
            acc_ref[h] = alpha * acc_ref[h] + pv
            m_ref[h] = m_new

    def body(j, c):
        block(j, False)
        return c

    lax.fori_loop(0, qi, body, 0)
    block(qi, True)
    for h in range(hg):
        o = acc_ref[h, :V_HEAD_DIM] / acc_ref[h, V_HEAD_DIM:V_HEAD_DIM + 1]
        o_ref[:, h * V_HEAD_DIM:(h + 1) * V_HEAD_DIM] = o.T.astype(o_ref.dtype)


def _attention(q3, k3, v3, side_weights, *, tq=1024, hg=2):
    b, s, _ = q3.shape
    tq = min(tq, s)
    ng, nq = N_HEADS // hg, s // tq
    steps = b * ng * nq
    w2 = [w.reshape(-1, w.shape[-1]) for w in side_weights]
    slab = [w.shape[0] // steps for w in w2]
    step = lambda bi, h, i: ((bi * ng + h) * nq + i, 0)
    wspec = [pl.BlockSpec((r, w.shape[1]), step) for r, w in zip(slab, w2)]
    outs = pl.pallas_call(
        functools.partial(_attn_kernel, tq=tq, hg=hg, n_side=len(w2)),
        grid=(b, ng, nq),
        in_specs=[
            pl.BlockSpec((None, tq, hg * HEAD_PAD), lambda bi, h, i: (bi, i, h)),
            pl.BlockSpec((None, s, hg * HEAD_PAD), lambda bi, h, i: (bi, 0, h)),
            pl.BlockSpec((None, s, hg * V_HEAD_DIM), lambda bi, h, i: (bi, 0, h)),
        ] + wspec,
        out_specs=[pl.BlockSpec((None, tq, hg * V_HEAD_DIM), lambda bi, h, i: (bi, i, h))] + wspec,
        out_shape=[jax.ShapeDtypeStruct((b, s, N_HEADS * V_HEAD_DIM), BF16)]
        + [jax.ShapeDtypeStruct(w.shape, BF16) for w in w2],
        scratch_shapes=[pltpu.VMEM((hg, 1, tq), F32), pltpu.VMEM((hg, V_HEAD_DIM + DENOM_ROWS, tq), F32)],
        compiler_params=_cparams(("parallel", "parallel", "arbitrary")),
        name="attention",
    )(q3, k3, v3, *w2)
    return (outs[0],) + tuple(o.reshape(w.shape) for o, w in zip(outs[1:], side_weights))


def _s5_kernel(u_ref, perm_ref, permt_ref, bb_ref, cc_ref, a_ref, d_ref, wglu_ref, wo_ref, o_ref,
               ut_ref, bu_ref, y_ref, carry_ref, *, tc):
    nb = u_ref.shape[0]
    rows_sub = nb * SUB_T
    nsub = tc // SUB_T

    @pl.when(pl.program_id(0) == 0)
    def _():
        carry_ref[...] = jnp.zeros(carry_ref.shape, F32)

    for j in range(nsub):
        ub = u_ref[:, j * SUB_T:(j + 1) * SUB_T, :].reshape(rows_sub, S5_WIDTH)
        ut_ref[j * rows_sub:(j + 1) * rows_sub, :] = jnp.dot(
            perm_ref[...], ub, preferred_element_type=F32).astype(BF16)

    for c in range(N_CLUSTERS):
        bu_ref[...] = jnp.dot(ut_ref[:, c * LANES:(c + 1) * LANES], bb_ref[c], preferred_element_type=F32)
        ar = a_ref[c, :, :CLUSTER_STATES]
        ai = a_ref[c, :, CLUSTER_STATES:]

        def step(t, carry):
            xr, xi = carry
            r0 = pl.multiple_of(t * nb, nb)
            br = bu_ref[pl.ds(r0, nb), :CLUSTER_STATES]
            bi = bu_ref[pl.ds(r0, nb), CLUSTER_STATES:]
            nxr = ar * xr - ai * xi + br
            nxi = ar * xi + ai * xr + bi
            bu_ref[pl.ds(r0, nb), :CLUSTER_STATES] = nxr
            bu_ref[pl.ds(r0, nb), CLUSTER_STATES:] = nxi
            return nxr, nxi

        xr, xi = lax.fori_loop(0, tc, step, (carry_ref[c, :, :CLUSTER_STATES], carry_ref[c, :, CLUSTER_STATES:]),
                               unroll=8)
        carry_ref[c, :, :CLUSTER_STATES] = xr
        carry_ref[c, :, CLUSTER_STATES:] = xi
        y_ref[:, c * LANES:(c + 1) * LANES] = jnp.dot(bu_ref[...].astype(BF16), cc_ref[c], preferred_element_type=F32)

    y = y_ref[...] + d_ref[...] * ut_ref[...].astype(F32)
    g = jax.nn.gelu(y)
    z = g * jax.nn.sigmoid(jnp.dot(g.astype(BF16), wglu_ref[...], preferred_element_type=F32))
    yb = jnp.dot(z.astype(BF16), wo_ref[...], preferred_element_type=F32).astype(BF16)
    for j in range(nsub):
        blk = jnp.dot(permt_ref[...], yb[j * rows_sub:(j + 1) * rows_sub, :], preferred_element_type=F32)
        o_ref[:, j * SUB_T:(j + 1) * SUB_T, :] = blk.astype(o_ref.dtype).reshape(nb, SUB_T, o_ref.shape[2])


def _s5(proj3, perm, permt, bb, cc, a_bc, d, wglu, wo, *, tc=128):
    b, s, _ = proj3.shape
    rows = b * tc
    dm = wo.shape[1]
    c2 = lambda shape: pl.BlockSpec(shape, lambda i: (0, 0))
    c3 = lambda shape: pl.BlockSpec(shape, lambda i: (0, 0, 0))
    return pl.pallas_call(
        functools.partial(_s5_kernel, tc=tc),
        grid=(s // tc,),
        in_specs=[
            pl.BlockSpec((b, tc, S5_WIDTH), lambda i: (0, i, 4)),
            c2(perm.shape), c2(permt.shape), c3(bb.shape), c3(cc.shape), c3(a_bc.shape), c2(d.shape),
            c2(wglu.shape), c2(wo.shape),
        ],
        out_specs=pl.BlockSpec((b, tc, dm), lambda i: (0, i, 0)),
        out_shape=jax.ShapeDtypeStruct((b, s, dm), BF16),
        scratch_shapes=[
            pltpu.VMEM((rows, S5_WIDTH), BF16),
            pltpu.VMEM((rows, 2 * CLUSTER_STATES), F32),
            pltpu.VMEM((rows, S5_WIDTH), F32),
            pltpu.VMEM((N_CLUSTERS, b, 2 * CLUSTER_STATES), F32),
        ],
        compiler_params=_cparams(("arbitrary",)),
        name="s5",
    )(proj3, perm, permt, bb, cc, a_bc, d, wglu, wo)


def _merge_kernel(o_ref, gl_ref, bg_ref, yb_ref, w_ref, m_ref):
    d = o_ref.shape[1]
    ya = jnp.dot(o_ref[...], w_ref[...], preferred_element_type=F32)
    ga = jax.nn.sigmoid(gl_ref[:, :d].astype(F32) + bg_ref[:, :d])
    gb = jax.nn.sigmoid(gl_ref[:, d:].astype(F32) + bg_ref[:, d:])
    m_ref[...] = (ga * ya + gb * yb_ref[...].astype(F32)).astype(m_ref.dtype)


def _merge(o2, proj, bg, yb2, w, *, tm=512):
    t, d = o2.shape
    return pl.pallas_call(
        _merge_kernel,
        grid=(t // tm,),
        in_specs=[
            pl.BlockSpec((tm, d), lambda i: (i, 0)),
            pl.BlockSpec((tm, 2 * d), lambda i: (i, 0)),
            pl.BlockSpec((1, 2 * d), lambda i: (0, 0)),
            pl.BlockSpec((tm, d), lambda i: (i, 0)),
            pl.BlockSpec((d, d), lambda i: (0, 0)),
        ],
        out_specs=pl.BlockSpec((tm, d), lambda i: (i, 0)),
        out_shape=jax.ShapeDtypeStruct((t, d), BF16),
        compiler_params=_cparams(("parallel",)),
        name="merge",
    )(o2, proj, bg, yb2, w)


def _to_row_tiles(x):
    return x.reshape(x.shape[0], x.shape[1] // LANES, LANES)


def _from_row_tiles(x):
    return x.reshape(x.shape[0], x.shape[1] * LANES)


def _outproj_kernel(m_ref, x_ref, w_ref, g_ref, wrh_ref, wrl_ref, br_ref, x1_ref, h2_ref, idx_ref, wt_ref):
    x1 = x_ref[...] + jnp.dot(m_ref[...], w_ref[...], preferred_element_type=F32)
    x1_ref[...] = x1
    ms = jnp.mean(x1 * x1, axis=-1, keepdims=True)
    h2 = x1 * lax.rsqrt(ms + EPS) * g_ref[...]
    h_hi = h2.astype(BF16)
    h2_ref[...] = _to_row_tiles(h_hi)
    h_lo = (h2 - h_hi.astype(F32)).astype(BF16)
    logits = (jnp.dot(h_hi, wrh_ref[...], preferred_element_type=F32)
              + jnp.dot(h_lo, wrh_ref[...], preferred_element_type=F32)
              + jnp.dot(h_hi, wrl_ref[...], preferred_element_type=F32)) + br_ref[...]
    lane = lax.broadcasted_iota(jnp.int32, logits.shape, 1)
    work = jnp.where(lane < N_EXPERTS, logits, -jnp.inf)
    idx_out = jnp.zeros(logits.shape, jnp.int32)
    val_out = jnp.zeros(logits.shape, F32)
    v0 = None
    denom = None
    for k in range(TOP_K):
        mx = jnp.max(work, axis=-1, keepdims=True)
        sel = jnp.min(jnp.where(work == mx, lane, LANES), axis=-1, keepdims=True)
        if k == 0:
            v0 = mx
        e = jnp.exp(mx - v0)
        denom = e if k == 0 else denom + e
        idx_out = jnp.where(lane == k, sel, idx_out)
        val_out = jnp.where(lane == k, e, val_out)
        work = jnp.where(lane == sel, -jnp.inf, work)
    idx_ref[...] = idx_out
    wt_ref[...] = val_out / denom


def _outproj(m2, x2, w, g, wrh, wrl, br, *, tm=512):
    t, d = x2.shape
    c2 = lambda shape: pl.BlockSpec(shape, lambda i: (0, 0))
    row = lambda width: pl.BlockSpec((tm, width), lambda i: (i, 0))
    return pl.pallas_call(
        _outproj_kernel,
        grid=(t // tm,),
        in_specs=[row(d), row(d), c2((d, d)), c2((1, d)), c2((d, LANES)), c2((d, LANES)), c2((1, LANES))],
        out_specs=[row(d), pl.BlockSpec((tm, d // LANES, LANES), lambda i: (i, 0, 0)), row(LANES), row(LANES)],
        out_shape=[
            jax.ShapeDtypeStruct((t, d), F32),
            jax.ShapeDtypeStruct((t, d // LANES, LANES), BF16),
            jax.ShapeDtypeStruct((t, LANES), jnp.int32),
            jax.ShapeDtypeStruct((t, LANES), F32),
        ],
        compiler_params=_cparams(("parallel",)),
        name="outproj",
    )(m2, x2, w, g, wrh, wrl, br)


def _slot_tokens_kernel(slot_ref, o_ref, *, chunk):
    i = pl.program_id(0)

    @pl.when(i == 0)
    def _():
        def clear(j, c):
            o_ref[j] = 0
            return c

        lax.fori_loop(0, o_ref.shape[0], clear, 0, unroll=32)

    tokens = chunk // TOP_K
    first_tok = i * tokens

    def place(t, c):
        for k in range(TOP_K):
            o_ref[slot_ref[t * TOP_K + k]] = first_tok + t
        return c

    lax.fori_loop(0, tokens, place, 0, unroll=4)


def _slot_tokens(slot, n_slots, *, chunk=8192):
    n = slot.shape[0]
    chunk = min(chunk, n)
    return pl.pallas_call(
        functools.partial(_slot_tokens_kernel, chunk=chunk),
        grid=(n // chunk,),
        in_specs=[pl.BlockSpec((chunk,), lambda i: (i,), memory_space=pltpu.SMEM)],
        out_specs=pl.BlockSpec(memory_space=pltpu.SMEM),
        out_shape=jax.ShapeDtypeStruct((n_slots,), jnp.int32),
        compiler_params=_cparams(("arbitrary",)),
        name="slot_tokens",
    )(slot)


BOTH_DMA_PRIORITIES = (0, 1)
ROW_QUARTERS = 4


def _start_row_gather(idx_ref, n_rows, src_hbm, dst_ref, sem, priorities=BOTH_DMA_PRIORITIES):
    n_pri = len(priorities)

    def issue(r2, c):
        for u, pri in enumerate(priorities):
            r = r2 * n_pri + u
            pltpu.make_async_copy(src_hbm.at[pl.ds(idx_ref[0, 0, r], 1)], dst_ref.at[pl.ds(r, 1)], sem).start(priority=pri)
        return c

    lax.fori_loop(0, n_rows // n_pri, issue, 0, unroll=8 // n_pri)


def _wait_rows(dst_ref, sem):
    pltpu.make_async_copy(dst_ref, dst_ref, sem).wait()


def _expert_kernel(be_ref, nu_ref, nv_ref, tokc_ref, tokn_ref, h_hbm, wg_ref, bg_ref, wu_ref, bu_ref, wd_ref, bd_ref,
                   o_ref, xbuf_ref, xb_ref, acc_ref, sem, *, tm):
    b = pl.program_id(0)
    f = pl.program_id(1)
    nxt = b + 1

    @pl.when(jnp.logical_and(f == 0, jnp.logical_and(b == 0, nu_ref[0] > 0)))
    def _():
        _start_row_gather(tokc_ref, tm, h_hbm, xbuf_ref.at[0], sem.at[0], priorities=(0,))

    @pl.when(jnp.logical_and(f == 0, jnp.logical_and(nxt < pl.num_programs(0), nxt < nu_ref[0])))
    def _():
        _start_row_gather(tokn_ref, tm, h_hbm, xbuf_ref.at[nxt % 2], sem.at[nxt % 2], priorities=(0,))

    @pl.when(b < nu_ref[0])
    def _():
        @pl.when(f == 0)
        def _():
            _wait_rows(xbuf_ref.at[b % 2], sem.at[b % 2])
            xb_ref[...] = _from_row_tiles(xbuf_ref[b % 2])
            acc_ref[...] = jnp.zeros(acc_ref.shape, F32)

        def ffn(rows):
            xb = xb_ref[:rows]
            gate = jnp.dot(xb, wg_ref[0], preferred_element_type=F32) + bg_ref[0]
            up = jnp.dot(xb, wu_ref[0], preferred_element_type=F32) + bu_ref[0]
            gate = jnp.minimum(gate, SWIGLU_LIMIT)
            up = jnp.clip(up, -SWIGLU_LIMIT, SWIGLU_LIMIT)
            glu = gate * jax.nn.sigmoid(SWIGLU_ALPHA * gate)
            act = ((up + 1.0) * glu).astype(BF16)
            acc_ref[:rows] += jnp.dot(act, wd_ref[0].astype(BF16), preferred_element_type=F32)

            @pl.when(f == pl.num_programs(1) - 1)
            def _():
                o_ref[:rows] = _to_row_tiles((acc_ref[:rows] + bd_ref[0]).astype(o_ref.dtype))
                if rows < tm:
                    o_ref[rows:] = jnp.zeros((tm - rows,) + o_ref.shape[1:], o_ref.dtype)

        nv = nv_ref[b]
        quarter = tm // ROW_QUARTERS
        for i in range(1, ROW_QUARTERS + 1):
            lo_ok = nv > (i - 1) * quarter if i > 1 else True
            hi_ok = nv <= i * quarter if i < ROW_QUARTERS else True
            pl.when(jnp.logical_and(lo_ok, hi_ok))(functools.partial(ffn, i * quarter))

    @pl.when(jnp.logical_and(b >= nu_ref[0], f == 0))
    def _():
        o_ref[...] = jnp.zeros(o_ref.shape, o_ref.dtype)


def _experts(block_expert, nused, nvalid, slot_tok, h2, wg, bg, wu, bu, wd, bd, *, tm, tf=1024):
    n_slots = slot_tok.shape[0]
    tile = h2.shape[1:]
    d = tile[0] * tile[1]
    nb = n_slots // tm
    dff = wg.shape[2]
    nf = dff // tf
    tok3 = slot_tok.reshape(nb, 1, tm)

    def blk(b, nu):
        return jnp.minimum(b, nu[0] - 1)

    def fidx(b, f, nu):
        return jnp.where(b < nu[0], f, nf - 1)

    return pl.pallas_call(
        functools.partial(_expert_kernel, tm=tm),
        grid_spec=pltpu.PrefetchScalarGridSpec(
            num_scalar_prefetch=3,
            grid=(nb, nf),
            in_specs=[
                pl.BlockSpec((1, 1, tm), lambda b, f, be, nu, nv: (b, 0, 0), memory_space=pltpu.SMEM),
                pl.BlockSpec((1, 1, tm), lambda b, f, be, nu, nv: (jnp.minimum(b + 1, nb - 1), 0, 0),
                             memory_space=pltpu.SMEM),
                pl.BlockSpec(memory_space=pl.ANY),
                pl.BlockSpec((1, d, tf), lambda b, f, be, nu, nv: (be[blk(b, nu)], 0, fidx(b, f, nu))),
                pl.BlockSpec((1, 1, tf), lambda b, f, be, nu, nv: (be[blk(b, nu)], 0, fidx(b, f, nu))),
                pl.BlockSpec((1, d, tf), lambda b, f, be, nu, nv: (be[blk(b, nu)], 0, fidx(b, f, nu))),
                pl.BlockSpec((1, 1, tf), lambda b, f, be, nu, nv: (be[blk(b, nu)], 0, fidx(b, f, nu))),
                pl.BlockSpec((1, tf, d), lambda b, f, be, nu, nv: (be[blk(b, nu)], fidx(b, f, nu), 0)),
                pl.BlockSpec((1, 1, d), lambda b, f, be, nu, nv: (be[blk(b, nu)], 0, 0)),
            ],
            out_specs=pl.BlockSpec((tm,) + tile, lambda b, f, be, nu, nv: (b, 0, 0)),
            scratch_shapes=[pltpu.VMEM((2, tm) + tile, h2.dtype), pltpu.VMEM((tm, d), BF16), pltpu.VMEM((tm, d), F32),
                            pltpu.SemaphoreType.DMA((2,))],
        ),
        out_shape=jax.ShapeDtypeStruct((n_slots,) + tile, BF16),
        compiler_params=_cparams(("arbitrary", "arbitrary")),
        name="experts",
    )(block_expert, nused, nvalid, tok3, tok3, h2, wg, bg, wu, bu, wd, bd)


def _combine_kernel(posc_ref, posn_ref, x1_ref, wt_ref, ys_hbm, o_ref, buf_ref, sem, *, tm):
    i = pl.program_id(0)
    nxt = i + 1

    @pl.when(i == 0)
    def _():
        _start_row_gather(posc_ref, TOP_K * tm, ys_hbm, buf_ref.at[0], sem.at[0])

    @pl.when(nxt < pl.num_programs(0))
    def _():
        _start_row_gather(posn_ref, TOP_K * tm, ys_hbm, buf_ref.at[nxt % 2], sem.at[nxt % 2])

    slot = i % 2
    _wait_rows(buf_ref.at[slot], sem.at[slot])
    acc = x1_ref[...]
    for k in range(TOP_K):
        rows = buf_ref[slot, k * tm:(k + 1) * tm]
        acc = acc + wt_ref[:, k:k + 1] * _from_row_tiles(rows).astype(F32)
    o_ref[...] = acc


def _combine(pos_km, x1, wt, ys, *, tm=128):
    t, d = x1.shape
    n = t // tm
    return pl.pallas_call(
        functools.partial(_combine_kernel, tm=tm),
        grid=(n,),
        in_specs=[
            pl.BlockSpec((1, 1, TOP_K * tm), lambda i: (i, 0, 0), memory_space=pltpu.SMEM),
            pl.BlockSpec((1, 1, TOP_K * tm), lambda i: (jnp.minimum(i + 1, n - 1), 0, 0), memory_space=pltpu.SMEM),
            pl.BlockSpec((tm, d), lambda i: (i, 0)),
            pl.BlockSpec((tm, LANES), lambda i: (i, 0)),
            pl.BlockSpec(memory_space=pl.ANY),
        ],
        out_specs=pl.BlockSpec((tm, d), lambda i: (i, 0)),
        out_shape=jax.ShapeDtypeStruct((t, d), F32),
        scratch_shapes=[pltpu.VMEM((2, TOP_K * tm) + ys.shape[1:], ys.dtype), pltpu.SemaphoreType.DMA((2,))],
        compiler_params=_cparams(("arbitrary",)),
        name="combine",
    )(pos_km, pos_km, x1, wt, ys)


def _pad_heads(w, per_head):
    lead = w.shape[:-1]
    w = w.reshape(lead + (N_HEADS, per_head))
    w = jnp.pad(w, [(0, 0)] * len(lead) + [(0, 0), (0, HEAD_PAD - per_head)])
    return w.reshape(lead + (N_HEADS * HEAD_PAD,))


def _s5_discretise(lam_re, lam_im, log_dt, b_re, b_im):
    dt = jnp.exp(log_dt.astype(F32))[:, None]
    lr, li = lam_re.astype(F32), lam_im.astype(F32)
    mag = jnp.exp(lr * dt)
    ar, ai = mag * jnp.cos(li * dt), mag * jnp.sin(li * dt)
    den = lr * lr + li * li
    zr = ((ar - 1.0) * lr + ai * li) / den
    zi = (ai * lr - (ar - 1.0) * li) / den
    br, bi = b_re.astype(F32), b_im.astype(F32)
    bbr = zr[..., None] * br - zi[..., None] * bi
    bbi = zr[..., None] * bi + zi[..., None] * br
    return ar, ai, bbr, bbi


def _s5_pack(ar, ai, bbr, bbi, c_re, c_im, nb):
    eye = jnp.eye(S5_CLUSTER, dtype=F32)

    def pack_b(m):
        m4 = m.reshape(N_CLUSTERS, S5_CLUSTER, S5_STATE, S5_GROUP)
        return jnp.einsum('xgpc,gh->xgchp', m4, eye).reshape(N_CLUSTERS, S5_CLUSTER * S5_GROUP, CLUSTER_STATES)

    def pack_c(m):
        m4 = m.reshape(N_CLUSTERS, S5_CLUSTER, S5_GROUP, S5_STATE)
        return jnp.einsum('xgcp,gh->xgphc', m4, eye).reshape(N_CLUSTERS, CLUSTER_STATES, S5_CLUSTER * S5_GROUP)

    bb = jnp.concatenate([pack_b(bbr), pack_b(bbi)], axis=2).astype(BF16)
    cc = jnp.concatenate([pack_c(c_re.astype(F32)), -pack_c(c_im.astype(F32))], axis=1).astype(BF16)
    a = jnp.concatenate([ar.reshape(N_CLUSTERS, CLUSTER_STATES), ai.reshape(N_CLUSTERS, CLUSTER_STATES)], axis=1)
    a_bc = jnp.broadcast_to(a[:, None, :], (N_CLUSTERS, nb, 2 * CLUSTER_STATES))
    return bb, cc, a_bc


def _perm_matrix(nb):
    n = nb * SUB_T
    p = np.zeros((n, n), np.float32)
    for b in range(nb):
        for t in range(SUB_T):
            p[t * nb + b, b * SUB_T + t] = 1.0
    return p


def kernel(x, positions, norm1_g, w_in, b_gates, q_norm_g, w_uq, kv_norm_g, w_ukv, qk_norm_q_g, qk_norm_k_g, w_o_mla, s5_lambda_re, s5_lambda_im, s5_log_dt, s5_b_re, s5_b_im, s5_c_re, s5_c_im, s5_d, w_glu, w_o_s5, w_out, norm2_g, w_router, b_router, w_gate, b_gate, w_up, b_up, w_down, b_down):
    bsz, seq, d = x.shape
    t = bsz * seq
    depth = norm1_g.shape[0]
    o1 = Q_LORA_RANK
    o2 = o1 + KV_LORA_RANK
    o3 = o2 + QK_ROPE_DIM
    o4 = o3 + S5_WIDTH
    half = ROPE_HALF
    inv_freq = ROPE_THETA ** (-jnp.arange(half, dtype=F32) / half)
    lane = np.arange(LANES)
    freq = jnp.where(lane < QK_ROPE_DIM, jnp.tile(inv_freq, LANES // half), 0.0).reshape(1, LANES).astype(F32)
    sign = jnp.asarray(np.where(lane % QK_ROPE_DIM < half, -1.0, 1.0).reshape(1, LANES), F32)
    pos = positions.reshape(t, 1).astype(F32)
    sm_scale = math.log2(math.e) / math.sqrt(QK_HEAD_DIM)
    ind_q = jnp.asarray(np.equal.outer(np.arange(N_HEADS * HEAD_PAD) // HEAD_PAD, lane), BF16)
    ind_k = jnp.asarray(np.equal.outer(np.arange(N_HEADS * QK_NOPE_DIM) // QK_NOPE_DIM, lane), BF16)
    perm_np = _perm_matrix(bsz)
    perm = jnp.asarray(perm_np, BF16)
    permt = jnp.asarray(perm_np.T, BF16)
    tm_e = 512
    n_assign = t * TOP_K
    nb_e = n_assign // tm_e + N_EXPERTS
    n_slots = nb_e * tm_e
    tm_c = 256

    for l in range(depth):
        wi = w_in[l]
        w_in_p = jnp.concatenate(
            [wi[:, o4:], wi[:, o3:o4], wi[:, :o1], wi[:, o1:o2], wi[:, o2:o3],
             jnp.zeros((d, PROJ_W - wi.shape[1]), wi.dtype)], axis=1).astype(BF16)
        w_uq_p = _pad_heads(w_uq[l], QK_HEAD_DIM).astype(BF16)
        gq_full = _pad_heads(jnp.tile(qk_norm_q_g[l].astype(F32), N_HEADS) * sm_scale, QK_HEAD_DIM).reshape(1, -1)
        wkv = w_ukv[l].reshape(KV_LORA_RANK, N_HEADS, QK_NOPE_DIM + V_HEAD_DIM)
        w_kv_p = jnp.concatenate(
            [wkv[:, :, :QK_NOPE_DIM].reshape(KV_LORA_RANK, -1), wkv[:, :, QK_NOPE_DIM:].reshape(KV_LORA_RANK, -1)],
            axis=1).astype(BF16)
        gk = qk_norm_k_g[l].astype(F32)
        gk_nope = gk[:QK_NOPE_DIM].reshape(1, LANES)
        gk_rope = jnp.pad(gk[QK_NOPE_DIM:], (0, LANES - QK_ROPE_DIM)).reshape(1, LANES)
        ar, ai, bbr, bbi = _s5_discretise(s5_lambda_re[l], s5_lambda_im[l], s5_log_dt[l], s5_b_re[l], s5_b_im[l])
        bb, cc, a_bc = _s5_pack(ar, ai, bbr, bbi, s5_c_re[l], s5_c_im[l], bsz)
        wr = jnp.pad(w_router[l].astype(F32), ((0, 0), (0, LANES - N_EXPERTS)))
        wr_hi = wr.astype(BF16)
        wr_lo = (wr - wr_hi.astype(F32)).astype(BF16)
        br = jnp.pad(b_router[l].astype(F32), (0, LANES - N_EXPERTS)).reshape(1, LANES)

        x2 = x.reshape(t, d)
        proj = _inproj(x2, norm1_g[l].reshape(1, d), w_in_p)
        q = _qprep(proj, pos, q_norm_g[l].reshape(1, -1), w_uq_p, gq_full, ind_q, freq, sign)
        k, v = _kprep(proj, pos, kv_norm_g[l].reshape(1, -1), w_kv_p, gk_nope, gk_rope, ind_k, freq, sign)
        o, wg_b, wu_b = _attention(q.reshape(bsz, seq, -1), k.reshape(bsz, seq, -1), v.reshape(bsz, seq, -1),
                                   (w_gate[l], w_up[l]))
        yb = _s5(proj.reshape(bsz, seq, PROJ_W), perm, permt, bb, cc, a_bc, s5_d[l].reshape(1, -1).astype(F32),
                 w_glu[l].astype(BF16), w_o_s5[l].astype(BF16))
        m = _merge(o.reshape(t, -1), proj, b_gates[l].reshape(1, -1).astype(F32), yb.reshape(t, d),
                   w_o_mla[l].astype(BF16))
        x1, h2, top_idx, top_w = _outproj(m, x2, w_out[l].astype(BF16), norm2_g[l].reshape(1, d), wr_hi, wr_lo, br)

        flat_e = top_idx[:, :TOP_K].reshape(-1)
        onehot = (flat_e[:, None] == jnp.arange(N_EXPERTS, dtype=jnp.int32)[None, :]).astype(jnp.int32)
        csum = jnp.cumsum(onehot, axis=0)
        rank = jnp.take_along_axis(csum, flat_e[:, None], axis=1)[:, 0] - 1
        counts = csum[-1]
        nblk = (counts + tm_e - 1) // tm_e
        blk_end = jnp.cumsum(nblk)
        blk_start = blk_end - nblk
        slot = blk_start[flat_e] * tm_e + rank
        nused = blk_end[-1:].astype(jnp.int32)
        block_expert = jnp.minimum(
            jnp.sum((blk_end[None, :] <= jnp.arange(nb_e, dtype=jnp.int32)[:, None]).astype(jnp.int32), axis=1),
            N_EXPERTS - 1).astype(jnp.int32)
        slot_tok = _slot_tokens(slot.astype(jnp.int32), n_slots)
        pos_km = slot.reshape(t // tm_c, tm_c, TOP_K).transpose(0, 2, 1).reshape(t // tm_c, 1, TOP_K * tm_c)

        blk_ids = jnp.arange(nb_e, dtype=jnp.int32)
        nvalid = jnp.clip(counts[block_expert] - (blk_ids - blk_start[block_expert]) * tm_e, 0, tm_e).astype(jnp.int32)
        ys = _experts(block_expert, nused, nvalid, slot_tok, h2,
                      wg_b, b_gate[l].reshape(N_EXPERTS, 1, -1).astype(F32),
                      wu_b, b_up[l].reshape(N_EXPERTS, 1, -1).astype(F32),
                      w_down[l], b_down[l].reshape(N_EXPERTS, 1, -1).astype(F32), tm=tm_e)
        x = _combine(pos_km, x1, top_w, ys, tm=tm_c).reshape(bsz, seq, d)
    return x
```

```python
import functools
import math

import numpy as np
import jax
import jax.numpy as jnp
from jax import lax
from jax.experimental import pallas as pl
from jax.experimental.pallas import tpu as pltpu

F32 = jnp.float32
BF16 = jnp.bfloat16

D_MODEL = 2048
CHUNK = 64
EPS = 1e-6
N_HEADS = 16
QK_NOPE_DIM = 128
QK_ROPE_DIM = 64
QK_HEAD_DIM = QK_NOPE_DIM + QK_ROPE_DIM
V_HEAD_DIM = 128
Q_LORA_RANK = 512
KV_LORA_RANK = 256
ROPE_THETA = 10000.0
S5_WIDTH = 1024
S5_GROUP = 16
S5_GROUPS = S5_WIDTH // S5_GROUP
S5_STATE = 64
N_EXPERTS = 32
TOP_K = 4
D_FF = 2048
SWIGLU_LIMIT = 7.0
SWIGLU_ALPHA = 1.702

LANES = 128
HEAD_PAD = 256
ROPE_HALF = QK_ROPE_DIM // 2
S5_CLUSTER = 8
N_CLUSTERS = S5_GROUPS // S5_CLUSTER
CLUSTER_STATES = S5_CLUSTER * S5_STATE
SUB_T = 32
PROJ_W = 6144
VMEM_LIMIT = 56 * 1024 * 1024
NEG = -1e30
DENOM_ROWS = 8


def _cparams(sem, vmem=VMEM_LIMIT, **kw):
    return pltpu.CompilerParams(dimension_semantics=sem, vmem_limit_bytes=vmem, **kw)


def _inproj_kernel(x_ref, g_ref, w_ref, o_ref, h_ref):
    @pl.when(pl.program_id(1) == 0)
    def _():
        x = x_ref[...]
        ms = jnp.mean(x * x, axis=-1, keepdims=True)
        h_ref[...] = (x * lax.rsqrt(ms + EPS) * g_ref[...]).astype(BF16)

    o_ref[...] = jnp.dot(h_ref[...], w_ref[...], preferred_element_type=F32).astype(o_ref.dtype)


def _inproj(x2, g, w, *, tm=1024, tn=1536):
    t, d = x2.shape
    n = w.shape[1]
    return pl.pallas_call(
        _inproj_kernel,
        grid=(t // tm, n // tn),
        in_specs=[
            pl.BlockSpec((tm, d), lambda i, j: (i, 0)),
            pl.BlockSpec((1, d), lambda i, j: (0, 0)),
            pl.BlockSpec((d, tn), lambda i, j: (0, j)),
        ],
        out_specs=pl.BlockSpec((tm, tn), lambda i, j: (i, j)),
        out_shape=jax.ShapeDtypeStruct((t, n), BF16),
        scratch_shapes=[pltpu.VMEM((tm, d), BF16)],
        compiler_params=_cparams(("parallel", "arbitrary")),
        name="inproj",
    )(x2, g, w)


def _rope_mid(mid, cos, sin_signed):
    lane = lax.broadcasted_iota(jnp.int32, mid.shape, 1)
    rot = jnp.where(lane < ROPE_HALF, pltpu.roll(mid, LANES - ROPE_HALF, 1), pltpu.roll(mid, ROPE_HALF, 1))
    return mid * cos + rot * sin_signed


def _qprep_kernel(ql_ref, pos_ref, gn_ref, w_ref, gfull_ref, ind_ref, freq_ref, sign_ref, o_ref):
    ql = ql_ref[...].astype(F32)
    ms = jnp.mean(ql * ql, axis=-1, keepdims=True)
    qn = (ql * lax.rsqrt(ms + EPS) * gn_ref[...]).astype(BF16)
    q = jnp.dot(qn, w_ref[...], preferred_element_type=F32)
    ssq = jnp.dot((q * q).astype(BF16), ind_ref[...], preferred_element_type=F32)
    sc = lax.rsqrt(ssq * (1.0 / QK_HEAD_DIM) + EPS)
    ang = pos_ref[...] * freq_ref[...]
    cos = jnp.cos(ang)
    sin_signed = jnp.sin(ang) * sign_ref[...]
    for h in range(N_HEADS):
        s_h = sc[:, h:h + 1]
        lo = h * HEAD_PAD
        nope = q[:, lo:lo + LANES] * s_h * gfull_ref[:, lo:lo + LANES]
        mid = q[:, lo + LANES:lo + HEAD_PAD] * s_h * gfull_ref[:, lo + LANES:lo + HEAD_PAD]
        o_ref[:, lo:lo + LANES] = nope.astype(o_ref.dtype)
        o_ref[:, lo + LANES:lo + HEAD_PAD] = _rope_mid(mid, cos, sin_signed).astype(o_ref.dtype)


def _qprep(proj, pos, gn, w, gfull, ind, freq, sign, *, tm=512):
    t = proj.shape[0]
    n = N_HEADS * HEAD_PAD
    full = lambda shape: pl.BlockSpec(shape, lambda i: (0, 0))
    return pl.pallas_call(
        _qprep_kernel,
        grid=(t // tm,),
        in_specs=[
            pl.BlockSpec((tm, Q_LORA_RANK), lambda i: (i, 10)),
            pl.BlockSpec((tm, 1), lambda i: (i, 0)),
            full((1, Q_LORA_RANK)),
            full((Q_LORA_RANK, n)),
            full((1, n)),
            full((n, LANES)),
            full((1, LANES)),
            full((1, LANES)),
        ],
        out_specs=pl.BlockSpec((tm, n), lambda i: (i, 0)),
        out_shape=jax.ShapeDtypeStruct((t, n), BF16),
        compiler_params=_cparams(("parallel",)),
        name="qprep",
    )(proj, pos, gn, w, gfull, ind, freq, sign)


def _kprep_kernel(kvl_ref, kr_ref, pos_ref, gn_ref, w_ref, gnope_ref, grope_ref, ind_ref, freq_ref, sign_ref,
                  k_ref, v_ref):
    kvl = kvl_ref[...].astype(F32)
    ms = jnp.mean(kvl * kvl, axis=-1, keepdims=True)
    kn = (kvl * lax.rsqrt(ms + EPS) * gn_ref[...]).astype(BF16)
    kv = jnp.dot(kn, w_ref[...], preferred_element_type=F32)
    nd = N_HEADS * QK_NOPE_DIM
    knope = kv[:, :nd]
    v_ref[...] = kv[:, nd:].astype(v_ref.dtype)
    kr = kr_ref[...].astype(F32)
    ssq = jnp.dot((knope * knope).astype(BF16), ind_ref[...], preferred_element_type=F32)
    ssq = ssq + jnp.sum(kr * kr, axis=-1, keepdims=True)
    sc = lax.rsqrt(ssq * (1.0 / QK_HEAD_DIM) + EPS)
    ang = pos_ref[...] * freq_ref[...]
    kr_rot = _rope_mid(kr * grope_ref[...], jnp.cos(ang), jnp.sin(ang) * sign_ref[...])
    for h in range(N_HEADS):
        s_h = sc[:, h:h + 1]
        lo = h * HEAD_PAD
        nope = knope[:, h * LANES:(h + 1) * LANES] * s_h * gnope_ref[...]
        k_ref[:, lo:lo + LANES] = nope.astype(k_ref.dtype)
        k_ref[:, lo + LANES:lo + HEAD_PAD] = (kr_rot * s_h).astype(k_ref.dtype)


def _kprep(proj, pos, gn, w, gnope, grope, ind, freq, sign, *, tm=512):
    t = proj.shape[0]
    n = N_HEADS * HEAD_PAD
    nd = N_HEADS * QK_NOPE_DIM
    full = lambda shape: pl.BlockSpec(shape, lambda i: (0, 0))
    return pl.pallas_call(
        _kprep_kernel,
        grid=(t // tm,),
        in_specs=[
            pl.BlockSpec((tm, KV_LORA_RANK), lambda i: (i, 22)),
            pl.BlockSpec((tm, LANES), lambda i: (i, 46)),
            pl.BlockSpec((tm, 1), lambda i: (i, 0)),
            full((1, KV_LORA_RANK)),
            full((KV_LORA_RANK, 2 * nd)),
            full((1, LANES)),
            full((1, LANES)),
            full((nd, LANES)),
            full((1, LANES)),
            full((1, LANES)),
        ],
        out_specs=[pl.BlockSpec((tm, n), lambda i: (i, 0)), pl.BlockSpec((tm, nd), lambda i: (i, 0))],
        out_shape=[jax.ShapeDtypeStruct((t, n), BF16), jax.ShapeDtypeStruct((t, nd), BF16)],
        compiler_params=_cparams(("parallel",)),
        name="kprep",
    )(proj, proj, pos, gn, w, gnope, grope, ind, freq, sign)


def _attn_kernel(q_ref, k_ref, v_ref, *refs, tq, hg, n_side):
    side_in, o_ref, side_out = refs[:n_side], refs[n_side], refs[n_side + 1:2 * n_side + 1]
    m_ref, acc_ref = refs[2 * n_side + 1:]
    for w_ref, wo_ref in zip(side_in, side_out):
        wo_ref[...] = w_ref[...].astype(wo_ref.dtype)

    qi = pl.program_id(2)
    m_ref[...] = jnp.full(m_ref.shape, NEG, F32)
    acc_ref[...] = jnp.zeros(acc_ref.shape, F32)
    ones = jnp.ones((tq, DENOM_ROWS), BF16)

    def block(j, masked):
        r0 = pl.multiple_of(j * tq, tq)
        for h in range(hg):
            q = q_ref[:, h * HEAD_PAD:(h + 1) * HEAD_PAD]
            kb = k_ref[pl.ds(r0, tq), h * HEAD_PAD:(h + 1) * HEAD_PAD]
            vb = v_ref[pl.ds(r0, tq), h * V_HEAD_DIM:(h + 1) * V_HEAD_DIM]
            st = lax.dot_general(kb, q, (((1,), (1,)), ((), ())), preferred_element_type=F32)
            if masked:
                key = lax.broadcasted_iota(jnp.int32, st.shape, 0) // CHUNK
                qry = lax.broadcasted_iota(jnp.int32, st.shape, 1) // CHUNK
                st = jnp.where(key <= qry, st, NEG)
            m_old = m_ref[h]
            m_new = jnp.maximum(m_old, jnp.max(st, axis=0, keepdims=True))
            alpha = jnp.exp2(m_old - m_new)
            p = jnp.exp2((st - m_new).astype(BF16))
            v_ext = jnp.concatenate([vb, ones], axis=1)
            pv = lax.dot_general(v_ext, p, (((0,), (0,)), ((), ())), preferred_element_type=F32)
            acc_ref[h] = alpha * acc_ref[h] + pv
            m_ref[h] = m_new

    def body(j, c):
        block(j, False)
        return c

    lax.fori_loop(0, qi, body, 0)
    block(qi, True)
    for h in range(hg):
        o = acc_ref[h, :V_HEAD_DIM] / acc_ref[h, V_HEAD_DIM:V_HEAD_DIM + 1]
        o_ref[:, h * V_HEAD_DIM:(h + 1) * V_HEAD_DIM] = o.T.astype(o_ref.dtype)


def _attention(q3, k3, v3, side_weights, *, tq=1024, hg=2):
    b, s, _ = q3.shape
    tq = min(tq, s)
    ng, nq = N_HEADS // hg, s // tq
    steps = b * ng * nq
    w2 = [w.reshape(-1, w.shape[-1]) for w in side_weights]
    slab = [w.shape[0] // steps for w in w2]
    step = lambda bi, h, i: ((bi * ng + h) * nq + i, 0)
    wspec = [pl.BlockSpec((r, w.shape[1]), step) for r, w in zip(slab, w2)]
    outs = pl.pallas_call(
        functools.partial(_attn_kernel, tq=tq, hg=hg, n_side=len(w2)),
        grid=(b, ng, nq),
        in_specs=[
            pl.BlockSpec((None, tq, hg * HEAD_PAD), lambda bi, h, i: (bi, i, h)),
            pl.BlockSpec((None, s, hg * HEAD_PAD), lambda bi, h, i: (bi, 0, h)),
            pl.BlockSpec((None, s, hg * V_HEAD_DIM), lambda bi, h, i: (bi, 0, h)),
        ] + wspec,
        out_specs=[pl.BlockSpec((None, tq, hg * V_HEAD_DIM), lambda bi, h, i: (bi, i, h))] + wspec,
        out_shape=[jax.ShapeDtypeStruct((b, s, N_HEADS * V_HEAD_DIM), BF16)]
        + [jax.ShapeDtypeStruct(w.shape, BF16) for w in w2],
        scratch_shapes=[pltpu.VMEM((hg, 1, tq), F32), pltpu.VMEM((hg, V_HEAD_DIM + DENOM_ROWS, tq), F32)],
        compiler_params=_cparams(("parallel", "parallel", "arbitrary")),
        name="attention",
    )(q3, k3, v3, *w2)
    return (outs[0],) + tuple(o.reshape(w.shape) for o, w in zip(outs[1:], side_weights))


def _s5_kernel(u_ref, perm_ref, permt_ref, bb_ref, cc_ref, a_ref, d_ref, wglu_ref, wo_ref, o_ref,
               ut_ref, bu_ref, y_ref, carry_ref, *, tc):
    nb = u_ref.shape[0]
    rows_sub = nb * SUB_T
    nsub = tc // SUB_T

    @pl.when(pl.program_id(0) == 0)
    def _():
        carry_ref[...] = jnp.zeros(carry_ref.shape, F32)

    for j in range(nsub):
        ub = u_ref[:, j * SUB_T:(j + 1) * SUB_T, :].reshape(rows_sub, S5_WIDTH)
        ut_ref[j * rows_sub:(j + 1) * rows_sub, :] = jnp.dot(
            perm_ref[...], ub, preferred_element_type=F32).astype(BF16)

    for c in range(N_CLUSTERS):
        bu_ref[...] = jnp.dot(ut_ref[:, c * LANES:(c + 1) * LANES], bb_ref[c], preferred_element_type=F32)
        ar = a_ref[c, :, :CLUSTER_STATES]
        ai = a_ref[c, :, CLUSTER_STATES:]

        def step(t, carry):
            xr, xi = carry
            r0 = pl.multiple_of(t * nb, nb)
            br = bu_ref[pl.ds(r0, nb), :CLUSTER_STATES]
            bi = bu_ref[pl.ds(r0, nb), CLUSTER_STATES:]
            nxr = ar * xr - ai * xi + br
            nxi = ar * xi + ai * xr + bi
            bu_ref[pl.ds(r0, nb), :CLUSTER_STATES] = nxr
            bu_ref[pl.ds(r0, nb), CLUSTER_STATES:] = nxi
            return nxr, nxi

        xr, xi = lax.fori_loop(0, tc, step, (carry_ref[c, :, :CLUSTER_STATES], carry_ref[c, :, CLUSTER_STATES:]),
                               unroll=8)
        carry_ref[c, :, :CLUSTER_STATES] = xr
        carry_ref[c, :, CLUSTER_STATES:] = xi
        y_ref[:, c * LANES:(c + 1) * LANES] = jnp.dot(bu_ref[...].astype(BF16), cc_ref[c], preferred_element_type=F32)

    y = y_ref[...] + d_ref[...] * ut_ref[...].astype(F32)
    g = jax.nn.gelu(y)
    z = g * jax.nn.sigmoid(jnp.dot(g.astype(BF16), wglu_ref[...], preferred_element_type=F32))
    yb = jnp.dot(z.astype(BF16), wo_ref[...], preferred_element_type=F32).astype(BF16)
    for j in range(nsub):
        blk = jnp.dot(permt_ref[...], yb[j * rows_sub:(j + 1) * rows_sub, :], preferred_element_type=F32)
        o_ref[:, j * SUB_T:(j + 1) * SUB_T, :] = blk.astype(o_ref.dtype).reshape(nb, SUB_T, o_ref.shape[2])


def _s5(proj3, perm, permt, bb, cc, a_bc, d, wglu, wo, *, tc=128):
    b, s, _ = proj3.shape
    rows = b * tc
    dm = wo.shape[1]
    c2 = lambda shape: pl.BlockSpec(shape, lambda i: (0, 0))
    c3 = lambda shape: pl.BlockSpec(shape, lambda i: (0, 0, 0))
    return pl.pallas_call(
        functools.partial(_s5_kernel, tc=tc),
        grid=(s // tc,),
        in_specs=[
            pl.BlockSpec((b, tc, S5_WIDTH), lambda i: (0, i, 4)),
            c2(perm.shape), c2(permt.shape), c3(bb.shape), c3(cc.shape), c3(a_bc.shape), c2(d.shape),
            c2(wglu.shape), c2(wo.shape),
        ],
        out_specs=pl.BlockSpec((b, tc, dm), lambda i: (0, i, 0)),
        out_shape=jax.ShapeDtypeStruct((b, s, dm), BF16),
        scratch_shapes=[
            pltpu.VMEM((rows, S5_WIDTH), BF16),
            pltpu.VMEM((rows, 2 * CLUSTER_STATES), F32),
            pltpu.VMEM((rows, S5_WIDTH), F32),
            pltpu.VMEM((N_CLUSTERS, b, 2 * CLUSTER_STATES), F32),
        ],
        compiler_params=_cparams(("arbitrary",)),
        name="s5",
    )(proj3, perm, permt, bb, cc, a_bc, d, wglu, wo)


def _merge_kernel(o_ref, gl_ref, bg_ref, yb_ref, w_ref, m_ref):
    d = o_ref.shape[1]
    ya = jnp.dot(o_ref[...], w_ref[...], preferred_element_type=F32)
    ga = jax.nn.sigmoid(gl_ref[:, :d].astype(F32) + bg_ref[:, :d])
    gb = jax.nn.sigmoid(gl_ref[:, d:].astype(F32) + bg_ref[:, d:])
    m_ref[...] = (ga * ya + gb * yb_ref[...].astype(F32)).astype(m_ref.dtype)


def _merge(o2, proj, bg, yb2, w, *, tm=512):
    t, d = o2.shape
    return pl.pallas_call(
        _merge_kernel,
        grid=(t // tm,),
        in_specs=[
            pl.BlockSpec((tm, d), lambda i: (i, 0)),
            pl.BlockSpec((tm, 2 * d), lambda i: (i, 0)),
            pl.BlockSpec((1, 2 * d), lambda i: (0, 0)),
            pl.BlockSpec((tm, d), lambda i: (i, 0)),
            pl.BlockSpec((d, d), lambda i: (0, 0)),
        ],
        out_specs=pl.BlockSpec((tm, d), lambda i: (i, 0)),
        out_shape=jax.ShapeDtypeStruct((t, d), BF16),
        compiler_params=_cparams(("parallel",)),
        name="merge",
    )(o2, proj, bg, yb2, w)


def _to_row_tiles(x):
    return x.reshape(x.shape[0], x.shape[1] // LANES, LANES)


def _from_row_tiles(x):
    return x.reshape(x.shape[0], x.shape[1] * LANES)


def _outproj_kernel(m_ref, x_ref, w_ref, g_ref, wrh_ref, wrl_ref, br_ref, x1_ref, h2_ref, idx_ref, wt_ref):
    x1 = x_ref[...] + jnp.dot(m_ref[...], w_ref[...], preferred_element_type=F32)
    x1_ref[...] = x1
    ms = jnp.mean(x1 * x1, axis=-1, keepdims=True)
    h2 = x1 * lax.rsqrt(ms + EPS) * g_ref[...]
    h_hi = h2.astype(BF16)
    h2_ref[...] = _to_row_tiles(h_hi)
    h_lo = (h2 - h_hi.astype(F32)).astype(BF16)
    logits = (jnp.dot(h_hi, wrh_ref[...], preferred_element_type=F32)
              + jnp.dot(h_lo, wrh_ref[...], preferred_element_type=F32)
              + jnp.dot(h_hi, wrl_ref[...], preferred_element_type=F32)) + br_ref[...]
    lane = lax.broadcasted_iota(jnp.int32, logits.shape, 1)
    work = jnp.where(lane < N_EXPERTS, logits, -jnp.inf)
    idx_out = jnp.zeros(logits.shape, jnp.int32)
    val_out = jnp.zeros(logits.shape, F32)
    v0 = None
    denom = None
    for k in range(TOP_K):
        mx = jnp.max(work, axis=-1, keepdims=True)
        sel = jnp.min(jnp.where(work == mx, lane, LANES), axis=-1, keepdims=True)
        if k == 0:
            v0 = mx
        e = jnp.exp(mx - v0)
        denom = e if k == 0 else denom + e
        idx_out = jnp.where(lane == k, sel, idx_out)
        val_out = jnp.where(lane == k, e, val_out)
        work = jnp.where(lane == sel, -jnp.inf, work)
    idx_ref[...] = idx_out
    wt_ref[...] = val_out / denom


def _outproj(m2, x2, w, g, wrh, wrl, br, *, tm=512):
    t, d = x2.shape
    c2 = lambda shape: pl.BlockSpec(shape, lambda i: (0, 0))
    row = lambda width: pl.BlockSpec((tm, width), lambda i: (i, 0))
    return pl.pallas_call(
        _outproj_kernel,
        grid=(t // tm,),
        in_specs=[row(d), row(d), c2((d, d)), c2((1, d)), c2((d, LANES)), c2((d, LANES)), c2((1, LANES))],
        out_specs=[row(d), pl.BlockSpec((tm, d // LANES, LANES), lambda i: (i, 0, 0)), row(LANES), row(LANES)],
        out_shape=[
            jax.ShapeDtypeStruct((t, d), F32),
            jax.ShapeDtypeStruct((t, d // LANES, LANES), BF16),
            jax.ShapeDtypeStruct((t, LANES), jnp.int32),
            jax.ShapeDtypeStruct((t, LANES), F32),
        ],
        compiler_params=_cparams(("parallel",)),
        name="outproj",
    )(m2, x2, w, g, wrh, wrl, br)


def _slot_tokens_kernel(slot_ref, o_ref, *, chunk):
    i = pl.program_id(0)

    @pl.when(i == 0)
    def _():
        def clear(j, c):
            o_ref[j] = 0
            return c

        lax.fori_loop(0, o_ref.shape[0], clear, 0, unroll=32)

    tokens = chunk // TOP_K
    first_tok = i * tokens

    def place(t, c):
        for k in range(TOP_K):
            o_ref[slot_ref[t * TOP_K + k]] = first_tok + t
        return c

    lax.fori_loop(0, tokens, place, 0, unroll=4)


def _slot_tokens(slot, n_slots, *, chunk=8192):
    n = slot.shape[0]
    chunk = min(chunk, n)
    return pl.pallas_call(
        functools.partial(_slot_tokens_kernel, chunk=chunk),
        grid=(n // chunk,),
        in_specs=[pl.BlockSpec((chunk,), lambda i: (i,), memory_space=pltpu.SMEM)],
        out_specs=pl.BlockSpec(memory_space=pltpu.SMEM),
        out_shape=jax.ShapeDtypeStruct((n_slots,), jnp.int32),
        compiler_params=_cparams(("arbitrary",)),
        name="slot_tokens",
    )(slot)


BOTH_DMA_PRIORITIES = (0, 1)
ROW_QUARTERS = 4


def _start_row_gather(idx_ref, n_rows, src_hbm, dst_ref, sem, priorities=BOTH_DMA_PRIORITIES):
    n_pri = len(priorities)

    def issue(r2, c):
        for u, pri in enumerate(priorities):
            r = r2 * n_pri + u
            pltpu.make_async_copy(src_hbm.at[pl.ds(idx_ref[0, 0, r], 1)], dst_ref.at[pl.ds(r, 1)], sem).start(priority=pri)
        return c

    lax.fori_loop(0, n_rows // n_pri, issue, 0, unroll=8 // n_pri)


def _wait_rows(dst_ref, sem):
    pltpu.make_async_copy(dst_ref, dst_ref, sem).wait()


def _expert_kernel(be_ref, nu_ref, nv_ref, tokc_ref, tokn_ref, h_hbm, wg_ref, bg_ref, wu_ref, bu_ref, wd_ref, bd_ref,
                   o_ref, xbuf_ref, xb_ref, acc_ref, sem, *, tm):
    b = pl.program_id(0)
    f = pl.program_id(1)
    nxt = b + 1

    @pl.when(jnp.logical_and(f == 0, jnp.logical_and(b == 0, nu_ref[0] > 0)))
    def _():
        _start_row_gather(tokc_ref, tm, h_hbm, xbuf_ref.at[0], sem.at[0], priorities=(0,))

    @pl.when(jnp.logical_and(f == 0, jnp.logical_and(nxt < pl.num_programs(0), nxt < nu_ref[0])))
    def _():
        _start_row_gather(tokn_ref, tm, h_hbm, xbuf_ref.at[nxt % 2], sem.at[nxt % 2], priorities=(0,))

    @pl.when(b < nu_ref[0])
    def _():
        @pl.when(f == 0)
        def _():
            _wait_rows(xbuf_ref.at[b % 2], sem.at[b % 2])
            xb_ref[...] = _from_row_tiles(xbuf_ref[b % 2])
            acc_ref[...] = jnp.zeros(acc_ref.shape, F32)

        def ffn(rows):
            xb = xb_ref[:rows]
            gate = jnp.dot(xb, wg_ref[0], preferred_element_type=F32) + bg_ref[0]
            up = jnp.dot(xb, wu_ref[0], preferred_element_type=F32) + bu_ref[0]
            gate = jnp.minimum(gate, SWIGLU_LIMIT)
            up = jnp.clip(up, -SWIGLU_LIMIT, SWIGLU_LIMIT)
            glu = gate * jax.nn.sigmoid(SWIGLU_ALPHA * gate)
            act = ((up + 1.0) * glu).astype(BF16)
            acc_ref[:rows] += jnp.dot(act, wd_ref[0].astype(BF16), preferred_element_type=F32)

            @pl.when(f == pl.num_programs(1) - 1)
            def _():
                o_ref[:rows] = _to_row_tiles((acc_ref[:rows] + bd_ref[0]).astype(o_ref.dtype))
                if rows < tm:
                    o_ref[rows:] = jnp.zeros((tm - rows,) + o_ref.shape[1:], o_ref.dtype)

        nv = nv_ref[b]
        quarter = tm // ROW_QUARTERS
        for i in range(1, ROW_QUARTERS + 1):
            lo_ok = nv > (i - 1) * quarter if i > 1 else True
            hi_ok = nv <= i * quarter if i < ROW_QUARTERS else True
            pl.when(jnp.logical_and(lo_ok, hi_ok))(functools.partial(ffn, i * quarter))

    @pl.when(jnp.logical_and(b >= nu_ref[0], f == 0))
    def _():
        o_ref[...] = jnp.zeros(o_ref.shape, o_ref.dtype)


def _experts(block_expert, nused, nvalid, slot_tok, h2, wg, bg, wu, bu, wd, bd, *, tm, tf=1024):
    n_slots = slot_tok.shape[0]
    tile = h2.shape[1:]
    d = tile[0] * tile[1]
    nb = n_slots // tm
    dff = wg.shape[2]
    nf = dff // tf
    tok3 = slot_tok.reshape(nb, 1, tm)

    def blk(b, nu):
        return jnp.minimum(b, nu[0] - 1)

    def fidx(b, f, nu):
        return jnp.where(b < nu[0], f, nf - 1)

    return pl.pallas_call(
        functools.partial(_expert_kernel, tm=tm),
        grid_spec=pltpu.PrefetchScalarGridSpec(
            num_scalar_prefetch=3,
            grid=(nb, nf),
            in_specs=[
                pl.BlockSpec((1, 1, tm), lambda b, f, be, nu, nv: (b, 0, 0), memory_space=pltpu.SMEM),
                pl.BlockSpec((1, 1, tm), lambda b, f, be, nu, nv: (jnp.minimum(b + 1, nb - 1), 0, 0),
                             memory_space=pltpu.SMEM),
                pl.BlockSpec(memory_space=pl.ANY),
                pl.BlockSpec((1, d, tf), lambda b, f, be, nu, nv: (be[blk(b, nu)], 0, fidx(b, f, nu))),
                pl.BlockSpec((1, 1, tf), lambda b, f, be, nu, nv: (be[blk(b, nu)], 0, fidx(b, f, nu))),
                pl.BlockSpec((1, d, tf), lambda b, f, be, nu, nv: (be[blk(b, nu)], 0, fidx(b, f, nu))),
                pl.BlockSpec((1, 1, tf), lambda b, f, be, nu, nv: (be[blk(b, nu)], 0, fidx(b, f, nu))),
                pl.BlockSpec((1, tf, d), lambda b, f, be, nu, nv: (be[blk(b, nu)], fidx(b, f, nu), 0)),
                pl.BlockSpec((1, 1, d), lambda b, f, be, nu, nv: (be[blk(b, nu)], 0, 0)),
            ],
            out_specs=pl.BlockSpec((tm,) + tile, lambda b, f, be, nu, nv: (b, 0, 0)),
            scratch_shapes=[pltpu.VMEM((2, tm) + tile, h2.dtype), pltpu.VMEM((tm, d), BF16), pltpu.VMEM((tm, d), F32),
                            pltpu.SemaphoreType.DMA((2,))],
        ),
        out_shape=jax.ShapeDtypeStruct((n_slots,) + tile, BF16),
        compiler_params=_cparams(("arbitrary", "arbitrary")),
        name="experts",
    )(block_expert, nused, nvalid, tok3, tok3, h2, wg, bg, wu, bu, wd, bd)


def _combine_kernel(posc_ref, posn_ref, x1_ref, wt_ref, ys_hbm, o_ref, buf_ref, sem, *, tm):
    i = pl.program_id(0)
    nxt = i + 1

    @pl.when(i == 0)
    def _():
        _start_row_gather(posc_ref, TOP_K * tm, ys_hbm, buf_ref.at[0], sem.at[0])

    @pl.when(nxt < pl.num_programs(0))
    def _():
        _start_row_gather(posn_ref, TOP_K * tm, ys_hbm, buf_ref.at[nxt % 2], sem.at[nxt % 2])

    slot = i % 2
    _wait_rows(buf_ref.at[slot], sem.at[slot])
    acc = x1_ref[...]
    for k in range(TOP_K):
        rows = buf_ref[slot, k * tm:(k + 1) * tm]
        acc = acc + wt_ref[:, k:k + 1] * _from_row_tiles(rows).astype(F32)
    o_ref[...] = acc


def _combine(pos_km, x1, wt, ys, *, tm=128):
    t, d = x1.shape
    n = t // tm
    return pl.pallas_call(
        functools.partial(_combine_kernel, tm=tm),
        grid=(n,),
        in_specs=[
            pl.BlockSpec((1, 1, TOP_K * tm), lambda i: (i, 0, 0), memory_space=pltpu.SMEM),
            pl.BlockSpec((1, 1, TOP_K * tm), lambda i: (jnp.minimum(i + 1, n - 1), 0, 0), memory_space=pltpu.SMEM),
            pl.BlockSpec((tm, d), lambda i: (i, 0)),
            pl.BlockSpec((tm, LANES), lambda i: (i, 0)),
            pl.BlockSpec(memory_space=pl.ANY),
        ],
        out_specs=pl.BlockSpec((tm, d), lambda i: (i, 0)),
        out_shape=jax.ShapeDtypeStruct((t, d), F32),
        scratch_shapes=[pltpu.VMEM((2, TOP_K * tm) + ys.shape[1:], ys.dtype), pltpu.SemaphoreType.DMA((2,))],
        compiler_params=_cparams(("arbitrary",)),
        name="combine",
    )(pos_km, pos_km, x1, wt, ys)


def _pad_heads(w, per_head):
    lead = w.shape[:-1]
    w = w.reshape(lead + (N_HEADS, per_head))
    w = jnp.pad(w, [(0, 0)] * len(lead) + [(0, 0), (0, HEAD_PAD - per_head)])
    return w.reshape(lead + (N_HEADS * HEAD_PAD,))


def _s5_discretise(lam_re, lam_im, log_dt, b_re, b_im):
    dt = jnp.exp(log_dt.astype(F32))[:, None]
    lr, li = lam_re.astype(F32), lam_im.astype(F32)
    mag = jnp.exp(lr * dt)
    ar, ai = mag * jnp.cos(li * dt), mag * jnp.sin(li * dt)
    den = lr * lr + li * li
    zr = ((ar - 1.0) * lr + ai * li) / den
    zi = (ai * lr - (ar - 1.0) * li) / den
    br, bi = b_re.astype(F32), b_im.astype(F32)
    bbr = zr[..., None] * br - zi[..., None] * bi
    bbi = zr[..., None] * bi + zi[..., None] * br
    return ar, ai, bbr, bbi


def _s5_pack(ar, ai, bbr, bbi, c_re, c_im, nb):
    eye = jnp.eye(S5_CLUSTER, dtype=F32)

    def pack_b(m):
        m4 = m.reshape(N_CLUSTERS, S5_CLUSTER, S5_STATE, S5_GROUP)
        return jnp.einsum('xgpc,gh->xgchp', m4, eye).reshape(N_CLUSTERS, S5_CLUSTER * S5_GROUP, CLUSTER_STATES)

    def pack_c(m):
        m4 = m.reshape(N_CLUSTERS, S5_CLUSTER, S5_GROUP, S5_STATE)
        return jnp.einsum('xgcp,gh->xgphc', m4, eye).reshape(N_CLUSTERS, CLUSTER_STATES, S5_CLUSTER * S5_GROUP)

    bb = jnp.concatenate([pack_b(bbr), pack_b(bbi)], axis=2).astype(BF16)
    cc = jnp.concatenate([pack_c(c_re.astype(F32)), -pack_c(c_im.astype(F32))], axis=1).astype(BF16)
    a = jnp.concatenate([ar.reshape(N_CLUSTERS, CLUSTER_STATES), ai.reshape(N_CLUSTERS, CLUSTER_STATES)], axis=1)
    a_bc = jnp.broadcast_to(a[:, None, :], (N_CLUSTERS, nb, 2 * CLUSTER_STATES))
    return bb, cc, a_bc


def _perm_matrix(nb):
    n = nb * SUB_T
    p = np.zeros((n, n), np.float32)
    for b in range(nb):
        for t in range(SUB_T):
            p[t * nb + b, b * SUB_T + t] = 1.0
    return p


def kernel(x, positions, norm1_g, w_in, b_gates, q_norm_g, w_uq, kv_norm_g, w_ukv, qk_norm_q_g, qk_norm_k_g, w_o_mla, s5_lambda_re, s5_lambda_im, s5_log_dt, s5_b_re, s5_b_im, s5_c_re, s5_c_im, s5_d, w_glu, w_o_s5, w_out, norm2_g, w_router, b_router, w_gate, b_gate, w_up, b_up, w_down, b_down):
    bsz, seq, d = x.shape
    t = bsz * seq
    depth = norm1_g.shape[0]
    o1 = Q_LORA_RANK
    o2 = o1 + KV_LORA_RANK
    o3 = o2 + QK_ROPE_DIM
    o4 = o3 + S5_WIDTH
    half = ROPE_HALF
    inv_freq = ROPE_THETA ** (-jnp.arange(half, dtype=F32) / half)
    lane = np.arange(LANES)
    freq = jnp.where(lane < QK_ROPE_DIM, jnp.tile(inv_freq, LANES // half), 0.0).reshape(1, LANES).astype(F32)
    sign = jnp.asarray(np.where(lane % QK_ROPE_DIM < half, -1.0, 1.0).reshape(1, LANES), F32)
    pos = positions.reshape(t, 1).astype(F32)
    sm_scale = math.log2(math.e) / math.sqrt(QK_HEAD_DIM)
    ind_q = jnp.asarray(np.equal.outer(np.arange(N_HEADS * HEAD_PAD) // HEAD_PAD, lane), BF16)
    ind_k = jnp.asarray(np.equal.outer(np.arange(N_HEADS * QK_NOPE_DIM) // QK_NOPE_DIM, lane), BF16)
    perm_np = _perm_matrix(bsz)
    perm = jnp.asarray(perm_np, BF16)
    permt = jnp.asarray(perm_np.T, BF16)
    tm_e = 512
    n_assign = t * TOP_K
    nb_e = n_assign // tm_e + N_EXPERTS
    n_slots = nb_e * tm_e
    tm_c = 128

    for l in range(depth):
        wi = w_in[l]
        w_in_p = jnp.concatenate(
            [wi[:, o4:], wi[:, o3:o4], wi[:, :o1], wi[:, o1:o2], wi[:, o2:o3],
             jnp.zeros((d, PROJ_W - wi.shape[1]), wi.dtype)], axis=1).astype(BF16)
        w_uq_p = _pad_heads(w_uq[l], QK_HEAD_DIM).astype(BF16)
        gq_full = _pad_heads(jnp.tile(qk_norm_q_g[l].astype(F32), N_HEADS) * sm_scale, QK_HEAD_DIM).reshape(1, -1)
        wkv = w_ukv[l].reshape(KV_LORA_RANK, N_HEADS, QK_NOPE_DIM + V_HEAD_DIM)
        w_kv_p = jnp.concatenate(
            [wkv[:, :, :QK_NOPE_DIM].reshape(KV_LORA_RANK, -1), wkv[:, :, QK_NOPE_DIM:].reshape(KV_LORA_RANK, -1)],
            axis=1).astype(BF16)
        gk = qk_norm_k_g[l].astype(F32)
        gk_nope = gk[:QK_NOPE_DIM].reshape(1, LANES)
        gk_rope = jnp.pad(gk[QK_NOPE_DIM:], (0, LANES - QK_ROPE_DIM)).reshape(1, LANES)
        ar, ai, bbr, bbi = _s5_discretise(s5_lambda_re[l], s5_lambda_im[l], s5_log_dt[l], s5_b_re[l], s5_b_im[l])
        bb, cc, a_bc = _s5_pack(ar, ai, bbr, bbi, s5_c_re[l], s5_c_im[l], bsz)
        wr = jnp.pad(w_router[l].astype(F32), ((0, 0), (0, LANES - N_EXPERTS)))
        wr_hi = wr.astype(BF16)
        wr_lo = (wr - wr_hi.astype(F32)).astype(BF16)
        br = jnp.pad(b_router[l].astype(F32), (0, LANES - N_EXPERTS)).reshape(1, LANES)

        x2 = x.reshape(t, d)
        proj = _inproj(x2, norm1_g[l].reshape(1, d), w_in_p)
        q = _qprep(proj, pos, q_norm_g[l].reshape(1, -1), w_uq_p, gq_full, ind_q, freq, sign)
        k, v = _kprep(proj, pos, kv_norm_g[l].reshape(1, -1), w_kv_p, gk_nope, gk_rope, ind_k, freq, sign)
        o, wg_b, wu_b = _attention(q.reshape(bsz, seq, -1), k.reshape(bsz, seq, -1), v.reshape(bsz, seq, -1),
                                   (w_gate[l], w_up[l]))
        yb = _s5(proj.reshape(bsz, seq, PROJ_W), perm, permt, bb, cc, a_bc, s5_d[l].reshape(1, -1).astype(F32),
                 w_glu[l].astype(BF16), w_o_s5[l].astype(BF16))
        m = _merge(o.reshape(t, -1), proj, b_gates[l].reshape(1, -1).astype(F32), yb.reshape(t, d),
                   w_o_mla[l].astype(BF16))
        x1, h2, top_idx, top_w = _outproj(m, x2, w_out[l].astype(BF16), norm2_g[l].reshape(1, d), wr_hi, wr_lo, br)

        flat_e = top_idx[:, :TOP_K].reshape(-1)
        onehot = (flat_e[:, None] == jnp.arange(N_EXPERTS, dtype=jnp.int32)[None, :]).astype(jnp.int32)
        csum = jnp.cumsum(onehot, axis=0)
        rank = jnp.take_along_axis(csum, flat_e[:, None], axis=1)[:, 0] - 1
        counts = csum[-1]
        nblk = (counts + tm_e - 1) // tm_e
        blk_end = jnp.cumsum(nblk)
        blk_start = blk_end - nblk
        slot = blk_start[flat_e] * tm_e + rank
        nused = blk_end[-1:].astype(jnp.int32)
        block_expert = jnp.minimum(
            jnp.sum((blk_end[None, :] <= jnp.arange(nb_e, dtype=jnp.int32)[:, None]).astype(jnp.int32), axis=1),
            N_EXPERTS - 1).astype(jnp.int32)
        slot_tok = _slot_tokens(slot.astype(jnp.int32), n_slots)
        pos_km = slot.reshape(t // tm_c, tm_c, TOP_K).transpose(0, 2, 1).reshape(t // tm_c, 1, TOP_K * tm_c)

        blk_ids = jnp.arange(nb_e, dtype=jnp.int32)
        nvalid = jnp.clip(counts[block_expert] - (blk_ids - blk_start[block_expert]) * tm_e, 0, tm_e).astype(jnp.int32)
        ys = _experts(block_expert, nused, nvalid, slot_tok, h2,
                      wg_b, b_gate[l].reshape(N_EXPERTS, 1, -1).astype(F32),
                      wu_b, b_up[l].reshape(N_EXPERTS, 1, -1).astype(F32),
                      w_down[l], b_down[l].reshape(N_EXPERTS, 1, -1).astype(F32), tm=tm_e)
        x = _combine(pos_km, x1, top_w, ys, tm=tm_c).reshape(bsz, seq, d)
    return x
```
